```python
import math
import jax, jax.numpy as jnp
from jax import lax
import numpy as np

D_MODEL = 1024
BATCH = 8
SEQ = 2048
DEPTH = 1
DEC_BATCH = 128
DEC_SEQ = 8
PAST_LEN = 16384
PAGE_SIZE = 128

S5_WIDTH = D_MODEL // 2
S5_GROUP = 16
S5_GROUPS = S5_WIDTH // S5_GROUP
S5_STATE = 64
RWKV_WIDTH = D_MODEL - S5_WIDTH
RWKV_HEAD = 64
RWKV_HEADS = RWKV_WIDTH // RWKV_HEAD
W_LORA = 64
A_LORA = 64
G_LORA = 128
RWKV_COLS = 3 * RWKV_WIDTH + W_LORA + A_LORA + G_LORA
IN_COLS = S5_WIDTH + RWKV_COLS
PEER_HEADS = 8
N_KEYS = 128
N_EXPERTS = N_KEYS * N_KEYS
PEER_TOPK = 16
PEER_KEY_DIM = 128
PEER_HALF = PEER_KEY_DIM // 2
PEER_BLOCK = 128
NORM_EPS = 1e-6
GN_EPS = 64e-5

kernel_name = "hybrid_s5_rwkv7_peer_adaln_step"

F32 = jnp.float32


def rmsnorm(x, g):
    x32 = x.astype(F32)
    return x32 * lax.rsqrt(jnp.mean(x32 * x32, axis=-1, keepdims=True) + NORM_EPS) * g.astype(F32)


def s5_mixer(u, h_re0, h_im0, a_re, a_im, log_dt, b_re, b_im, c_re, c_im, d_skip, w_glu, b_glu):
    nb, T, _ = u.shape
    u32 = u.astype(F32)
    ug = u32.reshape(nb, T, S5_GROUPS, S5_GROUP)
    dt = jnp.exp(log_dt.astype(F32))[:, None]
    lam_re, lam_im = a_re.astype(F32), a_im.astype(F32)
    mag = jnp.exp(lam_re * dt)
    ang = lam_im * dt
    lb_re, lb_im = mag * jnp.cos(ang), mag * jnp.sin(ang)
    den = lam_re * lam_re + lam_im * lam_im
    n_re, n_im = lb_re - 1.0, lb_im
    coef_re = (n_re * lam_re + n_im * lam_im) / den
    coef_im = (n_im * lam_re - n_re * lam_im) / den
    b_re32, b_im32 = b_re.astype(F32), b_im.astype(F32)
    bb_re = coef_re[..., None] * b_re32 - coef_im[..., None] * b_im32
    bb_im = coef_re[..., None] * b_im32 + coef_im[..., None] * b_re32
    bu_re = jnp.einsum('btgh,gph->tbgp', ug, bb_re)
    bu_im = jnp.einsum('btgh,gph->tbgp', ug, bb_im)
    h_re0, h_im0 = h_re0.astype(F32), h_im0.astype(F32)
    bu_re = bu_re.at[0].add(lb_re * h_re0 - lb_im * h_im0)
    bu_im = bu_im.at[0].add(lb_re * h_im0 + lb_im * h_re0)
    a_re_t = jnp.broadcast_to(lb_re, (T, 1, S5_GROUPS, S5_STATE))
    a_im_t = jnp.broadcast_to(lb_im, (T, 1, S5_GROUPS, S5_STATE))

    def combine(left, right):
        ar1, ai1, br1, bi1 = left
        ar2, ai2, br2, bi2 = right
        return (ar2 * ar1 - ai2 * ai1, ar2 * ai1 + ai2 * ar1,
                ar2 * br1 - ai2 * bi1 + br2, ar2 * bi1 + ai2 * br1 + bi2)

    _, _, hs_re, hs_im = lax.associative_scan(combine, (a_re_t, a_im_t, bu_re, bu_im), axis=0)
    y = (jnp.einsum('ghp,tbgp->btgh', c_re.astype(F32), hs_re)
         - jnp.einsum('ghp,tbgp->btgh', c_im.astype(F32), hs_im))
    y = y.reshape(nb, T, S5_WIDTH) + d_skip.astype(F32) * u32
    y = jax.nn.gelu(y, approximate=False)
    y = y * jax.nn.sigmoid(y @ w_glu + b_glu)
    return y, hs_re[-1], hs_im[-1]


def rwkv7_mixer(p, shift0, S0, mu, w0, w2, a0, a2, g2, k_k, k_a, r_k, gn_w, gn_b):
    nb, T, _ = p.shape
    p = p.astype(F32)
    p_prev = jnp.concatenate([shift0.astype(F32)[:, None, :], p[:, :-1]], axis=1)
    ps = p + (p_prev - p) * mu.astype(F32)
    o1 = RWKV_WIDTH
    r, k, v, w_lo, a_lo, g_lo = jnp.split(
        ps, [o1, 2 * o1, 3 * o1, 3 * o1 + W_LORA, 3 * o1 + W_LORA + A_LORA], axis=-1)
    w_raw = -jax.nn.softplus(-(w0 + jnp.tanh(w_lo) @ w2)) - 0.5
    decay = jnp.exp(-jnp.exp(w_raw))
    a = jax.nn.sigmoid(a0 + a_lo @ a2)
    g = jax.nn.sigmoid(g_lo) @ g2

    def heads(t):
        return t.reshape(nb, T, RWKV_HEADS, RWKV_HEAD)

    kk = heads(k * k_k)
    kk = kk / jnp.maximum(jnp.linalg.norm(kk, axis=-1, keepdims=True), 1e-12)
    k = k * (1.0 + (a - 1.0) * k_a)
    r_h, w_h, k_h, v_h, a_h = heads(r), heads(decay), heads(k), heads(v), heads(a)

    def step(S, inp):
        r_t, w_t, k_t, v_t, kk_t, a_t = inp
        sa = jnp.einsum('bhvk,bhk->bhv', S, -kk_t)
        S = (S * w_t[:, :, None, :] + sa[..., None] * (kk_t * a_t)[:, :, None, :]
             + v_t[..., None] * k_t[:, :, None, :])
        return S, jnp.einsum('bhvk,bhk->bhv', S, r_t)

    xs = (jnp.swapaxes(r_h, 0, 1), jnp.swapaxes(w_h, 0, 1), jnp.swapaxes(k_h, 0, 1),
          jnp.swapaxes(v_h, 0, 1), jnp.swapaxes(kk, 0, 1), jnp.swapaxes(a_h, 0, 1))
    S_T, y = lax.scan(step, S0.astype(F32), xs)
    y = jnp.swapaxes(y, 0, 1)
    mean = jnp.mean(y, axis=-1, keepdims=True)
    var = jnp.mean(jnp.square(y - mean), axis=-1, keepdims=True)
    y = (y - mean) * lax.rsqrt(var + GN_EPS)
    y = y * gn_w.reshape(RWKV_HEADS, RWKV_HEAD) + gn_b.reshape(RWKV_HEADS, RWKV_HEAD)
    y = y + jnp.sum(r_h * k_h * r_k, axis=-1, keepdims=True) * v_h
    y = y.reshape(nb, T, RWKV_WIDTH) * g
    return y, S_T, p[:, -1]


def peer_ffn(h, w_q, keys1, keys2, u_tab, v_tab):
    T = h.shape[0]
    n_blk = -(-T // PEER_BLOCK)
    hb = jnp.pad(h, ((0, n_blk * PEER_BLOCK - T), (0, 0))).reshape(n_blk, PEER_BLOCK, D_MODEL)
    k1 = keys1.astype(F32)
    k2 = keys2.astype(F32)

    def block(xb):
        q = (xb @ w_q).astype(F32).reshape(PEER_BLOCK, PEER_HEADS, 2, PEER_HALF)
        s1 = jnp.einsum('thd,hnd->thn', q[:, :, 0], k1)
        s2 = jnp.einsum('thd,hnd->thn', q[:, :, 1], k2)
        v1, i1 = lax.top_k(s1, PEER_TOPK)
        v2, i2 = lax.top_k(s2, PEER_TOPK)
        cand = (v1[..., :, None] + v2[..., None, :]).reshape(PEER_BLOCK, PEER_HEADS, PEER_TOPK * PEER_TOPK)
        cidx = (i1[..., :, None] * N_KEYS + i2[..., None, :]).reshape(PEER_BLOCK, PEER_HEADS, PEER_TOPK * PEER_TOPK)
        sc, pos = lax.top_k(cand, PEER_TOPK)
        idx = jnp.take_along_axis(cidx, pos, axis=-1)
        gate = jax.nn.softmax(sc, axis=-1)
        act = jax.nn.gelu(jnp.einsum('thkd,td->thk', jnp.take(u_tab, idx, axis=0), xb), approximate=False)
        return jnp.einsum('thk,thkd->td', gate * act, jnp.take(v_tab, idx, axis=0))

    return lax.map(block, hb).reshape(n_blk * PEER_BLOCK, D_MODEL)[:T]


def hybrid_layer(x, c, s5_re, s5_im, wkv, shift, prm):
    nb, T, _ = x.shape
    mod = jax.nn.silu(c.astype(F32)) @ prm["w_ada"] + prm["b_ada"]
    sh1, sc1, ga1, sh2, sc2, ga2 = jnp.split(mod[:, None, :], 6, axis=-1)
    h = rmsnorm(x, prm["norm1_g"]) * (1.0 + sc1) + sh1
    proj = h @ prm["w_in"]
    y_s5, s5_re, s5_im = s5_mixer(proj[..., :S5_WIDTH], s5_re, s5_im, prm["s5_a_re"], prm["s5_a_im"],
                                  prm["s5_log_dt"], prm["s5_b_re"], prm["s5_b_im"], prm["s5_c_re"],
                                  prm["s5_c_im"], prm["s5_d"], prm["w_glu"], prm["b_glu"])
    y_rw, wkv, shift = rwkv7_mixer(proj[..., S5_WIDTH:], shift, wkv, prm["rwkv_mu"], prm["rwkv_w0"],
                                   prm["rwkv_w2"], prm["rwkv_a0"], prm["rwkv_a2"], prm["rwkv_g2"],
                                   prm["rwkv_k_k"], prm["rwkv_k_a"], prm["rwkv_r_k"],
                                   prm["rwkv_gn_w"], prm["rwkv_gn_b"])
    x = x + ga1 * (jnp.concatenate([y_s5, y_rw], axis=-1) @ prm["w_out"])
    h = rmsnorm(x, prm["norm2_g"]) * (1.0 + sc2) + sh2
    ff = peer_ffn(h.reshape(nb * T, D_MODEL), prm["peer_w_q"], prm["peer_keys1"], prm["peer_keys2"],
                  prm["peer_u"], prm["peer_v"]).reshape(nb, T, D_MODEL)
    x = x + ga2 * ff
    return x, s5_re, s5_im, wkv, shift


def setup_inputs(seed: int = 0) -> dict:
    key = jax.random.key(seed)
    ks = iter(jax.random.split(key, 48))

    def nrm(shape, scale):
        return scale * jax.random.normal(next(ks), shape, F32)

    L = DEPTH
    d = D_MODEL
    a_im = jnp.broadcast_to(jnp.pi * jnp.arange(S5_STATE, dtype=F32), (L, S5_GROUPS, S5_STATE))
    return {
        "x_prompt": nrm((BATCH, SEQ, d), 1.0),
        "x_sample": nrm((DEC_BATCH, DEC_SEQ, d), 1.0),
        "state_s5_re": nrm((L, DEC_BATCH, S5_GROUPS, S5_STATE), 0.1),
        "state_s5_im": nrm((L, DEC_BATCH, S5_GROUPS, S5_STATE), 0.1),
        "state_wkv": nrm((L, DEC_BATCH, RWKV_HEADS, RWKV_HEAD, RWKV_HEAD), 0.1),
        "state_shift": nrm((L, DEC_BATCH, RWKV_COLS), 1.0),
        "c_prompt": nrm((BATCH, d), 1.0),
        "c_sample": nrm((DEC_BATCH, d), 1.0),
        "w_ada": nrm((L, d, 6 * d), 0.5 * d ** -0.5),
        "b_ada": nrm((L, 6 * d), 0.01),
        "norm1_g": 1.0 + nrm((L, d), 0.02),
        "norm2_g": 1.0 + nrm((L, d), 0.02),
        "w_in": nrm((L, d, IN_COLS), d ** -0.5),
        "w_out": nrm((L, S5_WIDTH + RWKV_WIDTH, d), (S5_WIDTH + RWKV_WIDTH) ** -0.5),
        "s5_a_re": -0.5 + nrm((L, S5_GROUPS, S5_STATE), 0.01),
        "s5_a_im": a_im + nrm((L, S5_GROUPS, S5_STATE), 0.01),
        "s5_log_dt": jax.random.uniform(next(ks), (L, S5_GROUPS), F32, math.log(1e-3), math.log(1e-1)),
        "s5_b_re": nrm((L, S5_GROUPS, S5_STATE, S5_GROUP), (2 * S5_GROUP) ** -0.5),
        "s5_b_im": nrm((L, S5_GROUPS, S5_STATE, S5_GROUP), (2 * S5_GROUP) ** -0.5),
        "s5_c_re": nrm((L, S5_GROUPS, S5_GROUP, S5_STATE), S5_STATE ** -0.5),
        "s5_c_im": nrm((L, S5_GROUPS, S5_GROUP, S5_STATE), S5_STATE ** -0.5),
        "s5_d": nrm((L, S5_WIDTH), 1.0),
        "w_glu": nrm((L, S5_WIDTH, S5_WIDTH), S5_WIDTH ** -0.5),
        "b_glu": nrm((L, S5_WIDTH), 0.01),
        "rwkv_mu": jax.random.uniform(next(ks), (L, RWKV_COLS), F32, 0.0, 1.0),
        "rwkv_w0": jax.random.uniform(next(ks), (L, RWKV_WIDTH), F32, -6.0, -1.0),
        "rwkv_w2": nrm((L, W_LORA, RWKV_WIDTH), 0.1),
        "rwkv_a0": nrm((L, RWKV_WIDTH), 0.1),
        "rwkv_a2": nrm((L, A_LORA, RWKV_WIDTH), 0.1),
        "rwkv_g2": nrm((L, G_LORA, RWKV_WIDTH), G_LORA ** -0.5),
        "rwkv_k_k": 0.85 + nrm((L, RWKV_WIDTH), 0.05),
        "rwkv_k_a": 1.0 + nrm((L, RWKV_WIDTH), 0.05),
        "rwkv_r_k": nrm((L, RWKV_HEADS, RWKV_HEAD), 0.1),
        "rwkv_gn_w": 1.0 + nrm((L, RWKV_WIDTH), 0.02),
        "rwkv_gn_b": nrm((L, RWKV_WIDTH), 0.01),
        "peer_w_q": nrm((L, d, PEER_HEADS * PEER_KEY_DIM), d ** -0.5),
        "peer_keys1": nrm((L, PEER_HEADS, N_KEYS, PEER_HALF), PEER_HALF ** -0.5),
        "peer_keys2": nrm((L, PEER_HEADS, N_KEYS, PEER_HALF), PEER_HALF ** -0.5),
        "peer_u": nrm((L, N_EXPERTS, d), d ** -0.5),
        "peer_v": nrm((L, N_EXPERTS, d), 0.3),
        "final_norm_g": 1.0 + nrm((d,), 0.02),
    }


def reference(x_prompt, x_sample, state_s5_re, state_s5_im, state_wkv, state_shift, c_prompt, c_sample,
              w_ada, b_ada, norm1_g, norm2_g, w_in, w_out, s5_a_re, s5_a_im, s5_log_dt, s5_b_re, s5_b_im,
              s5_c_re, s5_c_im, s5_d, w_glu, b_glu, rwkv_mu, rwkv_w0, rwkv_w2, rwkv_a0, rwkv_a2, rwkv_g2,
              rwkv_k_k, rwkv_k_a, rwkv_r_k, rwkv_gn_w, rwkv_gn_b, peer_w_q, peer_keys1, peer_keys2,
              peer_u, peer_v, final_norm_g):
    nbp = x_prompt.shape[0]
    hp = x_prompt.astype(F32)
    hs = x_sample.astype(F32)
    p_re_l, p_im_l, p_wkv_l, p_sh_l = [], [], [], []
    s_re_l, s_im_l, s_wkv_l, s_sh_l = [], [], [], []
    for l in range(DEPTH):
        prm = {
            "w_ada": w_ada[l], "b_ada": b_ada[l], "norm1_g": norm1_g[l], "norm2_g": norm2_g[l],
            "w_in": w_in[l], "w_out": w_out[l], "s5_a_re": s5_a_re[l], "s5_a_im": s5_a_im[l],
            "s5_log_dt": s5_log_dt[l], "s5_b_re": s5_b_re[l], "s5_b_im": s5_b_im[l],
            "s5_c_re": s5_c_re[l], "s5_c_im": s5_c_im[l], "s5_d": s5_d[l], "w_glu": w_glu[l],
            "b_glu": b_glu[l], "rwkv_mu": rwkv_mu[l], "rwkv_w0": rwkv_w0[l], "rwkv_w2": rwkv_w2[l],
            "rwkv_a0": rwkv_a0[l], "rwkv_a2": rwkv_a2[l], "rwkv_g2": rwkv_g2[l],
            "rwkv_k_k": rwkv_k_k[l], "rwkv_k_a": rwkv_k_a[l], "rwkv_r_k": rwkv_r_k[l],
            "rwkv_gn_w": rwkv_gn_w[l], "rwkv_gn_b": rwkv_gn_b[l], "peer_w_q": peer_w_q[l],
            "peer_keys1": peer_keys1[l], "peer_keys2": peer_keys2[l], "peer_u": peer_u[l],
            "peer_v": peer_v[l],
        }
        z_s5 = jnp.zeros((nbp, S5_GROUPS, S5_STATE), F32)
        z_wkv = jnp.zeros((nbp, RWKV_HEADS, RWKV_HEAD, RWKV_HEAD), F32)
        z_sh = jnp.zeros((nbp, RWKV_COLS), F32)
        hp, pr, pi, pw, psh = hybrid_layer(hp, c_prompt, z_s5, z_s5, z_wkv, z_sh, prm)
        hs, sr, si, sw, ssh = hybrid_layer(hs, c_sample, state_s5_re[l], state_s5_im[l], state_wkv[l],
                                           state_shift[l], prm)
        p_re_l.append(pr); p_im_l.append(pi); p_wkv_l.append(pw); p_sh_l.append(psh)
        s_re_l.append(sr); s_im_l.append(si); s_wkv_l.append(sw); s_sh_l.append(ssh)
    y_prompt = rmsnorm(hp, final_norm_g).astype(x_prompt.dtype)
    y_sample = rmsnorm(hs, final_norm_g).astype(x_sample.dtype)
    s5_re_prompt = jnp.stack(p_re_l)
    s5_im_prompt = jnp.stack(p_im_l)
    wkv_prompt = jnp.stack(p_wkv_l)
    shift_prompt = jnp.stack(p_sh_l)
    s5_re_sample = jnp.stack(s_re_l)
    s5_im_sample = jnp.stack(s_im_l)
    wkv_sample = jnp.stack(s_wkv_l)
    shift_sample = jnp.stack(s_sh_l)
    return (y_prompt, y_sample, s5_re_prompt, s5_im_prompt, wkv_prompt, shift_prompt,
            s5_re_sample, s5_im_sample, wkv_sample, shift_sample)
```

```python
import functools

import jax
import jax.numpy as jnp
from jax import lax
from jax.experimental import pallas as pl
from jax.experimental.pallas import tpu as pltpu

F32 = jnp.float32
BF16 = jnp.bfloat16
HIGHEST = lax.Precision.HIGHEST

LANES = 128
SUBLANES = 8
VMEM_LIMIT_BYTES = 56 * 1024 * 1024

D_MODEL = 1024
S5_WIDTH = 512
S5_GROUP = 16
S5_GROUPS = 32
S5_STATE = 64
S5_LANES = S5_GROUPS * S5_STATE
S5_CHUNKS = 4
S5_CHUNK_IN = S5_WIDTH // S5_CHUNKS
S5_CHUNK_ST = S5_LANES // S5_CHUNKS
RWKV_WIDTH = 512
RWKV_HEAD = 64
RWKV_HEADS = 8
RWKV_LORA = 128
RWKV_COLS = 3 * RWKV_WIDTH + 64 + 64 + 128
PEER_HEADS = 8
N_KEYS = 128
PEER_TOPK = 16
PEER_HALF = 64
PEER_TOKEN_GROUP = 256
PEER_KEYS_PER_BLOCK = 8
NORM_EPS = 1e-6
GN_EPS = 64e-5


def _params(sem):
    return pltpu.CompilerParams(dimension_semantics=sem, vmem_limit_bytes=VMEM_LIMIT_BYTES)


def _full(shape):
    return pl.BlockSpec(shape, lambda *_: (0,) * len(shape))


def _dot(a, b, precision=None):
    return jnp.dot(a, b, precision=precision, preferred_element_type=F32)


def _gelu(x):
    return 0.5 * x * (1.0 + lax.erf(x * (2.0 ** -0.5)))


def _rms(x, g):
    return x * lax.rsqrt(jnp.mean(x * x, axis=-1, keepdims=True) + NORM_EPS) * g


def _modulate(h, shift, scale, n_seq):
    rows, d = h.shape
    h3 = h.reshape(rows // n_seq, n_seq, d)
    return (h3 * (1.0 + scale)[None] + shift[None]).reshape(rows, d)


def _gated(h, gate, n_seq):
    rows, d = h.shape
    return (h.reshape(rows // n_seq, n_seq, d) * gate[None]).reshape(rows, d)


def _ada_kernel(c_ref, w_ref, b_ref, o_ref):
    s = jax.nn.silu(c_ref[...])
    o_ref[...] = _dot(s, w_ref[...], HIGHEST) + b_ref[...]


def _ada_call(c_all, w_ada, b_ada):
    n, d = c_all.shape
    cols = w_ada.shape[1]
    return pl.pallas_call(
        _ada_kernel,
        grid=(cols // d,),
        in_specs=[_full((n, d)), pl.BlockSpec((d, d), lambda j: (0, j)), pl.BlockSpec((1, d), lambda j: (0, j))],
        out_specs=pl.BlockSpec((n, d), lambda j: (0, j)),
        out_shape=jax.ShapeDtypeStruct((n, cols), F32),
        compiler_params=_params(("arbitrary",)),
        name="adaln_mod",
    )(c_all, w_ada, b_ada.reshape(1, cols))


def _inproj_kernel(x_ref, mod_ref, g_ref, w_ref, u_ref, p_ref, *, n_seq):
    h = _modulate(_rms(x_ref[...], g_ref[...]), mod_ref[0], mod_ref[1], n_seq)
    proj = _dot(h.astype(BF16), w_ref[...])
    u_ref[...] = proj[:, :S5_WIDTH]
    p_ref[...] = proj[:, S5_WIDTH:]


def _inproj_call(x_rows, mod6, norm_g, w_in_bf, n_seq, tile):
    rows, d = x_rows.shape
    return pl.pallas_call(
        functools.partial(_inproj_kernel, n_seq=n_seq),
        grid=(rows // tile,),
        in_specs=[pl.BlockSpec((tile, d), lambda i: (i, 0)), _full(mod6.shape), _full((1, d)),
                  _full(w_in_bf.shape)],
        out_specs=[pl.BlockSpec((tile, S5_WIDTH), lambda i: (i, 0)),
                   pl.BlockSpec((tile, RWKV_COLS), lambda i: (i, 0))],
        out_shape=[jax.ShapeDtypeStruct((rows, S5_WIDTH), F32), jax.ShapeDtypeStruct((rows, RWKV_COLS), F32)],
        compiler_params=_params(("arbitrary",)),
        name="norm1_inproj",
    )(x_rows, mod6, norm_g.reshape(1, d), w_in_bf)


def _s5_prep_kernel(are_ref, aim_ref, ldt_ref, bre_ref, bim_ref, lbr_ref, lbi_ref, wre_ref, wim_ref):
    lam_re, lam_im = are_ref[...], aim_ref[...]
    dt = jnp.exp(ldt_ref[...])
    mag = jnp.exp(lam_re * dt)
    ang = lam_im * dt
    lb_re, lb_im = mag * jnp.cos(ang), mag * jnp.sin(ang)
    den = lam_re * lam_re + lam_im * lam_im
    n_re, n_im = lb_re - 1.0, lb_im
    coef_re = (n_re * lam_re + n_im * lam_im) / den
    coef_im = (n_im * lam_re - n_re * lam_im) / den
    lbr_ref[...] = lb_re
    lbi_ref[...] = lb_im
    for c in range(S5_CHUNKS):
        cr = coef_re[:, c * S5_CHUNK_ST:(c + 1) * S5_CHUNK_ST]
        ci = coef_im[:, c * S5_CHUNK_ST:(c + 1) * S5_CHUNK_ST]
        wre_ref[c] = cr * bre_ref[c] - ci * bim_ref[c]
        wim_ref[c] = cr * bim_ref[c] + ci * bre_ref[c]


def _s5_prep_call(a_re, a_im, log_dt, b_re, b_im):
    row = lambda a: a.reshape(1, S5_LANES)
    ldt = jnp.repeat(log_dt, S5_STATE).reshape(1, S5_LANES)
    w_shape = (S5_CHUNKS, S5_CHUNK_IN, S5_CHUNK_ST)
    return pl.pallas_call(
        _s5_prep_kernel,
        out_shape=[jax.ShapeDtypeStruct((1, S5_LANES), F32)] * 2 + [jax.ShapeDtypeStruct(w_shape, F32)] * 2,
        compiler_params=pltpu.CompilerParams(vmem_limit_bytes=VMEM_LIMIT_BYTES),
        name="s5_discretise",
    )(row(a_re), row(a_im), ldt, _s5_in_blockdiag(b_re), _s5_in_blockdiag(b_im))


def _s5_in_blockdiag(b):
    gpc = S5_GROUPS // S5_CHUNKS
    bt = jnp.transpose(b, (0, 2, 1)).reshape(S5_CHUNKS, gpc, S5_GROUP, S5_STATE)
    eye = jnp.eye(gpc, dtype=b.dtype)
    bd = bt[:, :, :, None, :] * eye[None, :, None, :, None]
    return bd.reshape(S5_CHUNKS, S5_CHUNK_IN, S5_CHUNK_ST)


def _s5_out_blockdiag(c):
    gpc = S5_GROUPS // S5_CHUNKS
    ct = jnp.transpose(c, (0, 2, 1)).reshape(S5_CHUNKS, gpc, S5_STATE, S5_GROUP)
    eye = jnp.eye(gpc, dtype=c.dtype)
    bd = ct[:, :, :, None, :] * eye[None, :, None, :, None]
    return bd.reshape(S5_CHUNKS, S5_CHUNK_ST, S5_CHUNK_IN)


def _s5_kernel(u_ref, h0r_ref, h0i_ref, lbr_ref, lbi_ref, wre_ref, wim_ref, cre_ref, cim_ref, d_ref, wglu_ref,
               bglu_ref, y_ref, hr_out, hi_out, re_s, im_s, str_s, sti_s, *, n_seq, steps):
    chunk = pl.program_id(0)

    @pl.when(chunk == 0)
    def _():
        str_s[...] = h0r_ref[...]
        sti_s[...] = h0i_ref[...]

    u = u_ref[...]
    for c in range(S5_CHUNKS):
        uc = u[:, c * S5_CHUNK_IN:(c + 1) * S5_CHUNK_IN]
        re_s[:, c * S5_CHUNK_ST:(c + 1) * S5_CHUNK_ST] = _dot(uc, wre_ref[c], HIGHEST)
        im_s[:, c * S5_CHUNK_ST:(c + 1) * S5_CHUNK_ST] = _dot(uc, wim_ref[c], HIGHEST)

    def seq_block(rb, carry):
        r0 = pl.multiple_of(rb * SUBLANES, SUBLANES)
        for c in range(S5_CHUNKS):
            lanes = slice(c * S5_CHUNK_ST, (c + 1) * S5_CHUNK_ST)
            lr = jnp.broadcast_to(lbr_ref[:, lanes], (SUBLANES, S5_CHUNK_ST))
            li = jnp.broadcast_to(lbi_ref[:, lanes], (SUBLANES, S5_CHUNK_ST))

            def step(t, h):
                hr, hi = h
                row = pl.multiple_of(t * n_seq + r0, SUBLANES)
                nr = lr * hr - li * hi + re_s[pl.ds(row, SUBLANES), lanes]
                ni = lr * hi + li * hr + im_s[pl.ds(row, SUBLANES), lanes]
                re_s[pl.ds(row, SUBLANES), lanes] = nr
                im_s[pl.ds(row, SUBLANES), lanes] = ni
                return nr, ni

            h0 = (str_s[pl.ds(r0, SUBLANES), lanes], sti_s[pl.ds(r0, SUBLANES), lanes])
            hr, hi = lax.fori_loop(0, steps, step, h0, unroll=min(steps, 8))
            str_s[pl.ds(r0, SUBLANES), lanes] = hr
            sti_s[pl.ds(r0, SUBLANES), lanes] = hi
        return carry

    lax.fori_loop(0, n_seq // SUBLANES, seq_block, 0)

    ys = []
    for c in range(S5_CHUNKS):
        lanes = slice(c * S5_CHUNK_ST, (c + 1) * S5_CHUNK_ST)
        ys.append(_dot(re_s[:, lanes], cre_ref[c], HIGHEST) - _dot(im_s[:, lanes], cim_ref[c], HIGHEST))
    y = jnp.concatenate(ys, axis=-1) + d_ref[...] * u
    y = _gelu(y)
    y_ref[...] = y * jax.nn.sigmoid(_dot(y.astype(BF16), wglu_ref[...]) + bglu_ref[...])

    @pl.when(chunk == pl.num_programs(0) - 1)
    def _():
        hr_out[...] = str_s[...]
        hi_out[...] = sti_s[...]


def _s5_call(u_rows, h0_re, h0_im, consts, n_seq, steps):
    rows = u_rows.shape[0]
    tile = n_seq * steps
    lb_re, lb_im, w_re, w_im, c_re, c_im, d_skip, w_glu_bf, b_glu = consts
    state = jax.ShapeDtypeStruct((n_seq, S5_LANES), F32)
    args = (u_rows, h0_re, h0_im, lb_re, lb_im, w_re, w_im, c_re, c_im, d_skip, w_glu_bf, b_glu)
    in_specs = [pl.BlockSpec((tile, S5_WIDTH), lambda i: (i, 0))] + [_full(a.shape) for a in args[1:]]
    return pl.pallas_call(
        functools.partial(_s5_kernel, n_seq=n_seq, steps=steps),
        grid=(rows // tile,),
        in_specs=in_specs,
        out_specs=[pl.BlockSpec((tile, S5_WIDTH), lambda i: (i, 0)), _full(state.shape), _full(state.shape)],
        out_shape=[jax.ShapeDtypeStruct((rows, S5_WIDTH), F32), state, state],
        scratch_shapes=[pltpu.VMEM((tile, S5_LANES), F32), pltpu.VMEM((tile, S5_LANES), F32),
                        pltpu.VMEM((n_seq, S5_LANES), F32), pltpu.VMEM((n_seq, S5_LANES), F32)],
        compiler_params=_params(("arbitrary",)),
        name="s5_mixer",
    )(*args)


def _head_sum(x, ones_ref):
    return _dot(x, ones_ref[...], HIGHEST)


def _rwkv_pre_kernel(p_ref, prev_ref, shift_ref, mu_ref, w0_ref, w2_ref, a0_ref, a2_ref, g2_ref, kk_ref, ka_ref,
                     ones_ref, r_out, w_out, k_out, v_out, kk_out, kka_out, g_out, *, n_seq):
    p = p_ref[...]
    tile = p.shape[0]
    head = jnp.where(pl.program_id(0) == 0, shift_ref[...], prev_ref[...])
    p_prev = head if tile == n_seq else jnp.concatenate([head, p[:tile - n_seq]], axis=0)
    ps = p + (p_prev - p) * mu_ref[...]
    w = RWKV_WIDTH
    r, k, v = ps[:, :w], ps[:, w:2 * w], ps[:, 2 * w:3 * w]
    lo = ps[:, 3 * w:3 * w + RWKV_LORA]
    g_lo = ps[:, 3 * w + RWKV_LORA:]
    w_raw = -jax.nn.softplus(-(w0_ref[...] + _dot(jnp.tanh(lo), w2_ref[...], HIGHEST))) - 0.5
    a = jax.nn.sigmoid(a0_ref[...] + _dot(lo, a2_ref[...], HIGHEST))
    kk = k * kk_ref[...]
    norm = jnp.sqrt(_head_sum(kk * kk, ones_ref))
    kk = kk / jnp.maximum(norm, 1e-12)
    r_out[...] = r
    w_out[...] = jnp.exp(-jnp.exp(w_raw))
    k_out[...] = k * (1.0 + (a - 1.0) * ka_ref[...])
    v_out[...] = v
    kk_out[...] = kk
    kka_out[...] = kk * a
    g_out[...] = _dot(jax.nn.sigmoid(g_lo), g2_ref[...], HIGHEST)


def _rwkv_pre_call(p_rows, shift0, consts, n_seq, tile):
    rows = p_rows.shape[0]
    per = tile // n_seq
    vec = jax.ShapeDtypeStruct((rows, RWKV_WIDTH), F32)
    in_specs = [pl.BlockSpec((tile, RWKV_COLS), lambda i: (i, 0)),
                pl.BlockSpec((n_seq, RWKV_COLS), lambda i: (jnp.maximum(i * per - 1, 0), 0)),
                _full(shift0.shape)] + [_full(a.shape) for a in consts]
    return pl.pallas_call(
        functools.partial(_rwkv_pre_kernel, n_seq=n_seq),
        grid=(rows // tile,),
        in_specs=in_specs,
        out_specs=[pl.BlockSpec((tile, RWKV_WIDTH), lambda i: (i, 0))] * 7,
        out_shape=[vec] * 7,
        compiler_params=_params(("arbitrary",)),
        name="rwkv_prologue",
    )(p_rows, p_rows, shift0, *consts)


def _sublane_allsum(p):
    p = p + pltpu.roll(p, 4, 0)
    p = p + pltpu.roll(p, 2, 0)
    return p + pltpu.roll(p, 1, 0)


def _rwkv_scan_kernel(r_ref, w_ref, k_ref, kk_ref, kka_ref, v_ref, s0_ref, y_ref, sout_ref, s_s, *, steps, n_v):
    tc = pl.program_id(1)

    @pl.when(tc == 0)
    def _():
        s_s[...] = s0_ref[...]

    row_id = lax.broadcasted_iota(jnp.int32, (SUBLANES, LANES), 0)
    split = lambda x: x.reshape(RWKV_HEAD // SUBLANES, SUBLANES, LANES)

    def step(t, carry):
        r, w, k, kk, kka = split(r_ref[t]), split(w_ref[t]), split(k_ref[t]), split(kk_ref[t]), split(kka_ref[t])

        def v_block(vb, carry2):
            v0 = pl.multiple_of(vb * SUBLANES, SUBLANES)
            v_tile = v_ref[t, pl.ds(v0, SUBLANES), :]
            y_tile = jnp.zeros((SUBLANES, LANES), F32)
            for j in range(SUBLANES):
                s = split(s_s[v0 + j])
                sa = -_sublane_allsum(jnp.sum(s * kk, axis=0))
                v_row = jnp.broadcast_to(v_tile[j:j + 1, :], (SUBLANES, LANES))
                s = s * w + sa[None] * kka + v_row[None] * k
                s_s[v0 + j] = s.reshape(RWKV_HEAD, LANES)
                y_tile = jnp.where(row_id == j, _sublane_allsum(jnp.sum(s * r, axis=0)), y_tile)
            y_ref[t, pl.ds(v0, SUBLANES), :] = y_tile
            return carry2

        lax.fori_loop(0, n_v // SUBLANES, v_block, 0)
        return carry

    lax.fori_loop(0, steps, step, 0)

    @pl.when(tc == pl.num_programs(1) - 1)
    def _():
        sout_ref[...] = s_s[...]


def _rwkv_scan_call(r, w, k, kk, kka, v, s0, steps):
    n_t, _, lanes = r.shape
    n_v = v.shape[1]
    kspec = pl.BlockSpec((steps, RWKV_HEAD, LANES), lambda g, i: (i, 0, g))
    vspec = pl.BlockSpec((steps, n_v, LANES), lambda g, i: (i, 0, g))
    sspec = pl.BlockSpec((n_v, RWKV_HEAD, LANES), lambda g, i: (0, 0, g))
    return pl.pallas_call(
        functools.partial(_rwkv_scan_kernel, steps=steps, n_v=n_v),
        grid=(lanes // LANES, n_t // steps),
        in_specs=[kspec] * 5 + [vspec, sspec],
        out_specs=[vspec, sspec],
        out_shape=[jax.ShapeDtypeStruct(v.shape, F32), jax.ShapeDtypeStruct(s0.shape, F32)],
        scratch_shapes=[pltpu.VMEM((n_v, RWKV_HEAD, LANES), F32)],
        compiler_params=_params(("arbitrary", "arbitrary")),
        name="rwkv_recurrence",
    )(r, w, k, kk, kka, v, s0)


def _to_pairs_k(x_rows, n_t, n_seq, v_split):
    x = x_rows.reshape(n_t, n_seq, RWKV_HEADS, RWKV_HEAD)
    x = jnp.transpose(x, (0, 3, 1, 2)).reshape(n_t, RWKV_HEAD, n_seq * RWKV_HEADS)
    return jnp.concatenate([x] * v_split, axis=-1)


def _to_pairs_v(x_rows, n_t, n_seq, v_split):
    n_v = RWKV_HEAD // v_split
    x = x_rows.reshape(n_t, n_seq, RWKV_HEADS, v_split, n_v)
    return jnp.transpose(x, (0, 4, 3, 1, 2)).reshape(n_t, n_v, v_split * n_seq * RWKV_HEADS)


def _from_pairs_v(y, n_t, n_seq, v_split):
    n_v = RWKV_HEAD // v_split
    y = y.reshape(n_t, n_v, v_split, n_seq, RWKV_HEADS)
    return jnp.transpose(y, (0, 3, 4, 2, 1)).reshape(n_t * n_seq, RWKV_WIDTH)


def _state_to_pairs(s, v_split):
    n_seq = s.shape[0]
    n_v = RWKV_HEAD // v_split
    s = s.reshape(n_seq, RWKV_HEADS, v_split, n_v, RWKV_HEAD)
    return jnp.transpose(s, (3, 4, 2, 0, 1)).reshape(n_v, RWKV_HEAD, v_split * n_seq * RWKV_HEADS)


def _state_from_pairs(s, n_seq, v_split):
    n_v = RWKV_HEAD // v_split
    s = s.reshape(n_v, RWKV_HEAD, v_split, n_seq, RWKV_HEADS)
    return jnp.transpose(s, (3, 4, 2, 0, 1)).reshape(n_seq, RWKV_HEADS, RWKV_HEAD, RWKV_HEAD)


def _mixout_kernel(x_ref, ys5_ref, yrw_ref, r_ref, k_ref, v_ref, g_ref, mod_ref, gnw_ref, gnb_ref, rk_ref,
                   ones_ref, wout_ref, n2g_ref, x1_ref, h2_ref, *, n_seq):
    y = yrw_ref[...]
    inv_n = 1.0 / RWKV_HEAD
    mean = _head_sum(y, ones_ref) * inv_n
    yc = y - mean
    var = _head_sum(yc * yc, ones_ref) * inv_n
    y = yc * lax.rsqrt(var + GN_EPS) * gnw_ref[...] + gnb_ref[...]
    v = v_ref[...]
    y = y + _head_sum(r_ref[...] * k_ref[...] * rk_ref[...], ones_ref) * v
    y = y * g_ref[...]
    mix = jnp.concatenate([ys5_ref[...], y], axis=-1).astype(BF16)
    x1 = x_ref[...] + _gated(_dot(mix, wout_ref[...]), mod_ref[2], n_seq)
    x1_ref[...] = x1
    h2_ref[...] = _modulate(_rms(x1, n2g_ref[...]), mod_ref[3], mod_ref[4], n_seq)


def _mixout_call(x_rows, y_s5, y_rw, r, k, v, g, mod6, consts, n_seq, tile):
    rows, d = x_rows.shape
    wide = pl.BlockSpec((tile, d), lambda i: (i, 0))
    half = pl.BlockSpec((tile, RWKV_WIDTH), lambda i: (i, 0))
    return pl.pallas_call(
        functools.partial(_mixout_kernel, n_seq=n_seq),
        grid=(rows // tile,),
        in_specs=[wide] + [half] * 6 + [_full(mod6.shape)] + [_full(a.shape) for a in consts],
        out_specs=[wide, wide],
        out_shape=[jax.ShapeDtypeStruct((rows, d), F32)] * 2,
        compiler_params=_params(("arbitrary",)),
        name="mixer_out_norm2",
    )(x_rows, y_s5, y_rw, r, k, v, g, mod6, *consts)


def _top_desc(x, count):
    rows = []
    for _ in range(count):
        m = jnp.max(x, axis=0, keepdims=True)
        rows.append(m)
        x = jnp.where(x >= m, -jnp.inf, x)
    return rows


def _peer_select(xt, wq_ref, k1_ref, k2_ref, q_s, a_s, b_s, e1_s, e2_s, thr_s, lst_s, cand_s):
    tg = xt.shape[1]
    q_s[...] = _dot(wq_ref[...], xt, HIGHEST).reshape(PEER_HEADS, 2 * PEER_HALF, tg)

    def head(h, carry):
        q = q_s[h]
        s1 = _dot(k1_ref[h], q[:PEER_HALF], HIGHEST)
        s2 = _dot(k2_ref[h], q[PEER_HALF:], HIGHEST)
        top1 = _top_desc(s1, PEER_TOPK)
        top2 = _top_desc(s2, PEER_TOPK)
        for i in range(PEER_TOPK):
            lst_s[i:i + 1, :] = top2[i]
        best2 = lst_s[...]
        for i in range(PEER_TOPK):
            cand_s[i * PEER_TOPK:(i + 1) * PEER_TOPK, :] = top1[i] + best2
        topc = _top_desc(cand_s[...], PEER_TOPK)
        z = jnp.exp(topc[0] - topc[0])
        for i in range(1, PEER_TOPK):
            z = z + jnp.exp(topc[i] - topc[0])
        a = jnp.where(s1 >= top1[-1], s1, -jnp.inf)
        b = jnp.where(s2 >= top2[-1], s2, -jnp.inf)
        a_s[h] = a
        b_s[h] = b
        e1_s[h] = jnp.exp(a - top1[0]) / z
        e2_s[h] = jnp.exp(b - top2[0])
        thr_s[h] = jnp.broadcast_to(topc[-1], (SUBLANES, tg))
        return carry

    lax.fori_loop(0, PEER_HEADS, head, 0)


def _peer_kernel(h2_ref, x1_ref, ga_ref, wq_ref, k1_ref, k2_ref, u_ref, vt_ref, fg_ref, o_ref,
                 xt_s, q_s, a_s, b_s, e1_s, e2_s, thr_s, lst_s, cand_s, act_s, pt_s, acc_s, *, n_seq):
    j = pl.program_id(1)
    n_groups = xt_s.shape[0]
    tg = xt_s.shape[2]

    @pl.when(j == 0)
    def _():
        def prep(g, carry):
            t0 = pl.multiple_of(g * tg, tg)
            xt = h2_ref[pl.ds(t0, tg), :].T
            xt_s[g] = xt.astype(BF16)
            _peer_select(xt, wq_ref, k1_ref, k2_ref, q_s, a_s.at[g], b_s.at[g], e1_s.at[g], e2_s.at[g],
                         thr_s.at[g], lst_s, cand_s)
            return carry

        lax.fori_loop(0, n_groups, prep, 0)
        acc_s[...] = jnp.zeros_like(acc_s)

    key0 = pl.multiple_of(j * PEER_KEYS_PER_BLOCK, PEER_KEYS_PER_BLOCK)

    def group(g, carry):
        act_s[...] = _dot(u_ref[...], xt_s[g])
        for lg in range(tg // LANES):
            ls = slice(lg * LANES, (lg + 1) * LANES)
            a_rows = [a_s[g, h, pl.ds(key0, PEER_KEYS_PER_BLOCK), ls] for h in range(PEER_HEADS)]
            e_rows = [e1_s[g, h, pl.ds(key0, PEER_KEYS_PER_BLOCK), ls] for h in range(PEER_HEADS)]
            for i in range(PEER_KEYS_PER_BLOCK):
                gate = jnp.zeros((N_KEYS, LANES), F32)
                for h in range(PEER_HEADS):
                    score = b_s[g, h, :, ls] + a_rows[h][i:i + 1, :]
                    weight = e2_s[g, h, :, ls] * e_rows[h][i:i + 1, :]
                    gate = gate + jnp.where(score >= thr_s[g, h, 0:1, ls], weight, 0.0)
                act = _gelu(act_s[i * N_KEYS:(i + 1) * N_KEYS, ls])
                pt_s[i * N_KEYS:(i + 1) * N_KEYS, ls] = (gate * act).astype(BF16)
        acc_s[g] += _dot(vt_ref[...], pt_s[...])
        return carry

    lax.fori_loop(0, n_groups, group, 0)

    @pl.when(j == pl.num_programs(1) - 1)
    def _():
        for g in range(n_groups):
            rows = slice(g * tg, (g + 1) * tg)
            x2 = x1_ref[rows, :] + _gated(acc_s[g].T, ga_ref[...], min(n_seq, tg))
            o_ref[rows, :] = _rms(x2, fg_ref[...])


def _peer_call(h2, x1, gate2, wq_t, keys1, keys2, u_bf, vt_bf, final_g, n_seq, tm):
    rows, d = h2.shape
    n_exp = u_bf.shape[0]
    eb = PEER_KEYS_PER_BLOCK * N_KEYS
    tg = PEER_TOKEN_GROUP
    n_groups = tm // tg
    assert tm % tg == 0 and (tg % n_seq == 0 or n_seq % tg == 0) and n_exp % eb == 0
    if n_seq > tg:
        raise NotImplementedError("more than PEER_TOKEN_GROUP sequences per step")
    tok = pl.BlockSpec((tm, d), lambda i, j: (i, 0))
    sel = pltpu.VMEM((n_groups, PEER_HEADS, N_KEYS, tg), F32)
    return pl.pallas_call(
        functools.partial(_peer_kernel, n_seq=n_seq),
        grid=(rows // tm, n_exp // eb),
        in_specs=[tok, tok, _full(gate2.shape), _full(wq_t.shape), _full(keys1.shape), _full(keys2.shape),
                  pl.BlockSpec((eb, d), lambda i, j: (j, 0)), pl.BlockSpec((d, eb), lambda i, j: (0, j)),
                  _full((1, d))],
        out_specs=tok,
        out_shape=jax.ShapeDtypeStruct((rows, d), F32),
        scratch_shapes=[pltpu.VMEM((n_groups, d, tg), BF16), pltpu.VMEM((PEER_HEADS, 2 * PEER_HALF, tg), F32),
                        sel, sel, sel, sel, pltpu.VMEM((n_groups, PEER_HEADS, SUBLANES, tg), F32),
                        pltpu.VMEM((PEER_TOPK, tg), F32), pltpu.VMEM((PEER_TOPK * PEER_TOPK, tg), F32),
                        pltpu.VMEM((eb, tg), F32), pltpu.VMEM((eb, tg), BF16), pltpu.VMEM((n_groups, d, tg), F32)],
        compiler_params=_params(("arbitrary", "arbitrary")),
        name="peer_final_norm",
    )(h2, x1, gate2, wq_t, keys1, keys2, u_bf, vt_bf, final_g.reshape(1, d))


def _pick(n, target):
    t = min(n, target)
    while n % t:
        t -= 1
    return t


def _layer(x_btd, mod, s5_re0, s5_im0, wkv0, shift0, prm):
    n_seq, n_t, d = x_btd.shape
    rows = n_seq * n_t
    x_rows = jnp.transpose(x_btd, (1, 0, 2)).reshape(rows, d)
    mod6 = jnp.transpose(mod.reshape(n_seq, 6, d), (1, 0, 2))
    tile = n_seq * _pick(n_t, max(1, 512 // n_seq))

    u_rows, p_rows = _inproj_call(x_rows, mod6, prm["norm1_g"], prm["w_in_bf"], n_seq, tile)

    s5_steps = _pick(n_t, max(1, 512 // n_seq))
    y_s5, s5_re, s5_im = _s5_call(u_rows, s5_re0.reshape(n_seq, S5_LANES), s5_im0.reshape(n_seq, S5_LANES),
                                  prm["s5_consts"], n_seq, s5_steps)

    r, w, k, v, kk, kka, g = _rwkv_pre_call(p_rows, shift0, prm["rwkv_pre_consts"], n_seq, tile)
    v_split = max(1, LANES // (n_seq * RWKV_HEADS))
    pk = lambda a: _to_pairs_k(a, n_t, n_seq, v_split)
    y_pairs, s_pairs = _rwkv_scan_call(pk(r), pk(w), pk(k), pk(kk), pk(kka), _to_pairs_v(v, n_t, n_seq, v_split),
                                       _state_to_pairs(wkv0, v_split), _pick(n_t, 16))
    y_rw = _from_pairs_v(y_pairs, n_t, n_seq, v_split)
    wkv = _state_from_pairs(s_pairs, n_seq, v_split)

    x1, h2 = _mixout_call(x_rows, y_s5, y_rw, r, k, v, g, mod6, prm["mixout_consts"], n_seq, tile)

    tm = n_seq * _pick(n_t, max(1, 512 // n_seq))
    y_rows = _peer_call(h2, x1, mod6[5], prm["peer_wq_t"], prm["peer_keys1"], prm["peer_keys2"], prm["peer_u_bf"],
                        prm["peer_vt_bf"], prm["final_norm_g"], n_seq, tm)
    y = jnp.transpose(y_rows.reshape(n_t, n_seq, d), (1, 0, 2))
    shift = p_rows[rows - n_seq:]
    return (y, s5_re.reshape(n_seq, S5_GROUPS, S5_STATE), s5_im.reshape(n_seq, S5_GROUPS, S5_STATE), wkv, shift)


def kernel(x_prompt, x_sample, state_s5_re, state_s5_im, state_wkv, state_shift, c_prompt, c_sample, w_ada, b_ada, norm1_g, norm2_g, w_in, w_out, s5_a_re, s5_a_im, s5_log_dt, s5_b_re, s5_b_im, s5_c_re, s5_c_im, s5_d, w_glu, b_glu, rwkv_mu, rwkv_w0, rwkv_w2, rwkv_a0, rwkv_a2, rwkv_g2, rwkv_k_k, rwkv_k_a, rwkv_r_k, rwkv_gn_w, rwkv_gn_b, peer_w_q, peer_keys1, peer_keys2, peer_u, peer_v, final_norm_g):
    assert w_ada.shape[0] == 1, "single-layer model"
    nbp = x_prompt.shape[0]
    row = lambda a: a.reshape(1, -1)

    mod = _ada_call(jnp.concatenate([c_prompt, c_sample], axis=0).astype(F32), w_ada[0], b_ada[0])

    lb_re, lb_im, bw_re, bw_im = _s5_prep_call(s5_a_re[0], s5_a_im[0], s5_log_dt[0], s5_b_re[0], s5_b_im[0])
    head_ones = jnp.kron(jnp.eye(RWKV_HEADS, dtype=F32), jnp.ones((RWKV_HEAD, RWKV_HEAD), F32))
    lora_pad = jnp.zeros((RWKV_LORA // 2, RWKV_WIDTH), F32)
    prm = {
        "norm1_g": norm1_g[0],
        "w_in_bf": w_in[0].astype(BF16),
        "s5_consts": (lb_re, lb_im, bw_re, bw_im, _s5_out_blockdiag(s5_c_re[0]), _s5_out_blockdiag(s5_c_im[0]),
                      row(s5_d[0]), w_glu[0].astype(BF16), row(b_glu[0])),
        "rwkv_pre_consts": (row(rwkv_mu[0]), row(rwkv_w0[0]), jnp.concatenate([rwkv_w2[0], lora_pad], axis=0),
                            row(rwkv_a0[0]), jnp.concatenate([lora_pad, rwkv_a2[0]], axis=0), rwkv_g2[0],
                            row(rwkv_k_k[0]), row(rwkv_k_a[0]), head_ones),
        "mixout_consts": (row(rwkv_gn_w[0]), row(rwkv_gn_b[0]), row(rwkv_r_k[0]), head_ones,
                          w_out[0].astype(BF16), row(norm2_g[0])),
        "peer_wq_t": peer_w_q[0].T,
        "peer_keys1": peer_keys1[0],
        "peer_keys2": peer_keys2[0],
        "peer_u_bf": peer_u[0].astype(BF16),
        "peer_vt_bf": peer_v[0].T.astype(BF16),
        "final_norm_g": final_norm_g,
    }

    z_s5 = jnp.zeros((nbp, S5_GROUPS, S5_STATE), F32)
    z_wkv = jnp.zeros((nbp, RWKV_HEADS, RWKV_HEAD, RWKV_HEAD), F32)
    z_sh = jnp.zeros((nbp, RWKV_COLS), F32)
    yp, pr, pi, pw, psh = _layer(x_prompt.astype(F32), mod[:nbp], z_s5, z_s5, z_wkv, z_sh, prm)
    ys, sr, si, sw, ssh = _layer(x_sample.astype(F32), mod[nbp:], state_s5_re[0].astype(F32),
                                 state_s5_im[0].astype(F32), state_wkv[0].astype(F32),
                                 state_shift[0].astype(F32), prm)
    return (yp.astype(x_prompt.dtype), ys.astype(x_sample.dtype), pr[None], pi[None], pw[None], psh[None],
            sr[None], si[None], sw[None], ssh[None])
```

```python
import functools

import jax
import jax.numpy as jnp
from jax import lax
from jax.experimental import pallas as pl
from jax.experimental.pallas import tpu as pltpu

F32 = jnp.float32
BF16 = jnp.bfloat16
HIGHEST = lax.Precision.HIGHEST

LANES = 128
SUBLANES = 8
VMEM_LIMIT_BYTES = 56 * 1024 * 1024

D_MODEL = 1024
S5_WIDTH = 512
S5_GROUP = 16
S5_GROUPS = 32
S5_STATE = 64
S5_LANES = S5_GROUPS * S5_STATE
S5_CHUNKS = 4
S5_CHUNK_IN = S5_WIDTH // S5_CHUNKS
S5_CHUNK_ST = S5_LANES // S5_CHUNKS
RWKV_WIDTH = 512
RWKV_HEAD = 64
RWKV_HEADS = 8
RWKV_LORA = 128
RWKV_COLS = 3 * RWKV_WIDTH + 64 + 64 + 128
PEER_HEADS = 8
N_KEYS = 128
PEER_TOPK = 16
PEER_HALF = 64
PEER_TOKEN_GROUP = 256
PEER_KEYS_PER_BLOCK = 8
NORM_EPS = 1e-6
GN_EPS = 64e-5


def _params(sem):
    return pltpu.CompilerParams(dimension_semantics=sem, vmem_limit_bytes=VMEM_LIMIT_BYTES)


def _full(shape):
    return pl.BlockSpec(shape, lambda *_: (0,) * len(shape))


def _dot(a, b, precision=None):
    return jnp.dot(a, b, precision=precision, preferred_element_type=F32)


def _gelu(x):
    return 0.5 * x * (1.0 + lax.erf(x * (2.0 ** -0.5)))


def _rms(x, g):
    return x * lax.rsqrt(jnp.mean(x * x, axis=-1, keepdims=True) + NORM_EPS) * g


def _modulate(h, shift, scale, n_seq):
    rows, d = h.shape
    h3 = h.reshape(rows // n_seq, n_seq, d)
    return (h3 * (1.0 + scale)[None] + shift[None]).reshape(rows, d)


def _gated(h, gate, n_seq):
    rows, d = h.shape
    return (h.reshape(rows // n_seq, n_seq, d) * gate[None]).reshape(rows, d)


def _ada_kernel(c_ref, w_ref, b_ref, o_ref):
    s = jax.nn.silu(c_ref[...])
    o_ref[...] = _dot(s, w_ref[...], HIGHEST) + b_ref[...]


def _ada_call(c_all, w_ada, b_ada):
    n, d = c_all.shape
    cols = w_ada.shape[1]
    return pl.pallas_call(
        _ada_kernel,
        grid=(cols // d,),
        in_specs=[_full((n, d)), pl.BlockSpec((d, d), lambda j: (0, j)), pl.BlockSpec((1, d), lambda j: (0, j))],
        out_specs=pl.BlockSpec((n, d), lambda j: (0, j)),
        out_shape=jax.ShapeDtypeStruct((n, cols), F32),
        compiler_params=_params(("arbitrary",)),
        name="adaln_mod",
    )(c_all, w_ada, b_ada.reshape(1, cols))


def _inproj_kernel(x_ref, mod_ref, g_ref, w_ref, u_ref, p_ref, *, n_seq):
    h = _modulate(_rms(x_ref[...], g_ref[...]), mod_ref[0], mod_ref[1], n_seq)
    proj = _dot(h.astype(BF16), w_ref[...])
    u_ref[...] = proj[:, :S5_WIDTH]
    p_ref[...] = proj[:, S5_WIDTH:]


def _inproj_call(x_rows, mod6, norm_g, w_in_bf, n_seq, tile):
    rows, d = x_rows.shape
    return pl.pallas_call(
        functools.partial(_inproj_kernel, n_seq=n_seq),
        grid=(rows // tile,),
        in_specs=[pl.BlockSpec((tile, d), lambda i: (i, 0)), _full(mod6.shape), _full((1, d)),
                  _full(w_in_bf.shape)],
        out_specs=[pl.BlockSpec((tile, S5_WIDTH), lambda i: (i, 0)),
                   pl.BlockSpec((tile, RWKV_COLS), lambda i: (i, 0))],
        out_shape=[jax.ShapeDtypeStruct((rows, S5_WIDTH), F32), jax.ShapeDtypeStruct((rows, RWKV_COLS), F32)],
        compiler_params=_params(("arbitrary",)),
        name="norm1_inproj",
    )(x_rows, mod6, norm_g.reshape(1, d), w_in_bf)


def _s5_prep_kernel(are_ref, aim_ref, ldt_ref, bre_ref, bim_ref, lbr_ref, lbi_ref, wre_ref, wim_ref):
    lam_re, lam_im = are_ref[...], aim_ref[...]
    dt = jnp.exp(ldt_ref[...])
    mag = jnp.exp(lam_re * dt)
    ang = lam_im * dt
    lb_re, lb_im = mag * jnp.cos(ang), mag * jnp.sin(ang)
    den = lam_re * lam_re + lam_im * lam_im
    n_re, n_im = lb_re - 1.0, lb_im
    coef_re = (n_re * lam_re + n_im * lam_im) / den
    coef_im = (n_im * lam_re - n_re * lam_im) / den
    lbr_ref[...] = lb_re
    lbi_ref[...] = lb_im
    for c in range(S5_CHUNKS):
        cr = coef_re[:, c * S5_CHUNK_ST:(c + 1) * S5_CHUNK_ST]
        ci = coef_im[:, c * S5_CHUNK_ST:(c + 1) * S5_CHUNK_ST]
        wre_ref[c] = cr * bre_ref[c] - ci * bim_ref[c]
        wim_ref[c] = cr * bim_ref[c] + ci * bre_ref[c]


def _s5_prep_call(a_re, a_im, log_dt, b_re, b_im):
    row = lambda a: a.reshape(1, S5_LANES)
    ldt = jnp.repeat(log_dt, S5_STATE).reshape(1, S5_LANES)
    w_shape = (S5_CHUNKS, S5_CHUNK_IN, S5_CHUNK_ST)
    return pl.pallas_call(
        _s5_prep_kernel,
        out_shape=[jax.ShapeDtypeStruct((1, S5_LANES), F32)] * 2 + [jax.ShapeDtypeStruct(w_shape, F32)] * 2,
        compiler_params=pltpu.CompilerParams(vmem_limit_bytes=VMEM_LIMIT_BYTES),
        name="s5_discretise",
    )(row(a_re), row(a_im), ldt, _s5_in_blockdiag(b_re), _s5_in_blockdiag(b_im))


def _s5_in_blockdiag(b):
    gpc = S5_GROUPS // S5_CHUNKS
    bt = jnp.transpose(b, (0, 2, 1)).reshape(S5_CHUNKS, gpc, S5_GROUP, S5_STATE)
    eye = jnp.eye(gpc, dtype=b.dtype)
    bd = bt[:, :, :, None, :] * eye[None, :, None, :, None]
    return bd.reshape(S5_CHUNKS, S5_CHUNK_IN, S5_CHUNK_ST)


def _s5_out_blockdiag(c):
    gpc = S5_GROUPS // S5_CHUNKS
    ct = jnp.transpose(c, (0, 2, 1)).reshape(S5_CHUNKS, gpc, S5_STATE, S5_GROUP)
    eye = jnp.eye(gpc, dtype=c.dtype)
    bd = ct[:, :, :, None, :] * eye[None, :, None, :, None]
    return bd.reshape(S5_CHUNKS, S5_CHUNK_ST, S5_CHUNK_IN)


def _s5_kernel(u_ref, h0r_ref, h0i_ref, lbr_ref, lbi_ref, wre_ref, wim_ref, cre_ref, cim_ref, d_ref, wglu_ref,
               bglu_ref, y_ref, hr_out, hi_out, re_s, im_s, str_s, sti_s, *, n_seq, steps):
    chunk = pl.program_id(0)

    @pl.when(chunk == 0)
    def _():
        str_s[...] = h0r_ref[...]
        sti_s[...] = h0i_ref[...]

    u = u_ref[...]
    for c in range(S5_CHUNKS):
        uc = u[:, c * S5_CHUNK_IN:(c + 1) * S5_CHUNK_IN]
        re_s[:, c * S5_CHUNK_ST:(c + 1) * S5_CHUNK_ST] = _dot(uc, wre_ref[c], HIGHEST)
        im_s[:, c * S5_CHUNK_ST:(c + 1) * S5_CHUNK_ST] = _dot(uc, wim_ref[c], HIGHEST)

    def seq_block(rb, carry):
        r0 = pl.multiple_of(rb * SUBLANES, SUBLANES)
        for c in range(S5_CHUNKS):
            lanes = slice(c * S5_CHUNK_ST, (c + 1) * S5_CHUNK_ST)
            lr = jnp.broadcast_to(lbr_ref[:, lanes], (SUBLANES, S5_CHUNK_ST))
            li = jnp.broadcast_to(lbi_ref[:, lanes], (SUBLANES, S5_CHUNK_ST))

            def step(t, h):
                hr, hi = h
                row = pl.multiple_of(t * n_seq + r0, SUBLANES)
                nr = lr * hr - li * hi + re_s[pl.ds(row, SUBLANES), lanes]
                ni = lr * hi + li * hr + im_s[pl.ds(row, SUBLANES), lanes]
                re_s[pl.ds(row, SUBLANES), lanes] = nr
                im_s[pl.ds(row, SUBLANES), lanes] = ni
                return nr, ni

            h0 = (str_s[pl.ds(r0, SUBLANES), lanes], sti_s[pl.ds(r0, SUBLANES), lanes])
            hr, hi = lax.fori_loop(0, steps, step, h0, unroll=min(steps, 8))
            str_s[pl.ds(r0, SUBLANES), lanes] = hr
            sti_s[pl.ds(r0, SUBLANES), lanes] = hi
        return carry

    lax.fori_loop(0, n_seq // SUBLANES, seq_block, 0)

    ys = []
    for c in range(S5_CHUNKS):
        lanes = slice(c * S5_CHUNK_ST, (c + 1) * S5_CHUNK_ST)
        ys.append(_dot(re_s[:, lanes], cre_ref[c], HIGHEST) - _dot(im_s[:, lanes], cim_ref[c], HIGHEST))
    y = jnp.concatenate(ys, axis=-1) + d_ref[...] * u
    y = _gelu(y)
    y_ref[...] = y * jax.nn.sigmoid(_dot(y.astype(BF16), wglu_ref[...]) + bglu_ref[...])

    @pl.when(chunk == pl.num_programs(0) - 1)
    def _():
        hr_out[...] = str_s[...]
        hi_out[...] = sti_s[...]


def _s5_call(u_rows, h0_re, h0_im, consts, n_seq, steps):
    rows = u_rows.shape[0]
    tile = n_seq * steps
    lb_re, lb_im, w_re, w_im, c_re, c_im, d_skip, w_glu_bf, b_glu = consts
    state = jax.ShapeDtypeStruct((n_seq, S5_LANES), F32)
    args = (u_rows, h0_re, h0_im, lb_re, lb_im, w_re, w_im, c_re, c_im, d_skip, w_glu_bf, b_glu)
    in_specs = [pl.BlockSpec((tile, S5_WIDTH), lambda i: (i, 0))] + [_full(a.shape) for a in args[1:]]
    return pl.pallas_call(
        functools.partial(_s5_kernel, n_seq=n_seq, steps=steps),
        grid=(rows // tile,),
        in_specs=in_specs,
        out_specs=[pl.BlockSpec((tile, S5_WIDTH), lambda i: (i, 0)), _full(state.shape), _full(state.shape)],
        out_shape=[jax.ShapeDtypeStruct((rows, S5_WIDTH), F32), state, state],
        scratch_shapes=[pltpu.VMEM((tile, S5_LANES), F32), pltpu.VMEM((tile, S5_LANES), F32),
                        pltpu.VMEM((n_seq, S5_LANES), F32), pltpu.VMEM((n_seq, S5_LANES), F32)],
        compiler_params=_params(("arbitrary",)),
        name="s5_mixer",
    )(*args)


def _head_sum(x, ones_ref):
    return _dot(x, ones_ref[...], HIGHEST)


def _rwkv_pre_kernel(p_ref, prev_ref, shift_ref, mu_ref, w0_ref, w2_ref, a0_ref, a2_ref, g2_ref, kk_ref, ka_ref,
                     ones_ref, r_out, w_out, k_out, v_out, kk_out, kka_out, g_out, *, n_seq):
    p = p_ref[...]
    tile = p.shape[0]
    head = jnp.where(pl.program_id(0) == 0, shift_ref[...], prev_ref[...])
    p_prev = head if tile == n_seq else jnp.concatenate([head, p[:tile - n_seq]], axis=0)
    ps = p + (p_prev - p) * mu_ref[...]
    w = RWKV_WIDTH
    r, k, v = ps[:, :w], ps[:, w:2 * w], ps[:, 2 * w:3 * w]
    lo = ps[:, 3 * w:3 * w + RWKV_LORA]
    g_lo = ps[:, 3 * w + RWKV_LORA:]
    w_raw = -jax.nn.softplus(-(w0_ref[...] + _dot(jnp.tanh(lo), w2_ref[...], HIGHEST))) - 0.5
    a = jax.nn.sigmoid(a0_ref[...] + _dot(lo, a2_ref[...], HIGHEST))
    kk = k * kk_ref[...]
    norm = jnp.sqrt(_head_sum(kk * kk, ones_ref))
    kk = kk / jnp.maximum(norm, 1e-12)
    r_out[...] = r
    w_out[...] = jnp.exp(-jnp.exp(w_raw))
    k_out[...] = k * (1.0 + (a - 1.0) * ka_ref[...])
    v_out[...] = v
    kk_out[...] = kk
    kka_out[...] = kk * a
    g_out[...] = _dot(jax.nn.sigmoid(g_lo), g2_ref[...], HIGHEST)


def _rwkv_pre_call(p_rows, shift0, consts, n_seq, tile):
    rows = p_rows.shape[0]
    per = tile // n_seq
    vec = jax.ShapeDtypeStruct((rows, RWKV_WIDTH), F32)
    in_specs = [pl.BlockSpec((tile, RWKV_COLS), lambda i: (i, 0)),
                pl.BlockSpec((n_seq, RWKV_COLS), lambda i: (jnp.maximum(i * per - 1, 0), 0)),
                _full(shift0.shape)] + [_full(a.shape) for a in consts]
    return pl.pallas_call(
        functools.partial(_rwkv_pre_kernel, n_seq=n_seq),
        grid=(rows // tile,),
        in_specs=in_specs,
        out_specs=[pl.BlockSpec((tile, RWKV_WIDTH), lambda i: (i, 0))] * 7,
        out_shape=[vec] * 7,
        compiler_params=_params(("arbitrary",)),
        name="rwkv_prologue",
    )(p_rows, p_rows, shift0, *consts)


def _sublane_allsum(p):
    p = p + pltpu.roll(p, 4, 0)
    p = p + pltpu.roll(p, 2, 0)
    return p + pltpu.roll(p, 1, 0)


def _rwkv_scan_kernel(r_ref, w_ref, k_ref, kk_ref, kka_ref, v_ref, s0_ref, y_ref, sout_ref, s_s, *, steps, n_v):
    tc = pl.program_id(1)

    @pl.when(tc == 0)
    def _():
        s_s[...] = s0_ref[...]

    row_id = lax.broadcasted_iota(jnp.int32, (SUBLANES, LANES), 0)
    split = lambda x: x.reshape(RWKV_HEAD // SUBLANES, SUBLANES, LANES)

    def step(t, carry):
        r, w, k, kk, kka = split(r_ref[t]), split(w_ref[t]), split(k_ref[t]), split(kk_ref[t]), split(kka_ref[t])

        def v_block(vb, carry2):
            v0 = pl.multiple_of(vb * SUBLANES, SUBLANES)
            v_tile = v_ref[t, pl.ds(v0, SUBLANES), :]
            y_tile = jnp.zeros((SUBLANES, LANES), F32)
            for j in range(SUBLANES):
                s = split(s_s[v0 + j])
                sa = -_sublane_allsum(jnp.sum(s * kk, axis=0))
                v_row = jnp.broadcast_to(v_tile[j:j + 1, :], (SUBLANES, LANES))
                s = s * w + sa[None] * kka + v_row[None] * k
                s_s[v0 + j] = s.reshape(RWKV_HEAD, LANES)
                y_tile = jnp.where(row_id == j, _sublane_allsum(jnp.sum(s * r, axis=0)), y_tile)
            y_ref[t, pl.ds(v0, SUBLANES), :] = y_tile
            return carry2

        lax.fori_loop(0, n_v // SUBLANES, v_block, 0)
        return carry

    lax.fori_loop(0, steps, step, 0)

    @pl.when(tc == pl.num_programs(1) - 1)
    def _():
        sout_ref[...] = s_s[...]


def _rwkv_scan_call(r, w, k, kk, kka, v, s0, steps):
    n_t, _, lanes = r.shape
    n_v = v.shape[1]
    kspec = pl.BlockSpec((steps, RWKV_HEAD, LANES), lambda g, i: (i, 0, g))
    vspec = pl.BlockSpec((steps, n_v, LANES), lambda g, i: (i, 0, g))
    sspec = pl.BlockSpec((n_v, RWKV_HEAD, LANES), lambda g, i: (0, 0, g))
    return pl.pallas_call(
        functools.partial(_rwkv_scan_kernel, steps=steps, n_v=n_v),
        grid=(lanes // LANES, n_t // steps),
        in_specs=[kspec] * 5 + [vspec, sspec],
        out_specs=[vspec, sspec],
        out_shape=[jax.ShapeDtypeStruct(v.shape, F32), jax.ShapeDtypeStruct(s0.shape, F32)],
        scratch_shapes=[pltpu.VMEM((n_v, RWKV_HEAD, LANES), F32)],
        compiler_params=_params(("arbitrary", "arbitrary")),
        name="rwkv_recurrence",
    )(r, w, k, kk, kka, v, s0)


def _to_pairs_k(x_rows, n_t, n_seq, v_split):
    x = x_rows.reshape(n_t, n_seq, RWKV_HEADS, RWKV_HEAD)
    x = jnp.transpose(x, (0, 3, 1, 2)).reshape(n_t, RWKV_HEAD, n_seq * RWKV_HEADS)
    return jnp.concatenate([x] * v_split, axis=-1)


def _to_pairs_v(x_rows, n_t, n_seq, v_split):
    n_v = RWKV_HEAD // v_split
    x = x_rows.reshape(n_t, n_seq, RWKV_HEADS, v_split, n_v)
    return jnp.transpose(x, (0, 4, 3, 1, 2)).reshape(n_t, n_v, v_split * n_seq * RWKV_HEADS)


def _from_pairs_v(y, n_t, n_seq, v_split):
    n_v = RWKV_HEAD // v_split
    y = y.reshape(n_t, n_v, v_split, n_seq, RWKV_HEADS)
    return jnp.transpose(y, (0, 3, 4, 2, 1)).reshape(n_t * n_seq, RWKV_WIDTH)


def _state_to_pairs(s, v_split):
    n_seq = s.shape[0]
    n_v = RWKV_HEAD // v_split
    s = s.reshape(n_seq, RWKV_HEADS, v_split, n_v, RWKV_HEAD)
    return jnp.transpose(s, (3, 4, 2, 0, 1)).reshape(n_v, RWKV_HEAD, v_split * n_seq * RWKV_HEADS)


def _state_from_pairs(s, n_seq, v_split):
    n_v = RWKV_HEAD // v_split
    s = s.reshape(n_v, RWKV_HEAD, v_split, n_seq, RWKV_HEADS)
    return jnp.transpose(s, (3, 4, 2, 0, 1)).reshape(n_seq, RWKV_HEADS, RWKV_HEAD, RWKV_HEAD)


def _mixout_kernel(x_ref, ys5_ref, yrw_ref, r_ref, k_ref, v_ref, g_ref, mod_ref, gnw_ref, gnb_ref, rk_ref,
                   ones_ref, wout_ref, n2g_ref, x1_ref, h2_ref, *, n_seq):
    y = yrw_ref[...]
    inv_n = 1.0 / RWKV_HEAD
    mean = _head_sum(y, ones_ref) * inv_n
    yc = y - mean
    var = _head_sum(yc * yc, ones_ref) * inv_n
    y = yc * lax.rsqrt(var + GN_EPS) * gnw_ref[...] + gnb_ref[...]
    v = v_ref[...]
    y = y + _head_sum(r_ref[...] * k_ref[...] * rk_ref[...], ones_ref) * v
    y = y * g_ref[...]
    mix = jnp.concatenate([ys5_ref[...], y], axis=-1).astype(BF16)
    x1 = x_ref[...] + _gated(_dot(mix, wout_ref[...]), mod_ref[2], n_seq)
    x1_ref[...] = x1
    h2_ref[...] = _modulate(_rms(x1, n2g_ref[...]), mod_ref[3], mod_ref[4], n_seq)


def _mixout_call(x_rows, y_s5, y_rw, r, k, v, g, mod6, consts, n_seq, tile):
    rows, d = x_rows.shape
    wide = pl.BlockSpec((tile, d), lambda i: (i, 0))
    half = pl.BlockSpec((tile, RWKV_WIDTH), lambda i: (i, 0))
    return pl.pallas_call(
        functools.partial(_mixout_kernel, n_seq=n_seq),
        grid=(rows // tile,),
        in_specs=[wide] + [half] * 6 + [_full(mod6.shape)] + [_full(a.shape) for a in consts],
        out_specs=[wide, wide],
        out_shape=[jax.ShapeDtypeStruct((rows, d), F32)] * 2,
        compiler_params=_params(("arbitrary",)),
        name="mixer_out_norm2",
    )(x_rows, y_s5, y_rw, r, k, v, g, mod6, *consts)


def _sort16_pairs():
    pairs = []
    k = 2
    while k <= PEER_TOPK:
        j = k // 2
        while j >= 1:
            pairs += [(i, i ^ j, (i & k) == 0) for i in range(PEER_TOPK) if i ^ j > i]
            j //= 2
        k *= 2
    return pairs


def _bitonic_merge_desc(x):
    x = list(x)
    j = PEER_TOPK // 2
    while j >= 1:
        for i in range(PEER_TOPK):
            if not i & j:
                x[i], x[i | j] = jnp.maximum(x[i], x[i | j]), jnp.minimum(x[i], x[i | j])
        j //= 2
    return x


def _top16_desc(slabs):
    x = list(slabs)
    for i, l, i_max in _sort16_pairs():
        hi, lo = jnp.maximum(x[i], x[l]), jnp.minimum(x[i], x[l])
        x[i], x[l] = (hi, lo) if i_max else (lo, hi)
    for shift in (1, 2, 4):
        x = _bitonic_merge_desc([jnp.maximum(x[v], pltpu.roll(x[PEER_TOPK - 1 - v], shift, 0))
                                 for v in range(PEER_TOPK)])
    return x


def _pair_candidates(a, b):
    sub = lax.broadcasted_iota(jnp.int32, a[0].shape, 0)
    b_lo, b_hi = b[0], b[SUBLANES]
    for s in range(1, SUBLANES):
        b_lo = jnp.where(sub == s, b[s], b_lo)
    for s in range(2, SUBLANES):
        b_hi = jnp.where(sub == s, b[SUBLANES - 1 + s], b_hi)
    cand = [a[i] + b_lo for i in range(PEER_TOPK)]
    cand[15] = jnp.where(sub == 0, cand[15], a[0] + b_hi)
    cand[14] = jnp.where(sub == 1, a[0] + b[PEER_TOPK - 1], cand[14])
    return cand


def _peer_select(xt, wq_ref, k1_ref, k2_ref, q_s, a_s, b_s, e1_s, e2_s, thr_s):
    tg = xt.shape[1]
    q_s[...] = _dot(wq_ref[...], xt, HIGHEST).reshape(PEER_HEADS, 2 * PEER_HALF, tg)
    slabs = lambda s: [s[v * SUBLANES:(v + 1) * SUBLANES, :] for v in range(N_KEYS // SUBLANES)]
    tiled = lambda row: jnp.concatenate([row] * (N_KEYS // SUBLANES), axis=0)

    def head(h, carry):
        q = q_s[h]
        s1 = _dot(k1_ref[h], q[:PEER_HALF], HIGHEST)
        s2 = _dot(k2_ref[h], q[PEER_HALF:], HIGHEST)
        top1 = _top16_desc(slabs(s1))
        top2 = _top16_desc(slabs(s2))
        topc = _top16_desc(_pair_candidates(top1, top2))
        z = jnp.ones_like(topc[0])
        for i in range(1, PEER_TOPK):
            z = z + jnp.exp(topc[i] - topc[0])
        a = jnp.where(s1 >= tiled(top1[-1]), s1, -jnp.inf)
        b = jnp.where(s2 >= tiled(top2[-1]), s2, -jnp.inf)
        a_s[h] = a
        b_s[h] = b
        e1_s[h] = jnp.exp(a - tiled(top1[0])) / tiled(z)
        e2_s[h] = jnp.exp(b - tiled(top2[0]))
        thr_s[h] = topc[-1]
        return carry

    lax.fori_loop(0, PEER_HEADS, head, 0)


def _peer_gates(g, key0, act_ref, pt_ref, a_s, b_s, e1_s, e2_s, thr_s):
    tg = act_ref.shape[2]
    for lg in range(tg // LANES):
        ls = slice(lg * LANES, (lg + 1) * LANES)
        a_rows = [a_s[g, h, pl.ds(key0, PEER_KEYS_PER_BLOCK), ls] for h in range(PEER_HEADS)]
        e_rows = [e1_s[g, h, pl.ds(key0, PEER_KEYS_PER_BLOCK), ls] for h in range(PEER_HEADS)]
        for i in range(PEER_KEYS_PER_BLOCK):
            gate = jnp.zeros((N_KEYS, LANES), F32)
            for h in range(PEER_HEADS):
                score = b_s[g, h, :, ls] + a_rows[h][i:i + 1, :]
                weight = e2_s[g, h, :, ls] * e_rows[h][i:i + 1, :]
                gate = gate + jnp.where(score >= thr_s[g, h, 0:1, ls], weight, 0.0)
            act = _gelu(act_ref[g, i * N_KEYS:(i + 1) * N_KEYS, ls])
            pt_ref[g, i * N_KEYS:(i + 1) * N_KEYS, ls] = (gate * act).astype(BF16)


def _peer_kernel(h2_ref, x1_ref, ga_ref, wq_ref, k1_ref, k2_ref, u_ref, vt_ref, fg_ref, o_ref,
                 xt_s, q_s, a_s, b_s, e1_s, e2_s, thr_s, act0_s, act1_s, pt0_s, pt1_s, acc_s, *, n_seq, n_blocks):
    s = pl.program_id(1)
    n_groups = xt_s.shape[0]
    tg = xt_s.shape[2]
    acts, pts = (act0_s, act1_s), (pt0_s, pt1_s)
    sel = (a_s, b_s, e1_s, e2_s, thr_s)

    def stages(parity, first, gates, second):
        key0 = pl.multiple_of((s - 1) * PEER_KEYS_PER_BLOCK, PEER_KEYS_PER_BLOCK)

        def group(g, carry):
            if first:
                acts[parity][g] = _dot(u_ref[...], xt_s[g])
            if gates:
                _peer_gates(g, key0, acts[1 - parity], pts[1 - parity], *sel)
            if second:
                acc_s[g] += _dot(vt_ref[...], pts[parity][g])
            return carry

        lax.fori_loop(0, n_groups, group, 0)

    @pl.when(s == 0)
    def _():
        def prep(g, carry):
            t0 = pl.multiple_of(g * tg, tg)
            xt = h2_ref[pl.ds(t0, tg), :].T
            xt_s[g] = xt.astype(BF16)
            _peer_select(xt, wq_ref, k1_ref, k2_ref, q_s, a_s.at[g], b_s.at[g], e1_s.at[g], e2_s.at[g],
                         thr_s.at[g])
            return carry

        lax.fori_loop(0, n_groups, prep, 0)
        acc_s[...] = jnp.zeros_like(acc_s)
        pt1_s[...] = jnp.zeros_like(pt1_s)
        stages(0, True, False, False)

    for parity in (0, 1):
        @pl.when((s >= 1) & (s < n_blocks) & (s % 2 == parity))
        def _(parity=parity):
            stages(parity, True, True, True)

    @pl.when(s == n_blocks)
    def _():
        stages(n_blocks % 2, False, True, True)

    @pl.when(s == n_blocks + 1)
    def _():
        stages((n_blocks + 1) % 2, False, False, True)
        for g in range(n_groups):
            rows = slice(g * tg, (g + 1) * tg)
            x2 = x1_ref[rows, :] + _gated(acc_s[g].T, ga_ref[...], min(n_seq, tg))
            o_ref[rows, :] = _rms(x2, fg_ref[...])


def _peer_call(h2, x1, gate2, wq_t, keys1, keys2, u_bf, vt_bf, final_g, n_seq, tm):
    rows, d = h2.shape
    n_exp = u_bf.shape[0]
    eb = PEER_KEYS_PER_BLOCK * N_KEYS
    n_blocks = n_exp // eb
    tg = PEER_TOKEN_GROUP
    n_groups = tm // tg
    assert tm % tg == 0 and (tg % n_seq == 0 or n_seq % tg == 0) and n_exp % eb == 0 and n_blocks >= 2
    if n_seq > tg:
        raise NotImplementedError("more than PEER_TOKEN_GROUP sequences per step")
    tok = pl.BlockSpec((tm, d), lambda i, s: (i, 0))
    sel = pltpu.VMEM((n_groups, PEER_HEADS, N_KEYS, tg), F32)
    act = pltpu.VMEM((n_groups, eb, tg), F32)
    pt = pltpu.VMEM((n_groups, eb, tg), BF16)
    return pl.pallas_call(
        functools.partial(_peer_kernel, n_seq=n_seq, n_blocks=n_blocks),
        grid=(rows // tm, n_blocks + 2),
        in_specs=[tok, tok, _full(gate2.shape), _full(wq_t.shape), _full(keys1.shape), _full(keys2.shape),
                  pl.BlockSpec((eb, d), lambda i, s: (jnp.minimum(s, n_blocks - 1), 0)),
                  pl.BlockSpec((d, eb), lambda i, s: (0, jnp.clip(s - 2, 0, n_blocks - 1))),
                  _full((1, d))],
        out_specs=tok,
        out_shape=jax.ShapeDtypeStruct((rows, d), F32),
        scratch_shapes=[pltpu.VMEM((n_groups, d, tg), BF16), pltpu.VMEM((PEER_HEADS, 2 * PEER_HALF, tg), F32),
                        sel, sel, sel, sel, pltpu.VMEM((n_groups, PEER_HEADS, SUBLANES, tg), F32),
                        act, act, pt, pt, pltpu.VMEM((n_groups, d, tg), F32)],
        compiler_params=_params(("arbitrary", "arbitrary")),
        name="peer_final_norm",
    )(h2, x1, gate2, wq_t, keys1, keys2, u_bf, vt_bf, final_g.reshape(1, d))


def _pick(n, target):
    t = min(n, target)
    while n % t:
        t -= 1
    return t


def _layer(x_btd, mod, s5_re0, s5_im0, wkv0, shift0, prm):
    n_seq, n_t, d = x_btd.shape
    rows = n_seq * n_t
    x_rows = jnp.transpose(x_btd, (1, 0, 2)).reshape(rows, d)
    mod6 = jnp.transpose(mod.reshape(n_seq, 6, d), (1, 0, 2))
    tile = n_seq * _pick(n_t, max(1, 512 // n_seq))

    u_rows, p_rows = _inproj_call(x_rows, mod6, prm["norm1_g"], prm["w_in_bf"], n_seq, tile)

    s5_steps = _pick(n_t, max(1, 512 // n_seq))
    y_s5, s5_re, s5_im = _s5_call(u_rows, s5_re0.reshape(n_seq, S5_LANES), s5_im0.reshape(n_seq, S5_LANES),
                                  prm["s5_consts"], n_seq, s5_steps)

    r, w, k, v, kk, kka, g = _rwkv_pre_call(p_rows, shift0, prm["rwkv_pre_consts"], n_seq, tile)
    v_split = max(1, LANES // (n_seq * RWKV_HEADS))
    pk = lambda a: _to_pairs_k(a, n_t, n_seq, v_split)
    y_pairs, s_pairs = _rwkv_scan_call(pk(r), pk(w), pk(k), pk(kk), pk(kka), _to_pairs_v(v, n_t, n_seq, v_split),
                                       _state_to_pairs(wkv0, v_split), _pick(n_t, 16))
    y_rw = _from_pairs_v(y_pairs, n_t, n_seq, v_split)
    wkv = _state_from_pairs(s_pairs, n_seq, v_split)

    x1, h2 = _mixout_call(x_rows, y_s5, y_rw, r, k, v, g, mod6, prm["mixout_consts"], n_seq, tile)

    tm = n_seq * _pick(n_t, max(1, 512 // n_seq))
    y_rows = _peer_call(h2, x1, mod6[5], prm["peer_wq_t"], prm["peer_keys1"], prm["peer_keys2"], prm["peer_u_bf"],
                        prm["peer_vt_bf"], prm["final_norm_g"], n_seq, tm)
    y = jnp.transpose(y_rows.reshape(n_t, n_seq, d), (1, 0, 2))
    shift = p_rows[rows - n_seq:]
    return (y, s5_re.reshape(n_seq, S5_GROUPS, S5_STATE), s5_im.reshape(n_seq, S5_GROUPS, S5_STATE), wkv, shift)


def kernel(x_prompt, x_sample, state_s5_re, state_s5_im, state_wkv, state_shift, c_prompt, c_sample, w_ada, b_ada, norm1_g, norm2_g, w_in, w_out, s5_a_re, s5_a_im, s5_log_dt, s5_b_re, s5_b_im, s5_c_re, s5_c_im, s5_d, w_glu, b_glu, rwkv_mu, rwkv_w0, rwkv_w2, rwkv_a0, rwkv_a2, rwkv_g2, rwkv_k_k, rwkv_k_a, rwkv_r_k, rwkv_gn_w, rwkv_gn_b, peer_w_q, peer_keys1, peer_keys2, peer_u, peer_v, final_norm_g):
    assert w_ada.shape[0] == 1, "single-layer model"
    nbp = x_prompt.shape[0]
    row = lambda a: a.reshape(1, -1)

    mod = _ada_call(jnp.concatenate([c_prompt, c_sample], axis=0).astype(F32), w_ada[0], b_ada[0])

    lb_re, lb_im, bw_re, bw_im = _s5_prep_call(s5_a_re[0], s5_a_im[0], s5_log_dt[0], s5_b_re[0], s5_b_im[0])
    head_ones = jnp.kron(jnp.eye(RWKV_HEADS, dtype=F32), jnp.ones((RWKV_HEAD, RWKV_HEAD), F32))
    lora_pad = jnp.zeros((RWKV_LORA // 2, RWKV_WIDTH), F32)
    prm = {
        "norm1_g": norm1_g[0],
        "w_in_bf": w_in[0].astype(BF16),
        "s5_consts": (lb_re, lb_im, bw_re, bw_im, _s5_out_blockdiag(s5_c_re[0]), _s5_out_blockdiag(s5_c_im[0]),
                      row(s5_d[0]), w_glu[0].astype(BF16), row(b_glu[0])),
        "rwkv_pre_consts": (row(rwkv_mu[0]), row(rwkv_w0[0]), jnp.concatenate([rwkv_w2[0], lora_pad], axis=0),
                            row(rwkv_a0[0]), jnp.concatenate([lora_pad, rwkv_a2[0]], axis=0), rwkv_g2[0],
                            row(rwkv_k_k[0]), row(rwkv_k_a[0]), head_ones),
        "mixout_consts": (row(rwkv_gn_w[0]), row(rwkv_gn_b[0]), row(rwkv_r_k[0]), head_ones,
                          w_out[0].astype(BF16), row(norm2_g[0])),
        "peer_wq_t": peer_w_q[0].T,
        "peer_keys1": peer_keys1[0],
        "peer_keys2": peer_keys2[0],
        "peer_u_bf": peer_u[0].astype(BF16),
        "peer_vt_bf": peer_v[0].T.astype(BF16),
        "final_norm_g": final_norm_g,
    }

    z_s5 = jnp.zeros((nbp, S5_GROUPS, S5_STATE), F32)
    z_wkv = jnp.zeros((nbp, RWKV_HEADS, RWKV_HEAD, RWKV_HEAD), F32)
    z_sh = jnp.zeros((nbp, RWKV_COLS), F32)
    yp, pr, pi, pw, psh = _layer(x_prompt.astype(F32), mod[:nbp], z_s5, z_s5, z_wkv, z_sh, prm)
    ys, sr, si, sw, ssh = _layer(x_sample.astype(F32), mod[nbp:], state_s5_re[0].astype(F32),
                                 state_s5_im[0].astype(F32), state_wkv[0].astype(F32),
                                 state_shift[0].astype(F32), prm)
    return (yp.astype(x_prompt.dtype), ys.astype(x_sample.dtype), pr[None], pi[None], pw[None], psh[None],
            sr[None], si[None], sw[None], ssh[None])
```

```python
import functools

import jax
import jax.numpy as jnp
from jax import lax
from jax.experimental import pallas as pl
from jax.experimental.pallas import tpu as pltpu

F32 = jnp.float32
BF16 = jnp.bfloat16
HIGHEST = lax.Precision.HIGHEST

LANES = 128
SUBLANES = 8
VMEM_LIMIT_BYTES = 56 * 1024 * 1024

D_MODEL = 1024
S5_WIDTH = 512
S5_GROUP = 16
S5_GROUPS = 32
S5_STATE = 64
S5_LANES = S5_GROUPS * S5_STATE
S5_CHUNKS = 4
S5_CHUNK_IN = S5_WIDTH // S5_CHUNKS
S5_CHUNK_ST = S5_LANES // S5_CHUNKS
RWKV_WIDTH = 512
RWKV_HEAD = 64
RWKV_HEADS = 8
RWKV_LORA = 128
RWKV_COLS = 3 * RWKV_WIDTH + 64 + 64 + 128
PEER_HEADS = 8
N_KEYS = 128
PEER_TOPK = 16
PEER_HALF = 64
PEER_TOKEN_GROUP = 256
PEER_KEYS_PER_BLOCK = 8
NORM_EPS = 1e-6
GN_EPS = 64e-5


def _params(sem):
    return pltpu.CompilerParams(dimension_semantics=sem, vmem_limit_bytes=VMEM_LIMIT_BYTES)


def _full(shape):
    return pl.BlockSpec(shape, lambda *_: (0,) * len(shape))


def _dot(a, b, precision=None):
    return jnp.dot(a, b, precision=precision, preferred_element_type=F32)


def _gelu(x):
    return 0.5 * x * (1.0 + lax.erf(x * (2.0 ** -0.5)))


def _rms(x, g):
    return x * lax.rsqrt(jnp.mean(x * x, axis=-1, keepdims=True) + NORM_EPS) * g


def _modulate(h, shift, scale, n_seq):
    rows, d = h.shape
    h3 = h.reshape(rows // n_seq, n_seq, d)
    return (h3 * (1.0 + scale)[None] + shift[None]).reshape(rows, d)


def _gated(h, gate, n_seq):
    rows, d = h.shape
    return (h.reshape(rows // n_seq, n_seq, d) * gate[None]).reshape(rows, d)


def _ada_kernel(c_ref, w_ref, b_ref, o_ref):
    s = jax.nn.silu(c_ref[...])
    o_ref[...] = _dot(s, w_ref[...], HIGHEST) + b_ref[...]


def _ada_call(c_all, w_ada, b_ada):
    n, d = c_all.shape
    cols = w_ada.shape[1]
    return pl.pallas_call(
        _ada_kernel,
        grid=(cols // d,),
        in_specs=[_full((n, d)), pl.BlockSpec((d, d), lambda j: (0, j)), pl.BlockSpec((1, d), lambda j: (0, j))],
        out_specs=pl.BlockSpec((n, d), lambda j: (0, j)),
        out_shape=jax.ShapeDtypeStruct((n, cols), F32),
        compiler_params=_params(("arbitrary",)),
        name="adaln_mod",
    )(c_all, w_ada, b_ada.reshape(1, cols))


def _inproj_kernel(x_ref, mod_ref, g_ref, w_ref, u_ref, p_ref, *, n_seq):
    h = _modulate(_rms(x_ref[...], g_ref[...]), mod_ref[0], mod_ref[1], n_seq)
    proj = _dot(h.astype(BF16), w_ref[...])
    u_ref[...] = proj[:, :S5_WIDTH]
    p_ref[...] = proj[:, S5_WIDTH:]


def _inproj_call(x_rows, mod6, norm_g, w_in_bf, n_seq, tile):
    rows, d = x_rows.shape
    return pl.pallas_call(
        functools.partial(_inproj_kernel, n_seq=n_seq),
        grid=(rows // tile,),
        in_specs=[pl.BlockSpec((tile, d), lambda i: (i, 0)), _full(mod6.shape), _full((1, d)),
                  _full(w_in_bf.shape)],
        out_specs=[pl.BlockSpec((tile, S5_WIDTH), lambda i: (i, 0)),
                   pl.BlockSpec((tile, RWKV_COLS), lambda i: (i, 0))],
        out_shape=[jax.ShapeDtypeStruct((rows, S5_WIDTH), F32), jax.ShapeDtypeStruct((rows, RWKV_COLS), F32)],
        compiler_params=_params(("arbitrary",)),
        name="norm1_inproj",
    )(x_rows, mod6, norm_g.reshape(1, d), w_in_bf)


def _s5_prep_kernel(are_ref, aim_ref, ldt_ref, bre_ref, bim_ref, lbr_ref, lbi_ref, wre_ref, wim_ref):
    lam_re, lam_im = are_ref[...], aim_ref[...]
    dt = jnp.exp(ldt_ref[...])
    mag = jnp.exp(lam_re * dt)
    ang = lam_im * dt
    lb_re, lb_im = mag * jnp.cos(ang), mag * jnp.sin(ang)
    den = lam_re * lam_re + lam_im * lam_im
    n_re, n_im = lb_re - 1.0, lb_im
    coef_re = (n_re * lam_re + n_im * lam_im) / den
    coef_im = (n_im * lam_re - n_re * lam_im) / den
    lbr_ref[...] = lb_re
    lbi_ref[...] = lb_im
    for c in range(S5_CHUNKS):
        cr = coef_re[:, c * S5_CHUNK_ST:(c + 1) * S5_CHUNK_ST]
        ci = coef_im[:, c * S5_CHUNK_ST:(c + 1) * S5_CHUNK_ST]
        wre_ref[c] = cr * bre_ref[c] - ci * bim_ref[c]
        wim_ref[c] = cr * bim_ref[c] + ci * bre_ref[c]


def _s5_prep_call(a_re, a_im, log_dt, b_re, b_im):
    row = lambda a: a.reshape(1, S5_LANES)
    ldt = jnp.repeat(log_dt, S5_STATE).reshape(1, S5_LANES)
    w_shape = (S5_CHUNKS, S5_CHUNK_IN, S5_CHUNK_ST)
    return pl.pallas_call(
        _s5_prep_kernel,
        out_shape=[jax.ShapeDtypeStruct((1, S5_LANES), F32)] * 2 + [jax.ShapeDtypeStruct(w_shape, F32)] * 2,
        compiler_params=pltpu.CompilerParams(vmem_limit_bytes=VMEM_LIMIT_BYTES),
        name="s5_discretise",
    )(row(a_re), row(a_im), ldt, _s5_in_blockdiag(b_re), _s5_in_blockdiag(b_im))


def _s5_in_blockdiag(b):
    gpc = S5_GROUPS // S5_CHUNKS
    bt = jnp.transpose(b, (0, 2, 1)).reshape(S5_CHUNKS, gpc, S5_GROUP, S5_STATE)
    eye = jnp.eye(gpc, dtype=b.dtype)
    bd = bt[:, :, :, None, :] * eye[None, :, None, :, None]
    return bd.reshape(S5_CHUNKS, S5_CHUNK_IN, S5_CHUNK_ST)


def _s5_out_blockdiag(c):
    gpc = S5_GROUPS // S5_CHUNKS
    ct = jnp.transpose(c, (0, 2, 1)).reshape(S5_CHUNKS, gpc, S5_STATE, S5_GROUP)
    eye = jnp.eye(gpc, dtype=c.dtype)
    bd = ct[:, :, :, None, :] * eye[None, :, None, :, None]
    return bd.reshape(S5_CHUNKS, S5_CHUNK_ST, S5_CHUNK_IN)


def _s5_kernel(u_ref, h0r_ref, h0i_ref, lbr_ref, lbi_ref, wre_ref, wim_ref, cre_ref, cim_ref, d_ref, wglu_ref,
               bglu_ref, y_ref, hr_out, hi_out, re_s, im_s, str_s, sti_s, *, n_seq, steps):
    chunk = pl.program_id(0)

    @pl.when(chunk == 0)
    def _():
        str_s[...] = h0r_ref[...]
        sti_s[...] = h0i_ref[...]

    u = u_ref[...]
    for c in range(S5_CHUNKS):
        uc = u[:, c * S5_CHUNK_IN:(c + 1) * S5_CHUNK_IN]
        re_s[:, c * S5_CHUNK_ST:(c + 1) * S5_CHUNK_ST] = _dot(uc, wre_ref[c], HIGHEST)
        im_s[:, c * S5_CHUNK_ST:(c + 1) * S5_CHUNK_ST] = _dot(uc, wim_ref[c], HIGHEST)

    def seq_block(rb, carry):
        r0 = pl.multiple_of(rb * SUBLANES, SUBLANES)
        for c in range(S5_CHUNKS):
            lanes = slice(c * S5_CHUNK_ST, (c + 1) * S5_CHUNK_ST)
            lr = jnp.broadcast_to(lbr_ref[:, lanes], (SUBLANES, S5_CHUNK_ST))
            li = jnp.broadcast_to(lbi_ref[:, lanes], (SUBLANES, S5_CHUNK_ST))

            def step(t, h):
                hr, hi = h
                row = pl.multiple_of(t * n_seq + r0, SUBLANES)
                nr = lr * hr - li * hi + re_s[pl.ds(row, SUBLANES), lanes]
                ni = lr * hi + li * hr + im_s[pl.ds(row, SUBLANES), lanes]
                re_s[pl.ds(row, SUBLANES), lanes] = nr
                im_s[pl.ds(row, SUBLANES), lanes] = ni
                return nr, ni

            h0 = (str_s[pl.ds(r0, SUBLANES), lanes], sti_s[pl.ds(r0, SUBLANES), lanes])
            hr, hi = lax.fori_loop(0, steps, step, h0, unroll=min(steps, 8))
            str_s[pl.ds(r0, SUBLANES), lanes] = hr
            sti_s[pl.ds(r0, SUBLANES), lanes] = hi
        return carry

    lax.fori_loop(0, n_seq // SUBLANES, seq_block, 0)

    ys = []
    for c in range(S5_CHUNKS):
        lanes = slice(c * S5_CHUNK_ST, (c + 1) * S5_CHUNK_ST)
        ys.append(_dot(re_s[:, lanes], cre_ref[c], HIGHEST) - _dot(im_s[:, lanes], cim_ref[c], HIGHEST))
    y = jnp.concatenate(ys, axis=-1) + d_ref[...] * u
    y = _gelu(y)
    y_ref[...] = y * jax.nn.sigmoid(_dot(y.astype(BF16), wglu_ref[...]) + bglu_ref[...])

    @pl.when(chunk == pl.num_programs(0) - 1)
    def _():
        hr_out[...] = str_s[...]
        hi_out[...] = sti_s[...]


def _s5_call(u_rows, h0_re, h0_im, consts, n_seq, steps):
    rows = u_rows.shape[0]
    tile = n_seq * steps
    lb_re, lb_im, w_re, w_im, c_re, c_im, d_skip, w_glu_bf, b_glu = consts
    state = jax.ShapeDtypeStruct((n_seq, S5_LANES), F32)
    args = (u_rows, h0_re, h0_im, lb_re, lb_im, w_re, w_im, c_re, c_im, d_skip, w_glu_bf, b_glu)
    in_specs = [pl.BlockSpec((tile, S5_WIDTH), lambda i: (i, 0))] + [_full(a.shape) for a in args[1:]]
    return pl.pallas_call(
        functools.partial(_s5_kernel, n_seq=n_seq, steps=steps),
        grid=(rows // tile,),
        in_specs=in_specs,
        out_specs=[pl.BlockSpec((tile, S5_WIDTH), lambda i: (i, 0)), _full(state.shape), _full(state.shape)],
        out_shape=[jax.ShapeDtypeStruct((rows, S5_WIDTH), F32), state, state],
        scratch_shapes=[pltpu.VMEM((tile, S5_LANES), F32), pltpu.VMEM((tile, S5_LANES), F32),
                        pltpu.VMEM((n_seq, S5_LANES), F32), pltpu.VMEM((n_seq, S5_LANES), F32)],
        compiler_params=_params(("arbitrary",)),
        name="s5_mixer",
    )(*args)


def _head_sum(x, ones_ref):
    return _dot(x, ones_ref[...], HIGHEST)


def _rwkv_pre_kernel(p_ref, prev_ref, shift_ref, mu_ref, w0_ref, w2_ref, a0_ref, a2_ref, g2_ref, kk_ref, ka_ref,
                     ones_ref, r_out, w_out, k_out, v_out, kk_out, kka_out, g_out, *, n_seq):
    p = p_ref[...]
    tile = p.shape[0]
    head = jnp.where(pl.program_id(0) == 0, shift_ref[...], prev_ref[...])
    p_prev = head if tile == n_seq else jnp.concatenate([head, p[:tile - n_seq]], axis=0)
    ps = p + (p_prev - p) * mu_ref[...]
    w = RWKV_WIDTH
    r, k, v = ps[:, :w], ps[:, w:2 * w], ps[:, 2 * w:3 * w]
    lo = ps[:, 3 * w:3 * w + RWKV_LORA]
    g_lo = ps[:, 3 * w + RWKV_LORA:]
    w_raw = -jax.nn.softplus(-(w0_ref[...] + _dot(jnp.tanh(lo), w2_ref[...], HIGHEST))) - 0.5
    a = jax.nn.sigmoid(a0_ref[...] + _dot(lo, a2_ref[...], HIGHEST))
    kk = k * kk_ref[...]
    norm = jnp.sqrt(_head_sum(kk * kk, ones_ref))
    kk = kk / jnp.maximum(norm, 1e-12)
    r_out[...] = r
    w_out[...] = jnp.exp(-jnp.exp(w_raw))
    k_out[...] = k * (1.0 + (a - 1.0) * ka_ref[...])
    v_out[...] = v
    kk_out[...] = kk
    kka_out[...] = kk * a
    g_out[...] = _dot(jax.nn.sigmoid(g_lo), g2_ref[...], HIGHEST)


def _rwkv_pre_call(p_rows, shift0, consts, n_seq, tile):
    rows = p_rows.shape[0]
    per = tile // n_seq
    vec = jax.ShapeDtypeStruct((rows, RWKV_WIDTH), F32)
    in_specs = [pl.BlockSpec((tile, RWKV_COLS), lambda i: (i, 0)),
                pl.BlockSpec((n_seq, RWKV_COLS), lambda i: (jnp.maximum(i * per - 1, 0), 0)),
                _full(shift0.shape)] + [_full(a.shape) for a in consts]
    return pl.pallas_call(
        functools.partial(_rwkv_pre_kernel, n_seq=n_seq),
        grid=(rows // tile,),
        in_specs=in_specs,
        out_specs=[pl.BlockSpec((tile, RWKV_WIDTH), lambda i: (i, 0))] * 7,
        out_shape=[vec] * 7,
        compiler_params=_params(("arbitrary",)),
        name="rwkv_prologue",
    )(p_rows, p_rows, shift0, *consts)


def _sublane_allsum(p):
    p = p + pltpu.roll(p, 4, 0)
    p = p + pltpu.roll(p, 2, 0)
    return p + pltpu.roll(p, 1, 0)


def _pairs_from_rows(x, nb):
    cols = [x[:, c * LANES:(c + 1) * LANES] for c in range(RWKV_WIDTH // LANES)]
    m = jnp.concatenate(cols * (LANES // (4 * nb)), axis=0)
    mt = m.T
    lane = lax.broadcasted_iota(jnp.int32, (RWKV_HEAD, LANES), 1)
    even_head = (lane // nb) % RWKV_HEADS < RWKV_HEADS // 2
    return jnp.where(even_head, mt[:RWKV_HEAD], pltpu.roll(mt[RWKV_HEAD:], 4 * nb, 1))


def _rows_from_pairs(y, nb):
    pairs = RWKV_HEADS * nb
    shifts = [(-(copy * pairs + parity * pairs // 2)) % LANES
              for parity in range(2) for copy in range(LANES // pairs)]
    mt = jnp.concatenate([y if s == 0 else pltpu.roll(y, s, 1) for s in shifts], axis=0)
    m = mt.T
    return jnp.concatenate([m[c * nb:(c + 1) * nb, :] for c in range(RWKV_WIDTH // LANES)], axis=1)


def _rwkv_scan_kernel(r_ref, w_ref, k_ref, kk_ref, kka_ref, v_ref, s0_ref, y_ref, sout_ref, s_s, ka_s, kb_s, va_s,
                      vb_s, *, steps, n_v, nb):
    tc = pl.program_id(1)

    @pl.when(tc == 0)
    def _():
        s_s[...] = s0_ref[...]

    row_id = lax.broadcasted_iota(jnp.int32, (SUBLANES, LANES), 0)
    lane_id = lax.broadcasted_iota(jnp.int32, (n_v, LANES), 1)
    split = lambda x: x.reshape(RWKV_HEAD // SUBLANES, SUBLANES, LANES)
    k_refs = (r_ref, w_ref, k_ref, kk_ref, kka_ref)

    def stage(t, k_s, v_s):
        for i, ref in enumerate(k_refs):
            k_s[i] = _pairs_from_rows(ref[t], nb)
        full = _pairs_from_rows(v_ref[t], nb)
        v_s[...] = full if n_v == RWKV_HEAD else jnp.where(lane_id < LANES // 2, full[:n_v], full[n_v:])

    def advance(t, k_s, v_s):
        r, w, k, kk, kka = (split(k_s[i]) for i in range(len(k_refs)))
        y_tiles = []
        for vb in range(n_v // SUBLANES):
            v_tile = v_s[vb * SUBLANES:(vb + 1) * SUBLANES, :]
            y_tile = jnp.zeros((SUBLANES, LANES), F32)
            for j in range(SUBLANES):
                s = split(s_s[vb * SUBLANES + j])
                sa = -_sublane_allsum(jnp.sum(s * kk, axis=0))
                v_row = jnp.broadcast_to(v_tile[j:j + 1, :], (SUBLANES, LANES))
                s = s * w + sa[None] * kka + v_row[None] * k
                s_s[vb * SUBLANES + j] = s.reshape(RWKV_HEAD, LANES)
                y_tile = jnp.where(row_id == j, _sublane_allsum(jnp.sum(s * r, axis=0)), y_tile)
            y_tiles.append(y_tile)
        y_ref[t] = _rows_from_pairs(jnp.concatenate(y_tiles, axis=0), nb)

    stage(0, ka_s, va_s)

    def two_steps(i, carry):
        t = 2 * i
        stage(t + 1, kb_s, vb_s)
        advance(t, ka_s, va_s)
        stage(jnp.minimum(t + 2, steps - 1), ka_s, va_s)
        advance(t + 1, kb_s, vb_s)
        return carry

    lax.fori_loop(0, steps // 2, two_steps, 0)

    @pl.when(tc == pl.num_programs(1) - 1)
    def _():
        sout_ref[...] = s_s[...]


def _rwkv_scan_call(r, w, k, kk, kka, v, s0, n_seq, steps):
    rows = r.shape[0]
    n_t = rows // n_seq
    n_v, _, lanes = s0.shape
    nb = n_seq // (lanes // LANES)
    assert steps % 2 == 0 and n_t % steps == 0
    as_steps = lambda a: a.reshape(n_t, n_seq, RWKV_WIDTH)
    xspec = pl.BlockSpec((steps, nb, RWKV_WIDTH), lambda g, i: (i, g, 0))
    sspec = pl.BlockSpec((n_v, RWKV_HEAD, LANES), lambda g, i: (0, 0, g))
    stage_k = pltpu.VMEM((5, RWKV_HEAD, LANES), F32)
    stage_v = pltpu.VMEM((n_v, LANES), F32)
    y, s = pl.pallas_call(
        functools.partial(_rwkv_scan_kernel, steps=steps, n_v=n_v, nb=nb),
        grid=(lanes // LANES, n_t // steps),
        in_specs=[xspec] * 6 + [sspec],
        out_specs=[xspec, sspec],
        out_shape=[jax.ShapeDtypeStruct((n_t, n_seq, RWKV_WIDTH), F32), jax.ShapeDtypeStruct(s0.shape, F32)],
        scratch_shapes=[pltpu.VMEM((n_v, RWKV_HEAD, LANES), F32), stage_k, stage_k, stage_v, stage_v],
        compiler_params=_params(("arbitrary", "arbitrary")),
        name="rwkv_recurrence",
    )(*[as_steps(a) for a in (r, w, k, kk, kka, v)], s0)
    return y.reshape(rows, RWKV_WIDTH), s


def _state_to_pairs(s, nb):
    n_seq = s.shape[0]
    v_split = LANES // (RWKV_HEADS * nb)
    n_v = RWKV_HEAD // v_split
    s = s.reshape(n_seq // nb, nb, RWKV_HEADS // 2, 2, v_split, n_v, RWKV_HEAD)
    return jnp.transpose(s, (5, 6, 0, 4, 3, 2, 1)).reshape(n_v, RWKV_HEAD, n_seq // nb * LANES)


def _state_from_pairs(s, n_seq, nb):
    v_split = LANES // (RWKV_HEADS * nb)
    n_v = RWKV_HEAD // v_split
    s = s.reshape(n_v, RWKV_HEAD, n_seq // nb, v_split, 2, RWKV_HEADS // 2, nb)
    return jnp.transpose(s, (2, 6, 5, 4, 3, 0, 1)).reshape(n_seq, RWKV_HEADS, RWKV_HEAD, RWKV_HEAD)


def _mixout_kernel(x_ref, ys5_ref, yrw_ref, r_ref, k_ref, v_ref, g_ref, mod_ref, gnw_ref, gnb_ref, rk_ref,
                   ones_ref, wout_ref, n2g_ref, x1_ref, h2_ref, *, n_seq):
    y = yrw_ref[...]
    inv_n = 1.0 / RWKV_HEAD
    mean = _head_sum(y, ones_ref) * inv_n
    yc = y - mean
    var = _head_sum(yc * yc, ones_ref) * inv_n
    y = yc * lax.rsqrt(var + GN_EPS) * gnw_ref[...] + gnb_ref[...]
    v = v_ref[...]
    y = y + _head_sum(r_ref[...] * k_ref[...] * rk_ref[...], ones_ref) * v
    y = y * g_ref[...]
    mix = jnp.concatenate([ys5_ref[...], y], axis=-1).astype(BF16)
    x1 = x_ref[...] + _gated(_dot(mix, wout_ref[...]), mod_ref[2], n_seq)
    x1_ref[...] = x1
    h2_ref[...] = _modulate(_rms(x1, n2g_ref[...]), mod_ref[3], mod_ref[4], n_seq)


def _mixout_call(x_rows, y_s5, y_rw, r, k, v, g, mod6, consts, n_seq, tile):
    rows, d = x_rows.shape
    wide = pl.BlockSpec((tile, d), lambda i: (i, 0))
    half = pl.BlockSpec((tile, RWKV_WIDTH), lambda i: (i, 0))
    return pl.pallas_call(
        functools.partial(_mixout_kernel, n_seq=n_seq),
        grid=(rows // tile,),
        in_specs=[wide] + [half] * 6 + [_full(mod6.shape)] + [_full(a.shape) for a in consts],
        out_specs=[wide, wide],
        out_shape=[jax.ShapeDtypeStruct((rows, d), F32)] * 2,
        compiler_params=_params(("arbitrary",)),
        name="mixer_out_norm2",
    )(x_rows, y_s5, y_rw, r, k, v, g, mod6, *consts)


def _sort16_pairs():
    pairs = []
    k = 2
    while k <= PEER_TOPK:
        j = k // 2
        while j >= 1:
            pairs += [(i, i ^ j, (i & k) == 0) for i in range(PEER_TOPK) if i ^ j > i]
            j //= 2
        k *= 2
    return pairs


def _bitonic_merge_desc(x):
    x = list(x)
    j = PEER_TOPK // 2
    while j >= 1:
        for i in range(PEER_TOPK):
            if not i & j:
                x[i], x[i | j] = jnp.maximum(x[i], x[i | j]), jnp.minimum(x[i], x[i | j])
        j //= 2
    return x


def _top16_desc(slabs):
    x = list(slabs)
    for i, l, i_max in _sort16_pairs():
        hi, lo = jnp.maximum(x[i], x[l]), jnp.minimum(x[i], x[l])
        x[i], x[l] = (hi, lo) if i_max else (lo, hi)
    for shift in (1, 2, 4):
        x = _bitonic_merge_desc([jnp.maximum(x[v], pltpu.roll(x[PEER_TOPK - 1 - v], shift, 0))
                                 for v in range(PEER_TOPK)])
    return x


def _pair_candidates(a, b):
    sub = lax.broadcasted_iota(jnp.int32, a[0].shape, 0)
    b_lo, b_hi = b[0], b[SUBLANES]
    for s in range(1, SUBLANES):
        b_lo = jnp.where(sub == s, b[s], b_lo)
    for s in range(2, SUBLANES):
        b_hi = jnp.where(sub == s, b[SUBLANES - 1 + s], b_hi)
    cand = [a[i] + b_lo for i in range(PEER_TOPK)]
    cand[15] = jnp.where(sub == 0, cand[15], a[0] + b_hi)
    cand[14] = jnp.where(sub == 1, a[0] + b[PEER_TOPK - 1], cand[14])
    return cand


def _peer_select(xt, wq_ref, k1_ref, k2_ref, q_s, a_s, b_s, e1_s, e2_s, thr_s):
    tg = xt.shape[1]
    q_s[...] = _dot(wq_ref[...], xt, HIGHEST).reshape(PEER_HEADS, 2 * PEER_HALF, tg)
    slabs = lambda s: [s[v * SUBLANES:(v + 1) * SUBLANES, :] for v in range(N_KEYS // SUBLANES)]
    tiled = lambda row: jnp.concatenate([row] * (N_KEYS // SUBLANES), axis=0)

    def head(h, carry):
        q = q_s[h]
        s1 = _dot(k1_ref[h], q[:PEER_HALF], HIGHEST)
        s2 = _dot(k2_ref[h], q[PEER_HALF:], HIGHEST)
        top1 = _top16_desc(slabs(s1))
        top2 = _top16_desc(slabs(s2))
        topc = _top16_desc(_pair_candidates(top1, top2))
        z = jnp.ones_like(topc[0])
        for i in range(1, PEER_TOPK):
            z = z + jnp.exp(topc[i] - topc[0])
        a = jnp.where(s1 >= tiled(top1[-1]), s1, -jnp.inf)
        b = jnp.where(s2 >= tiled(top2[-1]), s2, -jnp.inf)
        a_s[h] = a
        b_s[h] = b
        e1_s[h] = jnp.exp(a - tiled(top1[0])) / tiled(z)
        e2_s[h] = jnp.exp(b - tiled(top2[0]))
        thr_s[h] = topc[-1]
        return carry

    lax.fori_loop(0, PEER_HEADS, head, 0)


def _peer_gates(g, key0, act_ref, pt_ref, a_s, b_s, e1_s, e2_s, thr_s):
    tg = act_ref.shape[2]
    for lg in range(tg // LANES):
        ls = slice(lg * LANES, (lg + 1) * LANES)
        a_rows = [a_s[g, h, pl.ds(key0, PEER_KEYS_PER_BLOCK), ls] for h in range(PEER_HEADS)]
        e_rows = [e1_s[g, h, pl.ds(key0, PEER_KEYS_PER_BLOCK), ls] for h in range(PEER_HEADS)]
        for i in range(PEER_KEYS_PER_BLOCK):
            gate = jnp.zeros((N_KEYS, LANES), F32)
            for h in range(PEER_HEADS):
                score = b_s[g, h, :, ls] + a_rows[h][i:i + 1, :]
                weight = e2_s[g, h, :, ls] * e_rows[h][i:i + 1, :]
                gate = gate + jnp.where(score >= thr_s[g, h, 0:1, ls], weight, 0.0)
            act = _gelu(act_ref[g, i * N_KEYS:(i + 1) * N_KEYS, ls])
            pt_ref[g, i * N_KEYS:(i + 1) * N_KEYS, ls] = (gate * act).astype(BF16)


def _peer_kernel(h2_ref, x1_ref, ga_ref, wq_ref, k1_ref, k2_ref, u_ref, vt_ref, fg_ref, o_ref,
                 xt_s, q_s, a_s, b_s, e1_s, e2_s, thr_s, act0_s, act1_s, pt0_s, pt1_s, acc_s, *, n_seq, n_blocks):
    s = pl.program_id(1)
    n_groups = xt_s.shape[0]
    tg = xt_s.shape[2]
    acts, pts = (act0_s, act1_s), (pt0_s, pt1_s)
    sel = (a_s, b_s, e1_s, e2_s, thr_s)

    def stages(parity, first, gates, second):
        key0 = pl.multiple_of((s - 1) * PEER_KEYS_PER_BLOCK, PEER_KEYS_PER_BLOCK)

        def group(g, carry):
            if first:
                acts[parity][g] = _dot(u_ref[...], xt_s[g])
            if gates:
                _peer_gates(g, key0, acts[1 - parity], pts[1 - parity], *sel)
            if second:
                acc_s[g] += _dot(vt_ref[...], pts[parity][g])
            return carry

        lax.fori_loop(0, n_groups, group, 0)

    @pl.when(s == 0)
    def _():
        def prep(g, carry):
            t0 = pl.multiple_of(g * tg, tg)
            xt = h2_ref[pl.ds(t0, tg), :].T
            xt_s[g] = xt.astype(BF16)
            _peer_select(xt, wq_ref, k1_ref, k2_ref, q_s, a_s.at[g], b_s.at[g], e1_s.at[g], e2_s.at[g],
                         thr_s.at[g])
            return carry

        lax.fori_loop(0, n_groups, prep, 0)
        acc_s[...] = jnp.zeros_like(acc_s)
        pt1_s[...] = jnp.zeros_like(pt1_s)
        stages(0, True, False, False)

    for parity in (0, 1):
        @pl.when((s >= 1) & (s < n_blocks) & (s % 2 == parity))
        def _(parity=parity):
            stages(parity, True, True, True)

    @pl.when(s == n_blocks)
    def _():
        stages(n_blocks % 2, False, True, True)

    @pl.when(s == n_blocks + 1)
    def _():
        stages((n_blocks + 1) % 2, False, False, True)
        for g in range(n_groups):
            rows = slice(g * tg, (g + 1) * tg)
            x2 = x1_ref[rows, :] + _gated(acc_s[g].T, ga_ref[...], min(n_seq, tg))
            o_ref[rows, :] = _rms(x2, fg_ref[...])


def _peer_call(h2, x1, gate2, wq_t, keys1, keys2, u_bf, vt_bf, final_g, n_seq, tm):
    rows, d = h2.shape
    n_exp = u_bf.shape[0]
    eb = PEER_KEYS_PER_BLOCK * N_KEYS
    n_blocks = n_exp // eb
    tg = PEER_TOKEN_GROUP
    n_groups = tm // tg
    assert tm % tg == 0 and (tg % n_seq == 0 or n_seq % tg == 0) and n_exp % eb == 0 and n_blocks >= 2
    if n_seq > tg:
        raise NotImplementedError("more than PEER_TOKEN_GROUP sequences per step")
    tok = pl.BlockSpec((tm, d), lambda i, s: (i, 0))
    sel = pltpu.VMEM((n_groups, PEER_HEADS, N_KEYS, tg), F32)
    act = pltpu.VMEM((n_groups, eb, tg), F32)
    pt = pltpu.VMEM((n_groups, eb, tg), BF16)
    return pl.pallas_call(
        functools.partial(_peer_kernel, n_seq=n_seq, n_blocks=n_blocks),
        grid=(rows // tm, n_blocks + 2),
        in_specs=[tok, tok, _full(gate2.shape), _full(wq_t.shape), _full(keys1.shape), _full(keys2.shape),
                  pl.BlockSpec((eb, d), lambda i, s: (jnp.minimum(s, n_blocks - 1), 0)),
                  pl.BlockSpec((d, eb), lambda i, s: (0, jnp.clip(s - 2, 0, n_blocks - 1))),
                  _full((1, d))],
        out_specs=tok,
        out_shape=jax.ShapeDtypeStruct((rows, d), F32),
        scratch_shapes=[pltpu.VMEM((n_groups, d, tg), BF16), pltpu.VMEM((PEER_HEADS, 2 * PEER_HALF, tg), F32),
                        sel, sel, sel, sel, pltpu.VMEM((n_groups, PEER_HEADS, SUBLANES, tg), F32),
                        act, act, pt, pt, pltpu.VMEM((n_groups, d, tg), F32)],
        compiler_params=_params(("arbitrary", "arbitrary")),
        name="peer_final_norm",
    )(h2, x1, gate2, wq_t, keys1, keys2, u_bf, vt_bf, final_g.reshape(1, d))


def _pick(n, target):
    t = min(n, target)
    while n % t:
        t -= 1
    return t


def _layer(x_btd, mod, s5_re0, s5_im0, wkv0, shift0, prm):
    n_seq, n_t, d = x_btd.shape
    rows = n_seq * n_t
    x_rows = jnp.transpose(x_btd, (1, 0, 2)).reshape(rows, d)
    mod6 = jnp.transpose(mod.reshape(n_seq, 6, d), (1, 0, 2))
    tile = n_seq * _pick(n_t, max(1, 512 // n_seq))

    u_rows, p_rows = _inproj_call(x_rows, mod6, prm["norm1_g"], prm["w_in_bf"], n_seq, tile)

    s5_steps = _pick(n_t, max(1, 512 // n_seq))
    y_s5, s5_re, s5_im = _s5_call(u_rows, s5_re0.reshape(n_seq, S5_LANES), s5_im0.reshape(n_seq, S5_LANES),
                                  prm["s5_consts"], n_seq, s5_steps)

    r, w, k, v, kk, kka, g = _rwkv_pre_call(p_rows, shift0, prm["rwkv_pre_consts"], n_seq, tile)
    nb = min(n_seq, LANES // RWKV_HEADS)
    y_rw, s_pairs = _rwkv_scan_call(r, w, k, kk, kka, v, _state_to_pairs(wkv0, nb), n_seq, _pick(n_t, 16))
    wkv = _state_from_pairs(s_pairs, n_seq, nb)

    x1, h2 = _mixout_call(x_rows, y_s5, y_rw, r, k, v, g, mod6, prm["mixout_consts"], n_seq, tile)

    tm = n_seq * _pick(n_t, max(1, 512 // n_seq))
    y_rows = _peer_call(h2, x1, mod6[5], prm["peer_wq_t"], prm["peer_keys1"], prm["peer_keys2"], prm["peer_u_bf"],
                        prm["peer_vt_bf"], prm["final_norm_g"], n_seq, tm)
    y = jnp.transpose(y_rows.reshape(n_t, n_seq, d), (1, 0, 2))
    shift = p_rows[rows - n_seq:]
    return (y, s5_re.reshape(n_seq, S5_GROUPS, S5_STATE), s5_im.reshape(n_seq, S5_GROUPS, S5_STATE), wkv, shift)


def kernel(x_prompt, x_sample, state_s5_re, state_s5_im, state_wkv, state_shift, c_prompt, c_sample, w_ada, b_ada, norm1_g, norm2_g, w_in, w_out, s5_a_re, s5_a_im, s5_log_dt, s5_b_re, s5_b_im, s5_c_re, s5_c_im, s5_d, w_glu, b_glu, rwkv_mu, rwkv_w0, rwkv_w2, rwkv_a0, rwkv_a2, rwkv_g2, rwkv_k_k, rwkv_k_a, rwkv_r_k, rwkv_gn_w, rwkv_gn_b, peer_w_q, peer_keys1, peer_keys2, peer_u, peer_v, final_norm_g):
    assert w_ada.shape[0] == 1, "single-layer model"
    nbp = x_prompt.shape[0]
    row = lambda a: a.reshape(1, -1)

    mod = _ada_call(jnp.concatenate([c_prompt, c_sample], axis=0).astype(F32), w_ada[0], b_ada[0])

    lb_re, lb_im, bw_re, bw_im = _s5_prep_call(s5_a_re[0], s5_a_im[0], s5_log_dt[0], s5_b_re[0], s5_b_im[0])
    head_ones = jnp.kron(jnp.eye(RWKV_HEADS, dtype=F32), jnp.ones((RWKV_HEAD, RWKV_HEAD), F32))
    lora_pad = jnp.zeros((RWKV_LORA // 2, RWKV_WIDTH), F32)
    prm = {
        "norm1_g": norm1_g[0],
        "w_in_bf": w_in[0].astype(BF16),
        "s5_consts": (lb_re, lb_im, bw_re, bw_im, _s5_out_blockdiag(s5_c_re[0]), _s5_out_blockdiag(s5_c_im[0]),
                      row(s5_d[0]), w_glu[0].astype(BF16), row(b_glu[0])),
        "rwkv_pre_consts": (row(rwkv_mu[0]), row(rwkv_w0[0]), jnp.concatenate([rwkv_w2[0], lora_pad], axis=0),
                            row(rwkv_a0[0]), jnp.concatenate([lora_pad, rwkv_a2[0]], axis=0), rwkv_g2[0],
                            row(rwkv_k_k[0]), row(rwkv_k_a[0]), head_ones),
        "mixout_consts": (row(rwkv_gn_w[0]), row(rwkv_gn_b[0]), row(rwkv_r_k[0]), head_ones,
                          w_out[0].astype(BF16), row(norm2_g[0])),
        "peer_wq_t": peer_w_q[0].T,
        "peer_keys1": peer_keys1[0],
        "peer_keys2": peer_keys2[0],
        "peer_u_bf": peer_u[0].astype(BF16),
        "peer_vt_bf": peer_v[0].T.astype(BF16),
        "final_norm_g": final_norm_g,
    }

    z_s5 = jnp.zeros((nbp, S5_GROUPS, S5_STATE), F32)
    z_wkv = jnp.zeros((nbp, RWKV_HEADS, RWKV_HEAD, RWKV_HEAD), F32)
    z_sh = jnp.zeros((nbp, RWKV_COLS), F32)
    yp, pr, pi, pw, psh = _layer(x_prompt.astype(F32), mod[:nbp], z_s5, z_s5, z_wkv, z_sh, prm)
    ys, sr, si, sw, ssh = _layer(x_sample.astype(F32), mod[nbp:], state_s5_re[0].astype(F32),
                                 state_s5_im[0].astype(F32), state_wkv[0].astype(F32),
                                 state_shift[0].astype(F32), prm)
    return (yp.astype(x_prompt.dtype), ys.astype(x_sample.dtype), pr[None], pi[None], pw[None], psh[None],
            sr[None], si[None], sw[None], ssh[None])
```

```python
import functools

import jax
import jax.numpy as jnp
from jax import lax
from jax.experimental import pallas as pl
from jax.experimental.pallas import tpu as pltpu

F32 = jnp.float32
BF16 = jnp.bfloat16
HIGHEST = lax.Precision.HIGHEST

LANES = 128
SUBLANES = 8
VMEM_LIMIT_BYTES = 56 * 1024 * 1024

D_MODEL = 1024
S5_WIDTH = 512
S5_GROUP = 16
S5_GROUPS = 32
S5_STATE = 64
S5_LANES = S5_GROUPS * S5_STATE
S5_CHUNKS = 4
S5_CHUNK_IN = S5_WIDTH // S5_CHUNKS
S5_CHUNK_ST = S5_LANES // S5_CHUNKS
RWKV_WIDTH = 512
RWKV_HEAD = 64
RWKV_HEADS = 8
RWKV_LORA = 128
RWKV_COLS = 3 * RWKV_WIDTH + 64 + 64 + 128
PEER_HEADS = 8
N_KEYS = 128
PEER_TOPK = 16
PEER_HALF = 64
PEER_TOKEN_GROUP = 256
PEER_KEYS_PER_BLOCK = 8
NORM_EPS = 1e-6
GN_EPS = 64e-5


def _params(sem):
    return pltpu.CompilerParams(dimension_semantics=sem, vmem_limit_bytes=VMEM_LIMIT_BYTES)


def _full(shape):
    return pl.BlockSpec(shape, lambda *_: (0,) * len(shape))


def _dot(a, b, precision=None):
    return jnp.dot(a, b, precision=precision, preferred_element_type=F32)


def _split_bf16(x):
    hi = x.astype(BF16)
    return hi, (x - hi.astype(F32)).astype(BF16)


def _dot_split(a, b_hi, b_lo):
    a_hi, a_lo = _split_bf16(a)
    return _dot(a_hi, b_hi) + (_dot(a_lo, b_hi) + _dot(a_hi, b_lo))


def _dot_split_lhs(a_hi, a_lo, b):
    b_hi, b_lo = _split_bf16(b)
    return _dot(a_hi, b_hi) + (_dot(a_lo, b_hi) + _dot(a_hi, b_lo))


def _gelu(x):
    return 0.5 * x * (1.0 + lax.erf(x * (2.0 ** -0.5)))


def _rms(x, g):
    return x * lax.rsqrt(jnp.mean(x * x, axis=-1, keepdims=True) + NORM_EPS) * g


def _modulate(h, shift, scale, n_seq):
    rows, d = h.shape
    h3 = h.reshape(rows // n_seq, n_seq, d)
    return (h3 * (1.0 + scale)[None] + shift[None]).reshape(rows, d)


def _gated(h, gate, n_seq):
    rows, d = h.shape
    return (h.reshape(rows // n_seq, n_seq, d) * gate[None]).reshape(rows, d)


def _split_kernel(w_ref, o_ref):
    o_ref[0], o_ref[1] = _split_bf16(w_ref[...])


def _split_call(w):
    return pl.pallas_call(
        _split_kernel,
        out_shape=jax.ShapeDtypeStruct((2,) + w.shape, BF16),
        compiler_params=pltpu.CompilerParams(vmem_limit_bytes=VMEM_LIMIT_BYTES),
        name="split_weight",
    )(w)


def _ada_kernel(c_ref, w_ref, b_ref, o_ref):
    s = jax.nn.silu(c_ref[...])
    o_ref[...] = _dot(s, w_ref[...], HIGHEST) + b_ref[...]


def _ada_call(c_all, w_ada, b_ada):
    n, d = c_all.shape
    cols = w_ada.shape[1]
    return pl.pallas_call(
        _ada_kernel,
        grid=(cols // d,),
        in_specs=[_full((n, d)), pl.BlockSpec((d, d), lambda j: (0, j)), pl.BlockSpec((1, d), lambda j: (0, j))],
        out_specs=pl.BlockSpec((n, d), lambda j: (0, j)),
        out_shape=jax.ShapeDtypeStruct((n, cols), F32),
        compiler_params=_params(("arbitrary",)),
        name="adaln_mod",
    )(c_all, w_ada, b_ada.reshape(1, cols))


def _time_major_rows(x_ref):
    if len(x_ref.shape) == 2:
        return x_ref[...]
    n_seq, steps, d = x_ref.shape
    return jnp.swapaxes(x_ref[...], 0, 1).reshape(n_seq * steps, d)


def _token_spec(x, n_seq, tile):
    if x.ndim == 2:
        return pl.BlockSpec((tile, x.shape[1]), lambda i, *_: (i, 0))
    return pl.BlockSpec((n_seq, tile // n_seq, x.shape[2]), lambda i, *_: (0, i, 0))


def _inproj_kernel(x_ref, mod_ref, g_ref, w_ref, u_ref, p_ref, *, n_seq):
    h = _modulate(_rms(_time_major_rows(x_ref), g_ref[...]), mod_ref[0], mod_ref[1], n_seq)
    proj = _dot(h.astype(BF16), w_ref[...])
    u_ref[...] = proj[:, :S5_WIDTH]
    p_ref[...] = proj[:, S5_WIDTH:]


def _inproj_call(x, mod6, norm_g, w_in_bf, n_seq, tile):
    d = x.shape[-1]
    rows = x.size // d
    return pl.pallas_call(
        functools.partial(_inproj_kernel, n_seq=n_seq),
        grid=(rows // tile,),
        in_specs=[_token_spec(x, n_seq, tile), _full(mod6.shape), _full((1, d)),
                  _full(w_in_bf.shape)],
        out_specs=[pl.BlockSpec((tile, S5_WIDTH), lambda i: (i, 0)),
                   pl.BlockSpec((tile, RWKV_COLS), lambda i: (i, 0))],
        out_shape=[jax.ShapeDtypeStruct((rows, S5_WIDTH), F32), jax.ShapeDtypeStruct((rows, RWKV_COLS), F32)],
        compiler_params=_params(("arbitrary",)),
        name="norm1_inproj",
    )(x, mod6, norm_g.reshape(1, d), w_in_bf)


def _s5_prep_kernel(are_ref, aim_ref, ldt_ref, bre_ref, bim_ref, lbr_ref, lbi_ref, wre_ref, wim_ref):
    lam_re, lam_im = are_ref[...], aim_ref[...]
    dt = jnp.exp(ldt_ref[...])
    mag = jnp.exp(lam_re * dt)
    ang = lam_im * dt
    lb_re, lb_im = mag * jnp.cos(ang), mag * jnp.sin(ang)
    den = lam_re * lam_re + lam_im * lam_im
    n_re, n_im = lb_re - 1.0, lb_im
    coef_re = (n_re * lam_re + n_im * lam_im) / den
    coef_im = (n_im * lam_re - n_re * lam_im) / den
    lbr_ref[...] = lb_re
    lbi_ref[...] = lb_im
    for c in range(S5_CHUNKS):
        cr = coef_re[:, c * S5_CHUNK_ST:(c + 1) * S5_CHUNK_ST]
        ci = coef_im[:, c * S5_CHUNK_ST:(c + 1) * S5_CHUNK_ST]
        wre_ref[0, c], wre_ref[1, c] = _split_bf16(cr * bre_ref[c] - ci * bim_ref[c])
        wim_ref[0, c], wim_ref[1, c] = _split_bf16(cr * bim_ref[c] + ci * bre_ref[c])


def _s5_prep_call(a_re, a_im, log_dt, b_re, b_im):
    row = lambda a: a.reshape(1, S5_LANES)
    ldt = jnp.repeat(log_dt, S5_STATE).reshape(1, S5_LANES)
    w_shape = (2, S5_CHUNKS, S5_CHUNK_IN, S5_CHUNK_ST)
    return pl.pallas_call(
        _s5_prep_kernel,
        out_shape=[jax.ShapeDtypeStruct((1, S5_LANES), F32)] * 2 + [jax.ShapeDtypeStruct(w_shape, BF16)] * 2,
        compiler_params=pltpu.CompilerParams(vmem_limit_bytes=VMEM_LIMIT_BYTES),
        name="s5_discretise",
    )(row(a_re), row(a_im), ldt, _s5_in_blockdiag(b_re), _s5_in_blockdiag(b_im))


def _s5_in_blockdiag(b):
    gpc = S5_GROUPS // S5_CHUNKS
    bt = jnp.transpose(b, (0, 2, 1)).reshape(S5_CHUNKS, gpc, S5_GROUP, S5_STATE)
    eye = jnp.eye(gpc, dtype=b.dtype)
    bd = bt[:, :, :, None, :] * eye[None, :, None, :, None]
    return bd.reshape(S5_CHUNKS, S5_CHUNK_IN, S5_CHUNK_ST)


def _s5_out_blockdiag(c):
    gpc = S5_GROUPS // S5_CHUNKS
    ct = jnp.transpose(c, (0, 2, 1)).reshape(S5_CHUNKS, gpc, S5_STATE, S5_GROUP)
    eye = jnp.eye(gpc, dtype=c.dtype)
    bd = ct[:, :, :, None, :] * eye[None, :, None, :, None]
    return bd.reshape(S5_CHUNKS, S5_CHUNK_ST, S5_CHUNK_IN)


def _s5_kernel(u_ref, h0r_ref, h0i_ref, lbr_ref, lbi_ref, wre_ref, wim_ref, cre_ref, cim_ref, d_ref, wglu_ref,
               bglu_ref, y_ref, hr_out, hi_out, re_s, im_s, str_s, sti_s, *, n_seq, steps):
    chunk = pl.program_id(0)

    @pl.when(chunk == 0)
    def _():
        str_s[...] = h0r_ref[...]
        sti_s[...] = h0i_ref[...]

    u = u_ref[...]
    for c in range(S5_CHUNKS):
        uc = u[:, c * S5_CHUNK_IN:(c + 1) * S5_CHUNK_IN]
        re_s[:, c * S5_CHUNK_ST:(c + 1) * S5_CHUNK_ST] = _dot_split(uc, wre_ref[0, c], wre_ref[1, c])
        im_s[:, c * S5_CHUNK_ST:(c + 1) * S5_CHUNK_ST] = _dot_split(uc, wim_ref[0, c], wim_ref[1, c])

    def seq_block(rb, carry):
        r0 = pl.multiple_of(rb * SUBLANES, SUBLANES)
        for c in range(S5_CHUNKS):
            lanes = slice(c * S5_CHUNK_ST, (c + 1) * S5_CHUNK_ST)
            lr = jnp.broadcast_to(lbr_ref[:, lanes], (SUBLANES, S5_CHUNK_ST))
            li = jnp.broadcast_to(lbi_ref[:, lanes], (SUBLANES, S5_CHUNK_ST))

            def step(t, h):
                hr, hi = h
                row = pl.multiple_of(t * n_seq + r0, SUBLANES)
                nr = lr * hr - li * hi + re_s[pl.ds(row, SUBLANES), lanes]
                ni = lr * hi + li * hr + im_s[pl.ds(row, SUBLANES), lanes]
                re_s[pl.ds(row, SUBLANES), lanes] = nr
                im_s[pl.ds(row, SUBLANES), lanes] = ni
                return nr, ni

            h0 = (str_s[pl.ds(r0, SUBLANES), lanes], sti_s[pl.ds(r0, SUBLANES), lanes])
            hr, hi = lax.fori_loop(0, steps, step, h0, unroll=min(steps, 8))
            str_s[pl.ds(r0, SUBLANES), lanes] = hr
            sti_s[pl.ds(r0, SUBLANES), lanes] = hi
        return carry

    lax.fori_loop(0, n_seq // SUBLANES, seq_block, 0)

    ys = []
    for c in range(S5_CHUNKS):
        lanes = slice(c * S5_CHUNK_ST, (c + 1) * S5_CHUNK_ST)
        ys.append(_dot(re_s[:, lanes].astype(BF16), cre_ref[c]) - _dot(im_s[:, lanes].astype(BF16), cim_ref[c]))
    y = jnp.concatenate(ys, axis=-1) + d_ref[...] * u
    y = _gelu(y)
    y_ref[...] = y * jax.nn.sigmoid(_dot(y.astype(BF16), wglu_ref[...]) + bglu_ref[...])

    @pl.when(chunk == pl.num_programs(0) - 1)
    def _():
        hr_out[...] = str_s[...]
        hi_out[...] = sti_s[...]


def _s5_call(u_rows, h0_re, h0_im, consts, n_seq, steps):
    rows = u_rows.shape[0]
    tile = n_seq * steps
    lb_re, lb_im, w_re, w_im, c_re, c_im, d_skip, w_glu_bf, b_glu = consts
    state = jax.ShapeDtypeStruct((n_seq, S5_LANES), F32)
    args = (u_rows, h0_re, h0_im, lb_re, lb_im, w_re, w_im, c_re, c_im, d_skip, w_glu_bf, b_glu)
    in_specs = [pl.BlockSpec((tile, S5_WIDTH), lambda i: (i, 0))] + [_full(a.shape) for a in args[1:]]
    return pl.pallas_call(
        functools.partial(_s5_kernel, n_seq=n_seq, steps=steps),
        grid=(rows // tile,),
        in_specs=in_specs,
        out_specs=[pl.BlockSpec((tile, S5_WIDTH), lambda i: (i, 0)), _full(state.shape), _full(state.shape)],
        out_shape=[jax.ShapeDtypeStruct((rows, S5_WIDTH), F32), state, state],
        scratch_shapes=[pltpu.VMEM((tile, S5_LANES), F32), pltpu.VMEM((tile, S5_LANES), F32),
                        pltpu.VMEM((n_seq, S5_LANES), F32), pltpu.VMEM((n_seq, S5_LANES), F32)],
        compiler_params=_params(("arbitrary",)),
        name="s5_mixer",
    )(*args)


def _head_sum(x, ones_ref):
    hi, lo = _split_bf16(x)
    return _dot(hi, ones_ref[...]) + _dot(lo, ones_ref[...])


def _rwkv_pre_kernel(p_ref, prev_ref, shift_ref, mu_ref, w0_ref, w2_ref, a0_ref, a2_ref, g2_ref, kk_ref, ka_ref,
                     ones_ref, r_out, w_out, k_out, v_out, kk_out, kka_out, g_out, *, n_seq):
    p = p_ref[...]
    tile = p.shape[0]
    head = jnp.where(pl.program_id(0) == 0, shift_ref[...], prev_ref[...])
    p_prev = head if tile == n_seq else jnp.concatenate([head, p[:tile - n_seq]], axis=0)
    ps = p + (p_prev - p) * mu_ref[...]
    w = RWKV_WIDTH
    r, k, v = ps[:, :w], ps[:, w:2 * w], ps[:, 2 * w:3 * w]
    lo = ps[:, 3 * w:3 * w + RWKV_LORA]
    g_lo = ps[:, 3 * w + RWKV_LORA:]
    w_raw = -jax.nn.softplus(-(w0_ref[...] + _dot_split(jnp.tanh(lo), w2_ref[0], w2_ref[1]))) - 0.5
    a = jax.nn.sigmoid(a0_ref[...] + _dot_split(lo, a2_ref[0], a2_ref[1]))
    kk = k * kk_ref[...]
    norm = jnp.sqrt(_head_sum(kk * kk, ones_ref))
    kk = kk / jnp.maximum(norm, 1e-12)
    r_out[...] = r
    w_out[...] = jnp.exp(-jnp.exp(w_raw))
    k_out[...] = k * (1.0 + (a - 1.0) * ka_ref[...])
    v_out[...] = v
    kk_out[...] = kk
    kka_out[...] = kk * a
    g_out[...] = _dot_split(jax.nn.sigmoid(g_lo), g2_ref[0], g2_ref[1])


def _rwkv_pre_call(p_rows, shift0, consts, n_seq, tile):
    rows = p_rows.shape[0]
    per = tile // n_seq
    vec = jax.ShapeDtypeStruct((rows, RWKV_WIDTH), F32)
    in_specs = [pl.BlockSpec((tile, RWKV_COLS), lambda i: (i, 0)),
                pl.BlockSpec((n_seq, RWKV_COLS), lambda i: (jnp.maximum(i * per - 1, 0), 0)),
                _full(shift0.shape)] + [_full(a.shape) for a in consts]
    return pl.pallas_call(
        functools.partial(_rwkv_pre_kernel, n_seq=n_seq),
        grid=(rows // tile,),
        in_specs=in_specs,
        out_specs=[pl.BlockSpec((tile, RWKV_WIDTH), lambda i: (i, 0))] * 7,
        out_shape=[vec] * 7,
        compiler_params=_params(("arbitrary",)),
        name="rwkv_prologue",
    )(p_rows, p_rows, shift0, *consts)


def _sublane_allsum(p):
    p = p + pltpu.roll(p, 4, 0)
    p = p + pltpu.roll(p, 2, 0)
    return p + pltpu.roll(p, 1, 0)


def _pairs_from_rows(x, nb):
    cols = [x[:, c * LANES:(c + 1) * LANES] for c in range(RWKV_WIDTH // LANES)]
    m = jnp.concatenate(cols * (LANES // (4 * nb)), axis=0)
    mt = m.T
    lane = lax.broadcasted_iota(jnp.int32, (RWKV_HEAD, LANES), 1)
    even_head = (lane // nb) % RWKV_HEADS < RWKV_HEADS // 2
    return jnp.where(even_head, mt[:RWKV_HEAD], pltpu.roll(mt[RWKV_HEAD:], 4 * nb, 1))


def _rows_from_pairs(y, nb):
    pairs = RWKV_HEADS * nb
    shifts = [(-(copy * pairs + parity * pairs // 2)) % LANES
              for parity in range(2) for copy in range(LANES // pairs)]
    mt = jnp.concatenate([y if s == 0 else pltpu.roll(y, s, 1) for s in shifts], axis=0)
    m = mt.T
    return jnp.concatenate([m[c * nb:(c + 1) * nb, :] for c in range(RWKV_WIDTH // LANES)], axis=1)


def _rwkv_scan_kernel(r_ref, w_ref, k_ref, kk_ref, kka_ref, v_ref, s0_ref, y_ref, sout_ref, s_s, ka_s, kb_s, va_s,
                      vb_s, *, steps, n_v, nb):
    tc = pl.program_id(1)

    @pl.when(tc == 0)
    def _():
        s_s[...] = s0_ref[...]

    row_id = lax.broadcasted_iota(jnp.int32, (SUBLANES, LANES), 0)
    lane_id = lax.broadcasted_iota(jnp.int32, (n_v, LANES), 1)
    split = lambda x: x.reshape(RWKV_HEAD // SUBLANES, SUBLANES, LANES)
    k_refs = (r_ref, w_ref, k_ref, kk_ref, kka_ref)

    def stage(t, k_s, v_s):
        for i, ref in enumerate(k_refs):
            k_s[i] = _pairs_from_rows(ref[t], nb)
        full = _pairs_from_rows(v_ref[t], nb)
        v_s[...] = full if n_v == RWKV_HEAD else jnp.where(lane_id < LANES // 2, full[:n_v], full[n_v:])

    def advance(t, k_s, v_s):
        r, w, k, kk, kka = (split(k_s[i]) for i in range(len(k_refs)))
        y_tiles = []
        for vb in range(n_v // SUBLANES):
            v_tile = v_s[vb * SUBLANES:(vb + 1) * SUBLANES, :]
            y_tile = jnp.zeros((SUBLANES, LANES), F32)
            for j in range(SUBLANES):
                s = split(s_s[vb * SUBLANES + j])
                sa = -_sublane_allsum(jnp.sum(s * kk, axis=0))
                v_row = jnp.broadcast_to(v_tile[j:j + 1, :], (SUBLANES, LANES))
                s = s * w + sa[None] * kka + v_row[None] * k
                s_s[vb * SUBLANES + j] = s.reshape(RWKV_HEAD, LANES)
                y_tile = jnp.where(row_id == j, _sublane_allsum(jnp.sum(s * r, axis=0)), y_tile)
            y_tiles.append(y_tile)
        y_ref[t] = _rows_from_pairs(jnp.concatenate(y_tiles, axis=0), nb)

    stage(0, ka_s, va_s)

    def two_steps(i, carry):
        t = 2 * i
        stage(t + 1, kb_s, vb_s)
        advance(t, ka_s, va_s)
        stage(jnp.minimum(t + 2, steps - 1), ka_s, va_s)
        advance(t + 1, kb_s, vb_s)
        return carry

    lax.fori_loop(0, steps // 2, two_steps, 0)

    @pl.when(tc == pl.num_programs(1) - 1)
    def _():
        sout_ref[...] = s_s[...]


def _rwkv_scan_call(r, w, k, kk, kka, v, s0, n_seq, steps):
    rows = r.shape[0]
    n_t = rows // n_seq
    n_v, _, lanes = s0.shape
    nb = n_seq // (lanes // LANES)
    assert steps % 2 == 0 and n_t % steps == 0
    as_steps = lambda a: a.reshape(n_t, n_seq, RWKV_WIDTH)
    xspec = pl.BlockSpec((steps, nb, RWKV_WIDTH), lambda g, i: (i, g, 0))
    sspec = pl.BlockSpec((n_v, RWKV_HEAD, LANES), lambda g, i: (0, 0, g))
    stage_k = pltpu.VMEM((5, RWKV_HEAD, LANES), F32)
    stage_v = pltpu.VMEM((n_v, LANES), F32)
    y, s = pl.pallas_call(
        functools.partial(_rwkv_scan_kernel, steps=steps, n_v=n_v, nb=nb),
        grid=(lanes // LANES, n_t // steps),
        in_specs=[xspec] * 6 + [sspec],
        out_specs=[xspec, sspec],
        out_shape=[jax.ShapeDtypeStruct((n_t, n_seq, RWKV_WIDTH), F32), jax.ShapeDtypeStruct(s0.shape, F32)],
        scratch_shapes=[pltpu.VMEM((n_v, RWKV_HEAD, LANES), F32), stage_k, stage_k, stage_v, stage_v],
        compiler_params=_params(("arbitrary", "arbitrary")),
        name="rwkv_recurrence",
    )(*[as_steps(a) for a in (r, w, k, kk, kka, v)], s0)
    return y.reshape(rows, RWKV_WIDTH), s


def _state_to_pairs(s, nb):
    n_seq = s.shape[0]
    v_split = LANES // (RWKV_HEADS * nb)
    n_v = RWKV_HEAD // v_split
    s = s.reshape(n_seq // nb, nb, RWKV_HEADS // 2, 2, v_split, n_v, RWKV_HEAD)
    return jnp.transpose(s, (5, 6, 0, 4, 3, 2, 1)).reshape(n_v, RWKV_HEAD, n_seq // nb * LANES)


def _state_from_pairs(s, n_seq, nb):
    v_split = LANES // (RWKV_HEADS * nb)
    n_v = RWKV_HEAD // v_split
    s = s.reshape(n_v, RWKV_HEAD, n_seq // nb, v_split, 2, RWKV_HEADS // 2, nb)
    return jnp.transpose(s, (2, 6, 5, 4, 3, 0, 1)).reshape(n_seq, RWKV_HEADS, RWKV_HEAD, RWKV_HEAD)


def _mixout_kernel(x_ref, ys5_ref, yrw_ref, r_ref, k_ref, v_ref, g_ref, mod_ref, gnw_ref, gnb_ref, rk_ref,
                   ones_ref, wout_ref, n2g_ref, x1_ref, h2_ref, *, n_seq):
    y = yrw_ref[...]
    inv_n = 1.0 / RWKV_HEAD
    mean = _head_sum(y, ones_ref) * inv_n
    yc = y - mean
    var = _head_sum(yc * yc, ones_ref) * inv_n
    y = yc * lax.rsqrt(var + GN_EPS) * gnw_ref[...] + gnb_ref[...]
    v = v_ref[...]
    y = y + _head_sum(r_ref[...] * k_ref[...] * rk_ref[...], ones_ref) * v
    y = y * g_ref[...]
    mix = jnp.concatenate([ys5_ref[...], y], axis=-1).astype(BF16)
    x1 = _time_major_rows(x_ref) + _gated(_dot(mix, wout_ref[...]), mod_ref[2], n_seq)
    x1_ref[...] = x1
    h2_ref[...] = _modulate(_rms(x1, n2g_ref[...]), mod_ref[3], mod_ref[4], n_seq)


def _mixout_call(x, y_s5, y_rw, r, k, v, g, mod6, consts, n_seq, tile):
    d = x.shape[-1]
    rows = x.size // d
    wide = pl.BlockSpec((tile, d), lambda i: (i, 0))
    half = pl.BlockSpec((tile, RWKV_WIDTH), lambda i: (i, 0))
    return pl.pallas_call(
        functools.partial(_mixout_kernel, n_seq=n_seq),
        grid=(rows // tile,),
        in_specs=[_token_spec(x, n_seq, tile)] + [half] * 6 + [_full(mod6.shape)] + [_full(a.shape) for a in consts],
        out_specs=[wide, wide],
        out_shape=[jax.ShapeDtypeStruct((rows, d), F32)] * 2,
        compiler_params=_params(("arbitrary",)),
        name="mixer_out_norm2",
    )(x, y_s5, y_rw, r, k, v, g, mod6, *consts)


def _sort16_pairs():
    pairs = []
    k = 2
    while k <= PEER_TOPK:
        j = k // 2
        while j >= 1:
            pairs += [(i, i ^ j, (i & k) == 0) for i in range(PEER_TOPK) if i ^ j > i]
            j //= 2
        k *= 2
    return pairs


def _bitonic_merge_desc(x):
    x = list(x)
    j = PEER_TOPK // 2
    while j >= 1:
        for i in range(PEER_TOPK):
            if not i & j:
                x[i], x[i | j] = jnp.maximum(x[i], x[i | j]), jnp.minimum(x[i], x[i | j])
        j //= 2
    return x


def _top16_desc(slabs):
    x = list(slabs)
    for i, l, i_max in _sort16_pairs():
        hi, lo = jnp.maximum(x[i], x[l]), jnp.minimum(x[i], x[l])
        x[i], x[l] = (hi, lo) if i_max else (lo, hi)
    for shift in (1, 2, 4):
        x = _bitonic_merge_desc([jnp.maximum(x[v], pltpu.roll(x[PEER_TOPK - 1 - v], shift, 0))
                                 for v in range(PEER_TOPK)])
    return x


def _pair_candidates(a, b):
    sub = lax.broadcasted_iota(jnp.int32, a[0].shape, 0)
    b_lo, b_hi = b[0], b[SUBLANES]
    for s in range(1, SUBLANES):
        b_lo = jnp.where(sub == s, b[s], b_lo)
    for s in range(2, SUBLANES):
        b_hi = jnp.where(sub == s, b[SUBLANES - 1 + s], b_hi)
    cand = [a[i] + b_lo for i in range(PEER_TOPK)]
    cand[15] = jnp.where(sub == 0, cand[15], a[0] + b_hi)
    cand[14] = jnp.where(sub == 1, a[0] + b[PEER_TOPK - 1], cand[14])
    return cand


def _peer_select(xt, wq_ref, k1_ref, k2_ref, q_s, a_s, b_s, e1_s, e2_s, thr_s):
    tg = xt.shape[1]
    q_s[...] = _dot_split_lhs(wq_ref[0], wq_ref[1], xt).reshape(PEER_HEADS, 2 * PEER_HALF, tg)
    slabs = lambda s: [s[v * SUBLANES:(v + 1) * SUBLANES, :] for v in range(N_KEYS // SUBLANES)]
    tiled = lambda row: jnp.concatenate([row] * (N_KEYS // SUBLANES), axis=0)

    def head(h, carry):
        q = q_s[h]
        s1 = _dot(k1_ref[h], q[:PEER_HALF], HIGHEST)
        s2 = _dot(k2_ref[h], q[PEER_HALF:], HIGHEST)
        top1 = _top16_desc(slabs(s1))
        top2 = _top16_desc(slabs(s2))
        topc = _top16_desc(_pair_candidates(top1, top2))
        z = jnp.ones_like(topc[0])
        for i in range(1, PEER_TOPK):
            z = z + jnp.exp(topc[i] - topc[0])
        a = jnp.where(s1 >= tiled(top1[-1]), s1, -jnp.inf)
        b = jnp.where(s2 >= tiled(top2[-1]), s2, -jnp.inf)
        a_s[h] = a
        b_s[h] = b
        e1_s[h] = jnp.exp(a - tiled(top1[0])) / tiled(z)
        e2_s[h] = jnp.exp(b - tiled(top2[0]))
        thr_s[h] = topc[-1]
        return carry

    lax.fori_loop(0, PEER_HEADS, head, 0)


def _peer_gates(g, key0, act_ref, pt_ref, a_s, b_s, e1_s, e2_s, thr_s):
    tg = act_ref.shape[2]
    for lg in range(tg // LANES):
        ls = slice(lg * LANES, (lg + 1) * LANES)
        a_rows = [a_s[g, h, pl.ds(key0, PEER_KEYS_PER_BLOCK), ls] for h in range(PEER_HEADS)]
        e_rows = [e1_s[g, h, pl.ds(key0, PEER_KEYS_PER_BLOCK), ls] for h in range(PEER_HEADS)]
        for i in range(PEER_KEYS_PER_BLOCK):
            gate = jnp.zeros((N_KEYS, LANES), F32)
            for h in range(PEER_HEADS):
                score = b_s[g, h, :, ls] + a_rows[h][i:i + 1, :]
                weight = e2_s[g, h, :, ls] * e_rows[h][i:i + 1, :]
                gate = gate + jnp.where(score >= thr_s[g, h, 0:1, ls], weight, 0.0)
            act = _gelu(act_ref[g, i * N_KEYS:(i + 1) * N_KEYS, ls])
            pt_ref[g, i * N_KEYS:(i + 1) * N_KEYS, ls] = (gate * act).astype(BF16)


def _peer_kernel(h2_ref, x1_ref, ga_ref, wq_ref, k1_ref, k2_ref, u_ref, vt_ref, fg_ref, o_ref,
                 xt_s, q_s, a_s, b_s, e1_s, e2_s, thr_s, act0_s, act1_s, pt0_s, pt1_s, acc_s, *, n_seq, n_blocks):
    s = pl.program_id(1)
    n_groups = xt_s.shape[0]
    tg = xt_s.shape[2]
    acts, pts = (act0_s, act1_s), (pt0_s, pt1_s)
    sel = (a_s, b_s, e1_s, e2_s, thr_s)

    def stages(parity, first, gates, second):
        key0 = pl.multiple_of((s - 1) * PEER_KEYS_PER_BLOCK, PEER_KEYS_PER_BLOCK)

        def group(g, carry):
            if first:
                acts[parity][g] = _dot(u_ref[...], xt_s[g])
            if gates:
                _peer_gates(g, key0, acts[1 - parity], pts[1 - parity], *sel)
            if second:
                acc_s[g] += _dot(vt_ref[...], pts[parity][g])
            return carry

        lax.fori_loop(0, n_groups, group, 0)

    @pl.when(s == 0)
    def _():
        def prep(g, carry):
            t0 = pl.multiple_of(g * tg, tg)
            xt = h2_ref[pl.ds(t0, tg), :].T
            xt_s[g] = xt.astype(BF16)
            _peer_select(xt, wq_ref, k1_ref, k2_ref, q_s, a_s.at[g], b_s.at[g], e1_s.at[g], e2_s.at[g],
                         thr_s.at[g])
            return carry

        lax.fori_loop(0, n_groups, prep, 0)
        acc_s[...] = jnp.zeros_like(acc_s)
        pt1_s[...] = jnp.zeros_like(pt1_s)
        stages(0, True, False, False)

    for parity in (0, 1):
        @pl.when((s >= 1) & (s < n_blocks) & (s % 2 == parity))
        def _(parity=parity):
            stages(parity, True, True, True)

    @pl.when(s == n_blocks)
    def _():
        stages(n_blocks % 2, False, True, True)

    @pl.when(s == n_blocks + 1)
    def _():
        stages((n_blocks + 1) % 2, False, False, True)
        for g in range(n_groups):
            rows = slice(g * tg, (g + 1) * tg)
            x2 = x1_ref[rows, :] + _gated(acc_s[g].T, ga_ref[...], min(n_seq, tg))
            y = _rms(x2, fg_ref[...])
            if len(o_ref.shape) == 2:
                o_ref[rows, :] = y
            else:
                sg = tg // n_seq
                o_ref[:, g * sg:(g + 1) * sg, :] = jnp.swapaxes(y.reshape(sg, n_seq, y.shape[1]), 0, 1)


def _peer_call(h2, x1, gate2, wq_t, keys1, keys2, u_bf, vt_bf, final_g, n_seq, tm, batch_major):
    rows, d = h2.shape
    n_exp = u_bf.shape[0]
    eb = PEER_KEYS_PER_BLOCK * N_KEYS
    n_blocks = n_exp // eb
    tg = PEER_TOKEN_GROUP
    n_groups = tm // tg
    assert tm % tg == 0 and (tg % n_seq == 0 or n_seq % tg == 0) and n_exp % eb == 0 and n_blocks >= 2
    if n_seq > tg:
        raise NotImplementedError("more than PEER_TOKEN_GROUP sequences per step")
    tok = pl.BlockSpec((tm, d), lambda i, s: (i, 0))
    out = jax.ShapeDtypeStruct((n_seq, rows // n_seq, d) if batch_major else (rows, d), F32)
    sel = pltpu.VMEM((n_groups, PEER_HEADS, N_KEYS, tg), F32)
    act = pltpu.VMEM((n_groups, eb, tg), F32)
    pt = pltpu.VMEM((n_groups, eb, tg), BF16)
    return pl.pallas_call(
        functools.partial(_peer_kernel, n_seq=n_seq, n_blocks=n_blocks),
        grid=(rows // tm, n_blocks + 2),
        in_specs=[tok, tok, _full(gate2.shape), _full(wq_t.shape), _full(keys1.shape), _full(keys2.shape),
                  pl.BlockSpec((eb, d), lambda i, s: (jnp.minimum(s, n_blocks - 1), 0)),
                  pl.BlockSpec((d, eb), lambda i, s: (0, jnp.clip(s - 2, 0, n_blocks - 1))),
                  _full((1, d))],
        out_specs=_token_spec(out, n_seq, tm),
        out_shape=out,
        scratch_shapes=[pltpu.VMEM((n_groups, d, tg), BF16), pltpu.VMEM((PEER_HEADS, 2 * PEER_HALF, tg), F32),
                        sel, sel, sel, sel, pltpu.VMEM((n_groups, PEER_HEADS, SUBLANES, tg), F32),
                        act, act, pt, pt, pltpu.VMEM((n_groups, d, tg), F32)],
        compiler_params=_params(("arbitrary", "arbitrary")),
        name="peer_final_norm",
    )(h2, x1, gate2, wq_t, keys1, keys2, u_bf, vt_bf, final_g.reshape(1, d))


def _pick(n, target):
    t = min(n, target)
    while n % t:
        t -= 1
    return t


def _layer(x_btd, mod, s5_re0, s5_im0, wkv0, shift0, prm):
    n_seq, n_t, d = x_btd.shape
    rows = n_seq * n_t
    mod6 = jnp.transpose(mod.reshape(n_seq, 6, d), (1, 0, 2))
    tile = n_seq * _pick(n_t, max(1, 512 // n_seq))
    reorder_in_kernel = (PEER_TOKEN_GROUP // n_seq) % SUBLANES == 0 and (tile // n_seq) % SUBLANES == 0
    x_in = x_btd if reorder_in_kernel else jnp.transpose(x_btd, (1, 0, 2)).reshape(rows, d)

    u_rows, p_rows = _inproj_call(x_in, mod6, prm["norm1_g"], prm["w_in_bf"], n_seq, tile)

    s5_steps = _pick(n_t, max(1, 512 // n_seq))
    y_s5, s5_re, s5_im = _s5_call(u_rows, s5_re0.reshape(n_seq, S5_LANES), s5_im0.reshape(n_seq, S5_LANES),
                                  prm["s5_consts"], n_seq, s5_steps)

    r, w, k, v, kk, kka, g = _rwkv_pre_call(p_rows, shift0, prm["rwkv_pre_consts"], n_seq, tile)
    nb = min(n_seq, LANES // RWKV_HEADS)
    y_rw, s_pairs = _rwkv_scan_call(r, w, k, kk, kka, v, _state_to_pairs(wkv0, nb), n_seq, _pick(n_t, 16))
    wkv = _state_from_pairs(s_pairs, n_seq, nb)

    x1, h2 = _mixout_call(x_in, y_s5, y_rw, r, k, v, g, mod6, prm["mixout_consts"], n_seq, tile)

    y = _peer_call(h2, x1, mod6[5], prm["peer_wq_t"], prm["peer_keys1"], prm["peer_keys2"], prm["peer_u_bf"],
                   prm["peer_vt_bf"], prm["final_norm_g"], n_seq, tile, reorder_in_kernel)
    if not reorder_in_kernel:
        y = jnp.transpose(y.reshape(n_t, n_seq, d), (1, 0, 2))
    shift = p_rows[rows - n_seq:]
    return (y, s5_re.reshape(n_seq, S5_GROUPS, S5_STATE), s5_im.reshape(n_seq, S5_GROUPS, S5_STATE), wkv, shift)


def kernel(x_prompt, x_sample, state_s5_re, state_s5_im, state_wkv, state_shift, c_prompt, c_sample, w_ada, b_ada, norm1_g, norm2_g, w_in, w_out, s5_a_re, s5_a_im, s5_log_dt, s5_b_re, s5_b_im, s5_c_re, s5_c_im, s5_d, w_glu, b_glu, rwkv_mu, rwkv_w0, rwkv_w2, rwkv_a0, rwkv_a2, rwkv_g2, rwkv_k_k, rwkv_k_a, rwkv_r_k, rwkv_gn_w, rwkv_gn_b, peer_w_q, peer_keys1, peer_keys2, peer_u, peer_v, final_norm_g):
    assert w_ada.shape[0] == 1, "single-layer model"
    nbp = x_prompt.shape[0]
    row = lambda a: a.reshape(1, -1)

    mod = _ada_call(jnp.concatenate([c_prompt, c_sample], axis=0).astype(F32), w_ada[0], b_ada[0])

    lb_re, lb_im, bw_re, bw_im = _s5_prep_call(s5_a_re[0], s5_a_im[0], s5_log_dt[0], s5_b_re[0], s5_b_im[0])
    head_ones = jnp.kron(jnp.eye(RWKV_HEADS, dtype=BF16), jnp.ones((RWKV_HEAD, RWKV_HEAD), BF16))
    lora_pad = jnp.zeros((RWKV_LORA // 2, RWKV_WIDTH), F32)
    prm = {
        "norm1_g": norm1_g[0],
        "w_in_bf": w_in[0].astype(BF16),
        "s5_consts": (lb_re, lb_im, bw_re, bw_im, _s5_out_blockdiag(s5_c_re[0]).astype(BF16),
                      _s5_out_blockdiag(s5_c_im[0]).astype(BF16), row(s5_d[0]), w_glu[0].astype(BF16),
                      row(b_glu[0])),
        "rwkv_pre_consts": (row(rwkv_mu[0]), row(rwkv_w0[0]),
                            _split_call(jnp.concatenate([rwkv_w2[0], lora_pad], axis=0)), row(rwkv_a0[0]),
                            _split_call(jnp.concatenate([lora_pad, rwkv_a2[0]], axis=0)), _split_call(rwkv_g2[0]),
                            row(rwkv_k_k[0]), row(rwkv_k_a[0]), head_ones),
        "mixout_consts": (row(rwkv_gn_w[0]), row(rwkv_gn_b[0]), row(rwkv_r_k[0]), head_ones,
                          w_out[0].astype(BF16), row(norm2_g[0])),
        "peer_wq_t": _split_call(peer_w_q[0].T),
        "peer_keys1": peer_keys1[0],
        "peer_keys2": peer_keys2[0],
        "peer_u_bf": peer_u[0].astype(BF16),
        "peer_vt_bf": peer_v[0].T.astype(BF16),
        "final_norm_g": final_norm_g,
    }

    z_s5 = jnp.zeros((nbp, S5_GROUPS, S5_STATE), F32)
    z_wkv = jnp.zeros((nbp, RWKV_HEADS, RWKV_HEAD, RWKV_HEAD), F32)
    z_sh = jnp.zeros((nbp, RWKV_COLS), F32)
    yp, pr, pi, pw, psh = _layer(x_prompt.astype(F32), mod[:nbp], z_s5, z_s5, z_wkv, z_sh, prm)
    ys, sr, si, sw, ssh = _layer(x_sample.astype(F32), mod[nbp:], state_s5_re[0].astype(F32),
                                 state_s5_im[0].astype(F32), state_wkv[0].astype(F32),
                                 state_shift[0].astype(F32), prm)
    return (yp.astype(x_prompt.dtype), ys.astype(x_sample.dtype), pr[None], pi[None], pw[None], psh[None],
            sr[None], si[None], sw[None], ssh[None])
```

```python
import functools

import jax
import jax.numpy as jnp
from jax import lax
from jax.experimental import pallas as pl
from jax.experimental.pallas import tpu as pltpu

F32 = jnp.float32
BF16 = jnp.bfloat16
HIGHEST = lax.Precision.HIGHEST

LANES = 128
SUBLANES = 8
VMEM_LIMIT_BYTES = 56 * 1024 * 1024

D_MODEL = 1024
S5_WIDTH = 512
S5_GROUP = 16
S5_GROUPS = 32
S5_STATE = 64
S5_LANES = S5_GROUPS * S5_STATE
S5_CHUNKS = 4
S5_CHUNK_IN = S5_WIDTH // S5_CHUNKS
S5_CHUNK_ST = S5_LANES // S5_CHUNKS
RWKV_WIDTH = 512
RWKV_HEAD = 64
RWKV_HEADS = 8
RWKV_LORA = 128
RWKV_COLS = 3 * RWKV_WIDTH + 64 + 64 + 128
PEER_HEADS = 8
N_KEYS = 128
PEER_TOPK = 16
PEER_HALF = 64
PEER_TOKEN_GROUP = 256
PEER_KEYS_PER_BLOCK = 8
NORM_EPS = 1e-6
GN_EPS = 64e-5


def _params(sem):
    return pltpu.CompilerParams(dimension_semantics=sem, vmem_limit_bytes=VMEM_LIMIT_BYTES)


def _full(shape):
    return pl.BlockSpec(shape, lambda *_: (0,) * len(shape))


def _dot(a, b, precision=None):
    return jnp.dot(a, b, precision=precision, preferred_element_type=F32)


def _split_bf16(x):
    hi = x.astype(BF16)
    return hi, (x - hi.astype(F32)).astype(BF16)


def _dot_split(a, b_hi, b_lo):
    a_hi, a_lo = _split_bf16(a)
    return _dot(a_hi, b_hi) + (_dot(a_lo, b_hi) + _dot(a_hi, b_lo))


def _dot_split_lhs(a_hi, a_lo, b):
    b_hi, b_lo = _split_bf16(b)
    return _dot(a_hi, b_hi) + (_dot(a_lo, b_hi) + _dot(a_hi, b_lo))


def _gelu(x):
    return 0.5 * x * (1.0 + lax.erf(x * (2.0 ** -0.5)))


def _rms(x, g):
    return x * lax.rsqrt(jnp.mean(x * x, axis=-1, keepdims=True) + NORM_EPS) * g


def _modulate(h, shift, scale, n_seq):
    rows, d = h.shape
    h3 = h.reshape(rows // n_seq, n_seq, d)
    return (h3 * (1.0 + scale)[None] + shift[None]).reshape(rows, d)


def _gated(h, gate, n_seq):
    rows, d = h.shape
    return (h.reshape(rows // n_seq, n_seq, d) * gate[None]).reshape(rows, d)


def _split_kernel(w_ref, o_ref):
    o_ref[0], o_ref[1] = _split_bf16(w_ref[...])


def _split_call(w):
    return pl.pallas_call(
        _split_kernel,
        out_shape=jax.ShapeDtypeStruct((2,) + w.shape, BF16),
        compiler_params=pltpu.CompilerParams(vmem_limit_bytes=VMEM_LIMIT_BYTES),
        name="split_weight",
    )(w)


def _ada_kernel(c_ref, w_ref, b_ref, o_ref):
    s = jax.nn.silu(c_ref[...])
    o_ref[...] = _dot(s, w_ref[...], HIGHEST) + b_ref[...]


def _ada_call(c_all, w_ada, b_ada):
    n, d = c_all.shape
    cols = w_ada.shape[1]
    return pl.pallas_call(
        _ada_kernel,
        grid=(cols // d,),
        in_specs=[_full((n, d)), pl.BlockSpec((d, d), lambda j: (0, j)), pl.BlockSpec((1, d), lambda j: (0, j))],
        out_specs=pl.BlockSpec((n, d), lambda j: (0, j)),
        out_shape=jax.ShapeDtypeStruct((n, cols), F32),
        compiler_params=_params(("arbitrary",)),
        name="adaln_mod",
    )(c_all, w_ada, b_ada.reshape(1, cols))


def _time_major_rows(x_ref):
    if len(x_ref.shape) == 2:
        return x_ref[...]
    n_seq, steps, d = x_ref.shape
    return jnp.swapaxes(x_ref[...], 0, 1).reshape(n_seq * steps, d)


def _token_spec(x, n_seq, tile):
    if x.ndim == 2:
        return pl.BlockSpec((tile, x.shape[1]), lambda i, *_: (i, 0))
    return pl.BlockSpec((n_seq, tile // n_seq, x.shape[2]), lambda i, *_: (0, i, 0))


def _inproj_kernel(x_ref, mod_ref, g_ref, w_ref, u_ref, p_ref, *, n_seq):
    h = _modulate(_rms(_time_major_rows(x_ref), g_ref[...]), mod_ref[0], mod_ref[1], n_seq)
    proj = _dot(h.astype(BF16), w_ref[...])
    u_ref[...] = proj[:, :S5_WIDTH]
    p_ref[...] = proj[:, S5_WIDTH:]


def _inproj_call(x, mod6, norm_g, w_in_bf, n_seq, tile):
    d = x.shape[-1]
    rows = x.size // d
    return pl.pallas_call(
        functools.partial(_inproj_kernel, n_seq=n_seq),
        grid=(rows // tile,),
        in_specs=[_token_spec(x, n_seq, tile), _full(mod6.shape), _full((1, d)),
                  _full(w_in_bf.shape)],
        out_specs=[pl.BlockSpec((tile, S5_WIDTH), lambda i: (i, 0)),
                   pl.BlockSpec((tile, RWKV_COLS), lambda i: (i, 0))],
        out_shape=[jax.ShapeDtypeStruct((rows, S5_WIDTH), F32), jax.ShapeDtypeStruct((rows, RWKV_COLS), F32)],
        compiler_params=_params(("arbitrary",)),
        name="norm1_inproj",
    )(x, mod6, norm_g.reshape(1, d), w_in_bf)


def _s5_prep_kernel(are_ref, aim_ref, ldt_ref, bre_ref, bim_ref, lbr_ref, lbi_ref, wre_ref, wim_ref):
    lam_re, lam_im = are_ref[...], aim_ref[...]
    dt = jnp.exp(ldt_ref[...])
    mag = jnp.exp(lam_re * dt)
    ang = lam_im * dt
    lb_re, lb_im = mag * jnp.cos(ang), mag * jnp.sin(ang)
    den = lam_re * lam_re + lam_im * lam_im
    n_re, n_im = lb_re - 1.0, lb_im
    coef_re = (n_re * lam_re + n_im * lam_im) / den
    coef_im = (n_im * lam_re - n_re * lam_im) / den
    lbr_ref[...] = lb_re
    lbi_ref[...] = lb_im
    for c in range(S5_CHUNKS):
        cr = coef_re[:, c * S5_CHUNK_ST:(c + 1) * S5_CHUNK_ST]
        ci = coef_im[:, c * S5_CHUNK_ST:(c + 1) * S5_CHUNK_ST]
        wre_ref[0, c], wre_ref[1, c] = _split_bf16(cr * bre_ref[c] - ci * bim_ref[c])
        wim_ref[0, c], wim_ref[1, c] = _split_bf16(cr * bim_ref[c] + ci * bre_ref[c])


def _s5_prep_call(a_re, a_im, log_dt, b_re, b_im):
    row = lambda a: a.reshape(1, S5_LANES)
    ldt = jnp.repeat(log_dt, S5_STATE).reshape(1, S5_LANES)
    w_shape = (2, S5_CHUNKS, S5_CHUNK_IN, S5_CHUNK_ST)
    return pl.pallas_call(
        _s5_prep_kernel,
        out_shape=[jax.ShapeDtypeStruct((1, S5_LANES), F32)] * 2 + [jax.ShapeDtypeStruct(w_shape, BF16)] * 2,
        compiler_params=pltpu.CompilerParams(vmem_limit_bytes=VMEM_LIMIT_BYTES),
        name="s5_discretise",
    )(row(a_re), row(a_im), ldt, _s5_in_blockdiag(b_re), _s5_in_blockdiag(b_im))


def _s5_in_blockdiag(b):
    gpc = S5_GROUPS // S5_CHUNKS
    bt = jnp.transpose(b, (0, 2, 1)).reshape(S5_CHUNKS, gpc, S5_GROUP, S5_STATE)
    eye = jnp.eye(gpc, dtype=b.dtype)
    bd = bt[:, :, :, None, :] * eye[None, :, None, :, None]
    return bd.reshape(S5_CHUNKS, S5_CHUNK_IN, S5_CHUNK_ST)


def _s5_out_blockdiag(c):
    gpc = S5_GROUPS // S5_CHUNKS
    ct = jnp.transpose(c, (0, 2, 1)).reshape(S5_CHUNKS, gpc, S5_STATE, S5_GROUP)
    eye = jnp.eye(gpc, dtype=c.dtype)
    bd = ct[:, :, :, None, :] * eye[None, :, None, :, None]
    return bd.reshape(S5_CHUNKS, S5_CHUNK_ST, S5_CHUNK_IN)


def _s5_kernel(u_ref, h0r_ref, h0i_ref, lbr_ref, lbi_ref, wre_ref, wim_ref, cre_ref, cim_ref, d_ref, wglu_ref,
               bglu_ref, y_ref, hr_out, hi_out, re_s, im_s, str_s, sti_s, *, n_seq, steps):
    chunk = pl.program_id(0)

    @pl.when(chunk == 0)
    def _():
        str_s[...] = h0r_ref[...]
        sti_s[...] = h0i_ref[...]

    u = u_ref[...]
    for c in range(S5_CHUNKS):
        uc = u[:, c * S5_CHUNK_IN:(c + 1) * S5_CHUNK_IN]
        re_s[:, c * S5_CHUNK_ST:(c + 1) * S5_CHUNK_ST] = _dot_split(uc, wre_ref[0, c], wre_ref[1, c])
        im_s[:, c * S5_CHUNK_ST:(c + 1) * S5_CHUNK_ST] = _dot_split(uc, wim_ref[0, c], wim_ref[1, c])

    def seq_block(rb, carry):
        r0 = pl.multiple_of(rb * SUBLANES, SUBLANES)
        for c in range(S5_CHUNKS):
            lanes = slice(c * S5_CHUNK_ST, (c + 1) * S5_CHUNK_ST)
            lr = jnp.broadcast_to(lbr_ref[:, lanes], (SUBLANES, S5_CHUNK_ST))
            li = jnp.broadcast_to(lbi_ref[:, lanes], (SUBLANES, S5_CHUNK_ST))

            def step(t, h):
                hr, hi = h
                row = pl.multiple_of(t * n_seq + r0, SUBLANES)
                nr = lr * hr - li * hi + re_s[pl.ds(row, SUBLANES), lanes]
                ni = lr * hi + li * hr + im_s[pl.ds(row, SUBLANES), lanes]
                re_s[pl.ds(row, SUBLANES), lanes] = nr
                im_s[pl.ds(row, SUBLANES), lanes] = ni
                return nr, ni

            h0 = (str_s[pl.ds(r0, SUBLANES), lanes], sti_s[pl.ds(r0, SUBLANES), lanes])
            hr, hi = lax.fori_loop(0, steps, step, h0, unroll=min(steps, 8))
            str_s[pl.ds(r0, SUBLANES), lanes] = hr
            sti_s[pl.ds(r0, SUBLANES), lanes] = hi
        return carry

    lax.fori_loop(0, n_seq // SUBLANES, seq_block, 0)

    ys = []
    for c in range(S5_CHUNKS):
        lanes = slice(c * S5_CHUNK_ST, (c + 1) * S5_CHUNK_ST)
        ys.append(_dot(re_s[:, lanes].astype(BF16), cre_ref[c]) - _dot(im_s[:, lanes].astype(BF16), cim_ref[c]))
    y = jnp.concatenate(ys, axis=-1) + d_ref[...] * u
    y = _gelu(y)
    y_ref[...] = y * jax.nn.sigmoid(_dot(y.astype(BF16), wglu_ref[...]) + bglu_ref[...])

    @pl.when(chunk == pl.num_programs(0) - 1)
    def _():
        hr_out[...] = str_s[...]
        hi_out[...] = sti_s[...]


def _s5_call(u_rows, h0_re, h0_im, consts, n_seq, steps):
    rows = u_rows.shape[0]
    tile = n_seq * steps
    lb_re, lb_im, w_re, w_im, c_re, c_im, d_skip, w_glu_bf, b_glu = consts
    state = jax.ShapeDtypeStruct((n_seq, S5_LANES), F32)
    args = (u_rows, h0_re, h0_im, lb_re, lb_im, w_re, w_im, c_re, c_im, d_skip, w_glu_bf, b_glu)
    in_specs = [pl.BlockSpec((tile, S5_WIDTH), lambda i: (i, 0))] + [_full(a.shape) for a in args[1:]]
    return pl.pallas_call(
        functools.partial(_s5_kernel, n_seq=n_seq, steps=steps),
        grid=(rows // tile,),
        in_specs=in_specs,
        out_specs=[pl.BlockSpec((tile, S5_WIDTH), lambda i: (i, 0)), _full(state.shape), _full(state.shape)],
        out_shape=[jax.ShapeDtypeStruct((rows, S5_WIDTH), F32), state, state],
        scratch_shapes=[pltpu.VMEM((tile, S5_LANES), F32), pltpu.VMEM((tile, S5_LANES), F32),
                        pltpu.VMEM((n_seq, S5_LANES), F32), pltpu.VMEM((n_seq, S5_LANES), F32)],
        compiler_params=_params(("arbitrary",)),
        name="s5_mixer",
    )(*args)


def _head_sum(x, ones_ref):
    hi, lo = _split_bf16(x)
    return _dot(hi, ones_ref[...]) + _dot(lo, ones_ref[...])


def _rwkv_pre_kernel(p_ref, prev_ref, shift_ref, mu_ref, w0_ref, w2_ref, a0_ref, a2_ref, g2_ref, kk_ref, ka_ref,
                     ones_ref, r_out, w_out, k_out, v_out, kk_out, kka_out, g_out, *, n_seq):
    p = p_ref[...]
    tile = p.shape[0]
    head = jnp.where(pl.program_id(0) == 0, shift_ref[...], prev_ref[...])
    p_prev = head if tile == n_seq else jnp.concatenate([head, p[:tile - n_seq]], axis=0)
    ps = p + (p_prev - p) * mu_ref[...]
    w = RWKV_WIDTH
    r, k, v = ps[:, :w], ps[:, w:2 * w], ps[:, 2 * w:3 * w]
    lo = ps[:, 3 * w:3 * w + RWKV_LORA]
    g_lo = ps[:, 3 * w + RWKV_LORA:]
    w_raw = -jax.nn.softplus(-(w0_ref[...] + _dot_split(jnp.tanh(lo), w2_ref[0], w2_ref[1]))) - 0.5
    a = jax.nn.sigmoid(a0_ref[...] + _dot_split(lo, a2_ref[0], a2_ref[1]))
    kk = k * kk_ref[...]
    norm = jnp.sqrt(_head_sum(kk * kk, ones_ref))
    kk = kk / jnp.maximum(norm, 1e-12)
    r_out[...] = r
    w_out[...] = jnp.exp(-jnp.exp(w_raw))
    k_out[...] = k * (1.0 + (a - 1.0) * ka_ref[...])
    v_out[...] = v
    kk_out[...] = kk
    kka_out[...] = kk * a
    g_out[...] = _dot_split(jax.nn.sigmoid(g_lo), g2_ref[0], g2_ref[1])


def _rwkv_pre_call(p_rows, shift0, consts, n_seq, tile):
    rows = p_rows.shape[0]
    per = tile // n_seq
    vec = jax.ShapeDtypeStruct((rows, RWKV_WIDTH), F32)
    in_specs = [pl.BlockSpec((tile, RWKV_COLS), lambda i: (i, 0)),
                pl.BlockSpec((n_seq, RWKV_COLS), lambda i: (jnp.maximum(i * per - 1, 0), 0)),
                _full(shift0.shape)] + [_full(a.shape) for a in consts]
    return pl.pallas_call(
        functools.partial(_rwkv_pre_kernel, n_seq=n_seq),
        grid=(rows // tile,),
        in_specs=in_specs,
        out_specs=[pl.BlockSpec((tile, RWKV_WIDTH), lambda i: (i, 0))] * 7,
        out_shape=[vec] * 7,
        compiler_params=_params(("arbitrary",)),
        name="rwkv_prologue",
    )(p_rows, p_rows, shift0, *consts)


def _sublane_allsum(p):
    p = p + pltpu.roll(p, 4, 0)
    p = p + pltpu.roll(p, 2, 0)
    return p + pltpu.roll(p, 1, 0)


def _pairs_from_rows(x, nb):
    cols = [x[:, c * LANES:(c + 1) * LANES] for c in range(RWKV_WIDTH // LANES)]
    m = jnp.concatenate(cols * (LANES // (4 * nb)), axis=0)
    mt = m.T
    lane = lax.broadcasted_iota(jnp.int32, (RWKV_HEAD, LANES), 1)
    even_head = (lane // nb) % RWKV_HEADS < RWKV_HEADS // 2
    return jnp.where(even_head, mt[:RWKV_HEAD], pltpu.roll(mt[RWKV_HEAD:], 4 * nb, 1))


def _rows_from_pairs(y, nb):
    pairs = RWKV_HEADS * nb
    shifts = [(-(copy * pairs + parity * pairs // 2)) % LANES
              for parity in range(2) for copy in range(LANES // pairs)]
    mt = jnp.concatenate([y if s == 0 else pltpu.roll(y, s, 1) for s in shifts], axis=0)
    m = mt.T
    return jnp.concatenate([m[c * nb:(c + 1) * nb, :] for c in range(RWKV_WIDTH // LANES)], axis=1)


def _rwkv_scan_kernel(r_ref, w_ref, k_ref, kk_ref, kka_ref, v_ref, s0_ref, y_ref, sout_ref, s_s, ka_s, kb_s, va_s,
                      vb_s, *, steps, n_v, nb):
    tc = pl.program_id(1)

    @pl.when(tc == 0)
    def _():
        s_s[...] = s0_ref[...]

    row_id = lax.broadcasted_iota(jnp.int32, (SUBLANES, LANES), 0)
    lane_id = lax.broadcasted_iota(jnp.int32, (n_v, LANES), 1)
    split = lambda x: x.reshape(RWKV_HEAD // SUBLANES, SUBLANES, LANES)
    k_refs = (r_ref, w_ref, k_ref, kk_ref, kka_ref)

    def stage(t, k_s, v_s):
        for i, ref in enumerate(k_refs):
            k_s[i] = _pairs_from_rows(ref[t], nb)
        full = _pairs_from_rows(v_ref[t], nb)
        v_s[...] = full if n_v == RWKV_HEAD else jnp.where(lane_id < LANES // 2, full[:n_v], full[n_v:])

    def advance(t, k_s, v_s):
        r, w, k, kk, kka = (split(k_s[i]) for i in range(len(k_refs)))
        y_tiles = []
        for vb in range(n_v // SUBLANES):
            v_tile = v_s[vb * SUBLANES:(vb + 1) * SUBLANES, :]
            y_tile = jnp.zeros((SUBLANES, LANES), F32)
            for j in range(SUBLANES):
                s = split(s_s[vb * SUBLANES + j])
                sa = -_sublane_allsum(jnp.sum(s * kk, axis=0))
                v_row = jnp.broadcast_to(v_tile[j:j + 1, :], (SUBLANES, LANES))
                s = s * w + sa[None] * kka + v_row[None] * k
                s_s[vb * SUBLANES + j] = s.reshape(RWKV_HEAD, LANES)
                y_tile = jnp.where(row_id == j, _sublane_allsum(jnp.sum(s * r, axis=0)), y_tile)
            y_tiles.append(y_tile)
        y_ref[t] = _rows_from_pairs(jnp.concatenate(y_tiles, axis=0), nb)

    stage(0, ka_s, va_s)

    def two_steps(i, carry):
        t = 2 * i
        stage(t + 1, kb_s, vb_s)
        advance(t, ka_s, va_s)
        stage(jnp.minimum(t + 2, steps - 1), ka_s, va_s)
        advance(t + 1, kb_s, vb_s)
        return carry

    lax.fori_loop(0, steps // 2, two_steps, 0)

    @pl.when(tc == pl.num_programs(1) - 1)
    def _():
        sout_ref[...] = s_s[...]


def _rwkv_scan_call(r, w, k, kk, kka, v, s0, n_seq, steps):
    rows = r.shape[0]
    n_t = rows // n_seq
    n_v, _, lanes = s0.shape
    nb = n_seq // (lanes // LANES)
    assert steps % 2 == 0 and n_t % steps == 0
    as_steps = lambda a: a.reshape(n_t, n_seq, RWKV_WIDTH)
    xspec = pl.BlockSpec((steps, nb, RWKV_WIDTH), lambda g, i: (i, g, 0))
    sspec = pl.BlockSpec((n_v, RWKV_HEAD, LANES), lambda g, i: (0, 0, g))
    stage_k = pltpu.VMEM((5, RWKV_HEAD, LANES), F32)
    stage_v = pltpu.VMEM((n_v, LANES), F32)
    y, s = pl.pallas_call(
        functools.partial(_rwkv_scan_kernel, steps=steps, n_v=n_v, nb=nb),
        grid=(lanes // LANES, n_t // steps),
        in_specs=[xspec] * 6 + [sspec],
        out_specs=[xspec, sspec],
        out_shape=[jax.ShapeDtypeStruct((n_t, n_seq, RWKV_WIDTH), F32), jax.ShapeDtypeStruct(s0.shape, F32)],
        scratch_shapes=[pltpu.VMEM((n_v, RWKV_HEAD, LANES), F32), stage_k, stage_k, stage_v, stage_v],
        compiler_params=_params(("arbitrary", "arbitrary")),
        name="rwkv_recurrence",
    )(*[as_steps(a) for a in (r, w, k, kk, kka, v)], s0)
    return y.reshape(rows, RWKV_WIDTH), s


def _state_to_pairs(s, nb):
    n_seq = s.shape[0]
    v_split = LANES // (RWKV_HEADS * nb)
    n_v = RWKV_HEAD // v_split
    s = s.reshape(n_seq // nb, nb, RWKV_HEADS // 2, 2, v_split, n_v, RWKV_HEAD)
    return jnp.transpose(s, (5, 6, 0, 4, 3, 2, 1)).reshape(n_v, RWKV_HEAD, n_seq // nb * LANES)


def _state_from_pairs(s, n_seq, nb):
    v_split = LANES // (RWKV_HEADS * nb)
    n_v = RWKV_HEAD // v_split
    s = s.reshape(n_v, RWKV_HEAD, n_seq // nb, v_split, 2, RWKV_HEADS // 2, nb)
    return jnp.transpose(s, (2, 6, 5, 4, 3, 0, 1)).reshape(n_seq, RWKV_HEADS, RWKV_HEAD, RWKV_HEAD)


def _mixout_kernel(x_ref, ys5_ref, yrw_ref, r_ref, k_ref, v_ref, g_ref, mod_ref, gnw_ref, gnb_ref, rk_ref,
                   ones_ref, wout_ref, n2g_ref, x1_ref, h2_ref, *, n_seq):
    y = yrw_ref[...]
    inv_n = 1.0 / RWKV_HEAD
    mean = _head_sum(y, ones_ref) * inv_n
    yc = y - mean
    var = _head_sum(yc * yc, ones_ref) * inv_n
    y = yc * lax.rsqrt(var + GN_EPS) * gnw_ref[...] + gnb_ref[...]
    v = v_ref[...]
    y = y + _head_sum(r_ref[...] * k_ref[...] * rk_ref[...], ones_ref) * v
    y = y * g_ref[...]
    mix = jnp.concatenate([ys5_ref[...], y], axis=-1).astype(BF16)
    x1 = _time_major_rows(x_ref) + _gated(_dot(mix, wout_ref[...]), mod_ref[2], n_seq)
    x1_ref[...] = x1
    h2_ref[...] = _modulate(_rms(x1, n2g_ref[...]), mod_ref[3], mod_ref[4], n_seq)


def _mixout_call(x, y_s5, y_rw, r, k, v, g, mod6, consts, n_seq, tile):
    d = x.shape[-1]
    rows = x.size // d
    wide = pl.BlockSpec((tile, d), lambda i: (i, 0))
    half = pl.BlockSpec((tile, RWKV_WIDTH), lambda i: (i, 0))
    return pl.pallas_call(
        functools.partial(_mixout_kernel, n_seq=n_seq),
        grid=(rows // tile,),
        in_specs=[_token_spec(x, n_seq, tile)] + [half] * 6 + [_full(mod6.shape)] + [_full(a.shape) for a in consts],
        out_specs=[wide, wide],
        out_shape=[jax.ShapeDtypeStruct((rows, d), F32)] * 2,
        compiler_params=_params(("arbitrary",)),
        name="mixer_out_norm2",
    )(x, y_s5, y_rw, r, k, v, g, mod6, *consts)


def _sort16_pairs():
    pairs = []
    k = 2
    while k <= PEER_TOPK:
        j = k // 2
        while j >= 1:
            pairs += [(i, i ^ j, (i & k) == 0) for i in range(PEER_TOPK) if i ^ j > i]
            j //= 2
        k *= 2
    return pairs


def _bitonic_merge_desc(x):
    x = list(x)
    j = PEER_TOPK // 2
    while j >= 1:
        for i in range(PEER_TOPK):
            if not i & j:
                x[i], x[i | j] = jnp.maximum(x[i], x[i | j]), jnp.minimum(x[i], x[i | j])
        j //= 2
    return x


def _top16_desc(slabs):
    x = list(slabs)
    for i, l, i_max in _sort16_pairs():
        hi, lo = jnp.maximum(x[i], x[l]), jnp.minimum(x[i], x[l])
        x[i], x[l] = (hi, lo) if i_max else (lo, hi)
    for shift in (1, 2, 4):
        x = _bitonic_merge_desc([jnp.maximum(x[v], pltpu.roll(x[PEER_TOPK - 1 - v], shift, 0))
                                 for v in range(PEER_TOPK)])
    return x


def _pair_candidates(a, b):
    sub = lax.broadcasted_iota(jnp.int32, a[0].shape, 0)
    b_lo, b_hi = b[0], b[SUBLANES]
    for s in range(1, SUBLANES):
        b_lo = jnp.where(sub == s, b[s], b_lo)
    for s in range(2, SUBLANES):
        b_hi = jnp.where(sub == s, b[SUBLANES - 1 + s], b_hi)
    cand = [a[i] + b_lo for i in range(PEER_TOPK)]
    cand[15] = jnp.where(sub == 0, cand[15], a[0] + b_hi)
    cand[14] = jnp.where(sub == 1, a[0] + b[PEER_TOPK - 1], cand[14])
    return cand


def _count_at_least(slabs, bound):
    count = jnp.zeros_like(bound)
    for s in slabs:
        count = count + jnp.where(s >= bound, 1.0, 0.0)
    return _sublane_allsum(count)


def _peer_select(xt, wq_ref, k1_ref, k2_ref, q_s, a_s, b_s, e1_s, e2_s, thr_s, tie_s):
    tg = xt.shape[1]
    q_s[...] = _dot_split_lhs(wq_ref[0], wq_ref[1], xt).reshape(PEER_HEADS, 2 * PEER_HALF, tg)
    slabs = lambda s: [s[v * SUBLANES:(v + 1) * SUBLANES, :] for v in range(N_KEYS // SUBLANES)]
    tiled = lambda row: jnp.concatenate([row] * (N_KEYS // SUBLANES), axis=0)
    sub = lax.broadcasted_iota(jnp.int32, (SUBLANES, tg), 0)

    def head(h, carry):
        q = q_s[h]
        s1 = _dot(k1_ref[h], q[:PEER_HALF], HIGHEST)
        s2 = _dot(k2_ref[h], q[PEER_HALF:], HIGHEST)
        top1 = _top16_desc(slabs(s1))
        top2 = _top16_desc(slabs(s2))
        topc = _top16_desc(_pair_candidates(top1, top2))
        z = jnp.ones_like(topc[0])
        for i in range(1, PEER_TOPK):
            z = z + jnp.exp(topc[i] - topc[0])
        a = jnp.where(s1 >= tiled(top1[-1]), s1, -jnp.inf)
        b = jnp.where(s2 >= tiled(top2[-1]), s2, -jnp.inf)
        a_s[h] = a
        b_s[h] = b
        e1_s[h] = jnp.exp(a - tiled(top1[0])) / tiled(z)
        e2_s[h] = jnp.exp(b - tiled(top2[0]))
        thr_s[h] = topc[-1]
        halves = []
        for j0 in (0, SUBLANES):
            t2 = top2[j0]
            for s in range(1, SUBLANES):
                t2 = jnp.where(sub == s, top2[j0 + s], t2)
            halves.append(t2)
        sums = [top1[i] + t2 for i in range(PEER_TOPK) for t2 in halves]
        tied = jnp.zeros_like(topc[0])
        for values, bound in ((slabs(s1), top1[-1]), (slabs(s2), top2[-1]), (sums, topc[-1])):
            tied = jnp.maximum(tied, jnp.where(_count_at_least(values, bound) == PEER_TOPK, 0.0, 1.0))
        tie_s[...] = jnp.maximum(tie_s[...], tied)
        return carry

    lax.fori_loop(0, PEER_HEADS, head, 0)


def _stable_top16(x, row_id):
    vals, rows = [], []
    for _ in range(PEER_TOPK):
        m = jnp.max(x, axis=0, keepdims=True)
        r = jnp.min(jnp.where(x == m, row_id, float(x.shape[0])), axis=0, keepdims=True)
        vals.append(m)
        rows.append(r)
        x = jnp.where(row_id == r, -jnp.inf, x)
    return vals, rows


def _peer_select_exact(k1_ref, k2_ref, q_s, a_s, b_s, e1_s, e2_s, thr_s, pa_s, pb_s, pthr_s, lst_s, cand_s):
    tg = q_s.shape[2]
    key_id = lax.broadcasted_iota(jnp.int32, (N_KEYS, tg), 0).astype(F32)
    pair_id = lax.broadcasted_iota(jnp.int32, (PEER_TOPK * PEER_TOPK, tg), 0).astype(F32)

    def head(h, carry):
        q = q_s[h]
        s1 = _dot(k1_ref[h], q[:PEER_HALF], HIGHEST)
        s2 = _dot(k2_ref[h], q[PEER_HALF:], HIGHEST)
        v1, i1 = _stable_top16(s1, key_id)
        v2, i2 = _stable_top16(s2, key_id)
        rank1 = jnp.full((N_KEYS, tg), float(PEER_TOPK), F32)
        rank2 = rank1
        for r in range(PEER_TOPK):
            rank1 = jnp.where(key_id == i1[r], float(r), rank1)
            rank2 = jnp.where(key_id == i2[r], float(r), rank2)
            lst_s[r:r + 1, :] = v2[r]
        best2 = lst_s[...]
        for r in range(PEER_TOPK):
            cand_s[r * PEER_TOPK:(r + 1) * PEER_TOPK, :] = v1[r] + best2
        vc, pc = _stable_top16(cand_s[...], pair_id)
        z = jnp.ones_like(vc[0])
        for r in range(1, PEER_TOPK):
            z = z + jnp.exp(vc[r] - vc[0])
        a = jnp.where(rank1 < PEER_TOPK, s1, -jnp.inf)
        b = jnp.where(rank2 < PEER_TOPK, s2, -jnp.inf)
        a_s[h] = a
        b_s[h] = b
        e1_s[h] = jnp.exp(a - v1[0]) / z
        e2_s[h] = jnp.exp(b - v2[0])
        pa_s[h] = rank1 * float(PEER_TOPK)
        pb_s[h] = rank2
        thr_s[h] = jnp.broadcast_to(vc[-1], (SUBLANES, tg))
        pthr_s[h] = jnp.broadcast_to(pc[-1], (SUBLANES, tg))
        return carry

    lax.fori_loop(0, PEER_HEADS, head, 0)


def _peer_gates(g, key0, act_ref, pt_ref, a_s, b_s, e1_s, e2_s, thr_s):
    tg = act_ref.shape[2]
    for lg in range(tg // LANES):
        ls = slice(lg * LANES, (lg + 1) * LANES)
        a_rows = [a_s[g, h, pl.ds(key0, PEER_KEYS_PER_BLOCK), ls] for h in range(PEER_HEADS)]
        e_rows = [e1_s[g, h, pl.ds(key0, PEER_KEYS_PER_BLOCK), ls] for h in range(PEER_HEADS)]
        for i in range(PEER_KEYS_PER_BLOCK):
            gate = jnp.zeros((N_KEYS, LANES), F32)
            for h in range(PEER_HEADS):
                score = b_s[g, h, :, ls] + a_rows[h][i:i + 1, :]
                weight = e2_s[g, h, :, ls] * e_rows[h][i:i + 1, :]
                gate = gate + jnp.where(score >= thr_s[g, h, 0:1, ls], weight, 0.0)
            act = act_ref[g, i * N_KEYS:(i + 1) * N_KEYS, ls]
            pt_ref[g, i * N_KEYS:(i + 1) * N_KEYS, ls] = (gate * act).astype(BF16)


def _peer_gates_exact(g, key0, act_ref, pt_ref, a_s, b_s, e1_s, e2_s, thr_s, pa_s, pb_s, pthr_s):
    tg = act_ref.shape[2]
    rows = pl.ds(key0, PEER_KEYS_PER_BLOCK)
    for lg in range(tg // LANES):
        ls = slice(lg * LANES, (lg + 1) * LANES)
        for i in range(PEER_KEYS_PER_BLOCK):
            gate = jnp.zeros((N_KEYS, LANES), F32)
            for h in range(PEER_HEADS):
                score = b_s[g, h, :, ls] + a_s[g, h, rows, ls][i:i + 1, :]
                pos = pb_s[g, h, :, ls] + pa_s[g, h, rows, ls][i:i + 1, :]
                weight = e2_s[g, h, :, ls] * e1_s[g, h, rows, ls][i:i + 1, :]
                thr = thr_s[g, h, 0:1, ls]
                tied_in = jnp.where(score == thr, jnp.where(pos <= pthr_s[g, h, 0:1, ls], weight, 0.0), 0.0)
                gate = gate + jnp.where(score > thr, weight, tied_in)
            act = act_ref[g, i * N_KEYS:(i + 1) * N_KEYS, ls]
            pt_ref[g, i * N_KEYS:(i + 1) * N_KEYS, ls] = (gate * act).astype(BF16)


def _peer_kernel(h2_ref, x1_ref, ga_ref, wq_ref, k1_ref, k2_ref, u_ref, vt_ref, fg_ref, o_ref,
                 xt_s, q_s, a_s, b_s, e1_s, e2_s, thr_s, pa_s, pb_s, pthr_s, tie_s, lst_s, cand_s, tied_s,
                 act0_s, act1_s, pt0_s, pt1_s, acc_s, *, n_seq, n_blocks):
    s = pl.program_id(1)
    n_groups = xt_s.shape[0]
    tg = xt_s.shape[2]
    acts, pts = (act0_s, act1_s), (pt0_s, pt1_s)
    sel = (a_s, b_s, e1_s, e2_s, thr_s)
    sel_exact = sel + (pa_s, pb_s, pthr_s)

    def stages(parity, first, gates, second, exact=False):
        key0 = pl.multiple_of((s - 1) * PEER_KEYS_PER_BLOCK, PEER_KEYS_PER_BLOCK)

        def group(g, carry):
            if first:
                acts[parity][g] = _gelu(_dot(u_ref[...], xt_s[g]))
            if gates and exact:
                _peer_gates_exact(g, key0, acts[1 - parity], pts[1 - parity], *sel_exact)
            elif gates:
                _peer_gates(g, key0, acts[1 - parity], pts[1 - parity], *sel)
            if second:
                acc_s[g] += _dot(vt_ref[...], pts[parity][g])
            return carry

        lax.fori_loop(0, n_groups, group, 0)

    @pl.when(s == 0)
    def _():
        tie_s[...] = jnp.zeros_like(tie_s)

        def prep(g, carry):
            t0 = pl.multiple_of(g * tg, tg)
            xt = h2_ref[pl.ds(t0, tg), :].T
            xt_s[g] = xt.astype(BF16)
            _peer_select(xt, wq_ref, k1_ref, k2_ref, q_s, a_s.at[g], b_s.at[g], e1_s.at[g], e2_s.at[g],
                         thr_s.at[g], tie_s)
            return carry

        lax.fori_loop(0, n_groups, prep, 0)
        tied_s[0] = (jnp.max(tie_s[...]) > 0.0).astype(jnp.int32)

        @pl.when(tied_s[0] != 0)
        def _():
            def prep_exact(g, carry):
                t0 = pl.multiple_of(g * tg, tg)
                xt = h2_ref[pl.ds(t0, tg), :].T
                q_s[...] = _dot_split_lhs(wq_ref[0], wq_ref[1], xt).reshape(q_s.shape)
                _peer_select_exact(k1_ref, k2_ref, q_s, a_s.at[g], b_s.at[g], e1_s.at[g], e2_s.at[g], thr_s.at[g],
                                   pa_s.at[g], pb_s.at[g], pthr_s.at[g], lst_s, cand_s)
                return carry

            lax.fori_loop(0, n_groups, prep_exact, 0)

        acc_s[...] = jnp.zeros_like(acc_s)
        pt1_s[...] = jnp.zeros_like(pt1_s)
        stages(0, True, False, False)

    for exact in (False, True):
        for parity in (0, 1):
            @pl.when((s >= 1) & (s < n_blocks) & (s % 2 == parity) & ((tied_s[0] != 0) == exact))
            def _(parity=parity, exact=exact):
                stages(parity, True, True, True, exact)

        @pl.when((s == n_blocks) & ((tied_s[0] != 0) == exact))
        def _(exact=exact):
            stages(n_blocks % 2, False, True, True, exact)

    @pl.when(s == n_blocks + 1)
    def _():
        stages((n_blocks + 1) % 2, False, False, True)
        for g in range(n_groups):
            rows = slice(g * tg, (g + 1) * tg)
            x2 = x1_ref[rows, :] + _gated(acc_s[g].T, ga_ref[...], min(n_seq, tg))
            y = _rms(x2, fg_ref[...])
            if len(o_ref.shape) == 2:
                o_ref[rows, :] = y
            else:
                sg = tg // n_seq
                o_ref[:, g * sg:(g + 1) * sg, :] = jnp.swapaxes(y.reshape(sg, n_seq, y.shape[1]), 0, 1)


def _peer_call(h2, x1, gate2, wq_t, keys1, keys2, u_bf, vt_bf, final_g, n_seq, tm, batch_major):
    rows, d = h2.shape
    n_exp = u_bf.shape[0]
    eb = PEER_KEYS_PER_BLOCK * N_KEYS
    n_blocks = n_exp // eb
    tg = PEER_TOKEN_GROUP
    n_groups = tm // tg
    assert tm % tg == 0 and (tg % n_seq == 0 or n_seq % tg == 0) and n_exp % eb == 0 and n_blocks >= 2
    if n_seq > tg:
        raise NotImplementedError("more than PEER_TOKEN_GROUP sequences per step")
    tok = pl.BlockSpec((tm, d), lambda i, s: (i, 0))
    out = jax.ShapeDtypeStruct((n_seq, rows // n_seq, d) if batch_major else (rows, d), F32)
    sel = pltpu.VMEM((n_groups, PEER_HEADS, N_KEYS, tg), F32)
    row = pltpu.VMEM((n_groups, PEER_HEADS, SUBLANES, tg), F32)
    act = pltpu.VMEM((n_groups, eb, tg), F32)
    pt = pltpu.VMEM((n_groups, eb, tg), BF16)
    return pl.pallas_call(
        functools.partial(_peer_kernel, n_seq=n_seq, n_blocks=n_blocks),
        grid=(rows // tm, n_blocks + 2),
        in_specs=[tok, tok, _full(gate2.shape), _full(wq_t.shape), _full(keys1.shape), _full(keys2.shape),
                  pl.BlockSpec((eb, d), lambda i, s: (jnp.minimum(s, n_blocks - 1), 0)),
                  pl.BlockSpec((d, eb), lambda i, s: (0, jnp.clip(s - 2, 0, n_blocks - 1))),
                  _full((1, d))],
        out_specs=_token_spec(out, n_seq, tm),
        out_shape=out,
        scratch_shapes=[pltpu.VMEM((n_groups, d, tg), BF16), pltpu.VMEM((PEER_HEADS, 2 * PEER_HALF, tg), F32),
                        sel, sel, sel, sel, row, sel, sel, row, pltpu.VMEM((SUBLANES, tg), F32),
                        pltpu.VMEM((PEER_TOPK, tg), F32), pltpu.VMEM((PEER_TOPK * PEER_TOPK, tg), F32),
                        pltpu.SMEM((1,), jnp.int32),
                        act, act, pt, pt, pltpu.VMEM((n_groups, d, tg), F32)],
        compiler_params=_params(("arbitrary", "arbitrary")),
        name="peer_final_norm",
    )(h2, x1, gate2, wq_t, keys1, keys2, u_bf, vt_bf, final_g.reshape(1, d))


def _pick(n, target):
    t = min(n, target)
    while n % t:
        t -= 1
    return t


def _layer(x_btd, mod, s5_re0, s5_im0, wkv0, shift0, prm):
    n_seq, n_t, d = x_btd.shape
    rows = n_seq * n_t
    mod6 = jnp.transpose(mod.reshape(n_seq, 6, d), (1, 0, 2))
    tile = n_seq * _pick(n_t, max(1, 512 // n_seq))
    reorder_in_kernel = (PEER_TOKEN_GROUP // n_seq) % SUBLANES == 0 and (tile // n_seq) % SUBLANES == 0
    x_in = x_btd if reorder_in_kernel else jnp.transpose(x_btd, (1, 0, 2)).reshape(rows, d)

    u_rows, p_rows = _inproj_call(x_in, mod6, prm["norm1_g"], prm["w_in_bf"], n_seq, tile)

    s5_steps = _pick(n_t, max(1, 512 // n_seq))
    y_s5, s5_re, s5_im = _s5_call(u_rows, s5_re0.reshape(n_seq, S5_LANES), s5_im0.reshape(n_seq, S5_LANES),
                                  prm["s5_consts"], n_seq, s5_steps)

    r, w, k, v, kk, kka, g = _rwkv_pre_call(p_rows, shift0, prm["rwkv_pre_consts"], n_seq, tile)
    nb = min(n_seq, LANES // RWKV_HEADS)
    y_rw, s_pairs = _rwkv_scan_call(r, w, k, kk, kka, v, _state_to_pairs(wkv0, nb), n_seq, _pick(n_t, 16))
    wkv = _state_from_pairs(s_pairs, n_seq, nb)

    x1, h2 = _mixout_call(x_in, y_s5, y_rw, r, k, v, g, mod6, prm["mixout_consts"], n_seq, tile)

    y = _peer_call(h2, x1, mod6[5], prm["peer_wq_t"], prm["peer_keys1"], prm["peer_keys2"], prm["peer_u_bf"],
                   prm["peer_vt_bf"], prm["final_norm_g"], n_seq, tile, reorder_in_kernel)
    if not reorder_in_kernel:
        y = jnp.transpose(y.reshape(n_t, n_seq, d), (1, 0, 2))
    shift = p_rows[rows - n_seq:]
    return (y, s5_re.reshape(n_seq, S5_GROUPS, S5_STATE), s5_im.reshape(n_seq, S5_GROUPS, S5_STATE), wkv, shift)


def kernel(x_prompt, x_sample, state_s5_re, state_s5_im, state_wkv, state_shift, c_prompt, c_sample, w_ada, b_ada, norm1_g, norm2_g, w_in, w_out, s5_a_re, s5_a_im, s5_log_dt, s5_b_re, s5_b_im, s5_c_re, s5_c_im, s5_d, w_glu, b_glu, rwkv_mu, rwkv_w0, rwkv_w2, rwkv_a0, rwkv_a2, rwkv_g2, rwkv_k_k, rwkv_k_a, rwkv_r_k, rwkv_gn_w, rwkv_gn_b, peer_w_q, peer_keys1, peer_keys2, peer_u, peer_v, final_norm_g):
    assert w_ada.shape[0] == 1, "single-layer model"
    nbp = x_prompt.shape[0]
    row = lambda a: a.reshape(1, -1)

    mod = _ada_call(jnp.concatenate([c_prompt, c_sample], axis=0).astype(F32), w_ada[0], b_ada[0])

    lb_re, lb_im, bw_re, bw_im = _s5_prep_call(s5_a_re[0], s5_a_im[0], s5_log_dt[0], s5_b_re[0], s5_b_im[0])
    head_ones = jnp.kron(jnp.eye(RWKV_HEADS, dtype=BF16), jnp.ones((RWKV_HEAD, RWKV_HEAD), BF16))
    lora_pad = jnp.zeros((RWKV_LORA // 2, RWKV_WIDTH), F32)
    prm = {
        "norm1_g": norm1_g[0],
        "w_in_bf": w_in[0].astype(BF16),
        "s5_consts": (lb_re, lb_im, bw_re, bw_im, _s5_out_blockdiag(s5_c_re[0]).astype(BF16),
                      _s5_out_blockdiag(s5_c_im[0]).astype(BF16), row(s5_d[0]), w_glu[0].astype(BF16),
                      row(b_glu[0])),
        "rwkv_pre_consts": (row(rwkv_mu[0]), row(rwkv_w0[0]),
                            _split_call(jnp.concatenate([rwkv_w2[0], lora_pad], axis=0)), row(rwkv_a0[0]),
                            _split_call(jnp.concatenate([lora_pad, rwkv_a2[0]], axis=0)), _split_call(rwkv_g2[0]),
                            row(rwkv_k_k[0]), row(rwkv_k_a[0]), head_ones),
        "mixout_consts": (row(rwkv_gn_w[0]), row(rwkv_gn_b[0]), row(rwkv_r_k[0]), head_ones,
                          w_out[0].astype(BF16), row(norm2_g[0])),
        "peer_wq_t": _split_call(peer_w_q[0].T),
        "peer_keys1": peer_keys1[0],
        "peer_keys2": peer_keys2[0],
        "peer_u_bf": peer_u[0].astype(BF16),
        "peer_vt_bf": peer_v[0].T.astype(BF16),
        "final_norm_g": final_norm_g,
    }

    z_s5 = jnp.zeros((nbp, S5_GROUPS, S5_STATE), F32)
    z_wkv = jnp.zeros((nbp, RWKV_HEADS, RWKV_HEAD, RWKV_HEAD), F32)
    z_sh = jnp.zeros((nbp, RWKV_COLS), F32)
    yp, pr, pi, pw, psh = _layer(x_prompt.astype(F32), mod[:nbp], z_s5, z_s5, z_wkv, z_sh, prm)
    ys, sr, si, sw, ssh = _layer(x_sample.astype(F32), mod[nbp:], state_s5_re[0].astype(F32),
                                 state_s5_im[0].astype(F32), state_wkv[0].astype(F32),
                                 state_shift[0].astype(F32), prm)
    return (yp.astype(x_prompt.dtype), ys.astype(x_sample.dtype), pr[None], pi[None], pw[None], psh[None],
            sr[None], si[None], sw[None], ssh[None])
```

```python
import functools

import jax
import jax.numpy as jnp
from jax import lax
from jax.experimental import pallas as pl
from jax.experimental.pallas import tpu as pltpu

F32 = jnp.float32
BF16 = jnp.bfloat16
HIGHEST = lax.Precision.HIGHEST

LANES = 128
SUBLANES = 8
VMEM_LIMIT_BYTES = 56 * 1024 * 1024

D_MODEL = 1024
S5_WIDTH = 512
S5_GROUP = 16
S5_GROUPS = 32
S5_STATE = 64
S5_LANES = S5_GROUPS * S5_STATE
S5_CHUNKS = 4
S5_CHUNK_IN = S5_WIDTH // S5_CHUNKS
S5_CHUNK_ST = S5_LANES // S5_CHUNKS
RWKV_WIDTH = 512
RWKV_HEAD = 64
RWKV_HEADS = 8
RWKV_LORA = 128
RWKV_COLS = 3 * RWKV_WIDTH + 64 + 64 + 128
PEER_HEADS = 8
N_KEYS = 128
PEER_TOPK = 16
PEER_HALF = 64
PEER_TOKEN_GROUP = 256
PEER_KEYS_PER_BLOCK = 8
NORM_EPS = 1e-6
GN_EPS = 64e-5


def _params(sem):
    return pltpu.CompilerParams(dimension_semantics=sem, vmem_limit_bytes=VMEM_LIMIT_BYTES)


def _full(shape):
    return pl.BlockSpec(shape, lambda *_: (0,) * len(shape))


def _dot(a, b, precision=None):
    return jnp.dot(a, b, precision=precision, preferred_element_type=F32)


def _split_bf16(x):
    hi = x.astype(BF16)
    return hi, (x - hi.astype(F32)).astype(BF16)


def _dot_split(a, b_hi, b_lo):
    a_hi, a_lo = _split_bf16(a)
    return _dot(a_hi, b_hi) + (_dot(a_lo, b_hi) + _dot(a_hi, b_lo))


def _dot_split_lhs(a_hi, a_lo, b):
    b_hi, b_lo = _split_bf16(b)
    return _dot(a_hi, b_hi) + (_dot(a_lo, b_hi) + _dot(a_hi, b_lo))


def _gelu(x):
    return 0.5 * x * (1.0 + lax.erf(x * (2.0 ** -0.5)))


def _rms(x, g):
    return x * lax.rsqrt(jnp.mean(x * x, axis=-1, keepdims=True) + NORM_EPS) * g


def _modulate(h, shift, scale, n_seq):
    rows, d = h.shape
    h3 = h.reshape(rows // n_seq, n_seq, d)
    return (h3 * (1.0 + scale)[None] + shift[None]).reshape(rows, d)


def _gated(h, gate, n_seq):
    rows, d = h.shape
    return (h.reshape(rows // n_seq, n_seq, d) * gate[None]).reshape(rows, d)


def _split_kernel(w_ref, o_ref):
    o_ref[0], o_ref[1] = _split_bf16(w_ref[...])


def _split_call(w):
    return pl.pallas_call(
        _split_kernel,
        out_shape=jax.ShapeDtypeStruct((2,) + w.shape, BF16),
        compiler_params=pltpu.CompilerParams(vmem_limit_bytes=VMEM_LIMIT_BYTES),
        name="split_weight",
    )(w)


def _ada_kernel(c_ref, w_ref, b_ref, o_ref):
    s = jax.nn.silu(c_ref[...])
    o_ref[...] = _dot(s, w_ref[...], HIGHEST) + b_ref[...]


def _ada_call(c_all, w_ada, b_ada):
    n, d = c_all.shape
    cols = w_ada.shape[1]
    return pl.pallas_call(
        _ada_kernel,
        grid=(cols // d,),
        in_specs=[_full((n, d)), pl.BlockSpec((d, d), lambda j: (0, j)), pl.BlockSpec((1, d), lambda j: (0, j))],
        out_specs=pl.BlockSpec((n, d), lambda j: (0, j)),
        out_shape=jax.ShapeDtypeStruct((n, cols), F32),
        compiler_params=_params(("arbitrary",)),
        name="adaln_mod",
    )(c_all, w_ada, b_ada.reshape(1, cols))


def _time_major_rows(x_ref):
    if len(x_ref.shape) == 2:
        return x_ref[...]
    n_seq, steps, d = x_ref.shape
    return jnp.swapaxes(x_ref[...], 0, 1).reshape(n_seq * steps, d)


def _token_spec(x, n_seq, tile):
    if x.ndim == 2:
        return pl.BlockSpec((tile, x.shape[1]), lambda i, *_: (i, 0))
    return pl.BlockSpec((n_seq, tile // n_seq, x.shape[2]), lambda i, *_: (0, i, 0))


def _inproj_kernel(x_ref, mod_ref, g_ref, w_ref, u_ref, p_ref, *, n_seq):
    h = _modulate(_rms(_time_major_rows(x_ref), g_ref[...]), mod_ref[0], mod_ref[1], n_seq)
    proj = _dot(h.astype(BF16), w_ref[...])
    u_ref[...] = proj[:, :S5_WIDTH]
    p_ref[...] = proj[:, S5_WIDTH:]


def _inproj_call(x, mod6, norm_g, w_in_bf, n_seq, tile):
    d = x.shape[-1]
    rows = x.size // d
    return pl.pallas_call(
        functools.partial(_inproj_kernel, n_seq=n_seq),
        grid=(rows // tile,),
        in_specs=[_token_spec(x, n_seq, tile), _full(mod6.shape), _full((1, d)),
                  _full(w_in_bf.shape)],
        out_specs=[pl.BlockSpec((tile, S5_WIDTH), lambda i: (i, 0)),
                   pl.BlockSpec((tile, RWKV_COLS), lambda i: (i, 0))],
        out_shape=[jax.ShapeDtypeStruct((rows, S5_WIDTH), F32), jax.ShapeDtypeStruct((rows, RWKV_COLS), F32)],
        compiler_params=_params(("arbitrary",)),
        name="norm1_inproj",
    )(x, mod6, norm_g.reshape(1, d), w_in_bf)


def _s5_prep_kernel(are_ref, aim_ref, ldt_ref, bre_ref, bim_ref, lbr_ref, lbi_ref, wre_ref, wim_ref):
    lam_re, lam_im = are_ref[...], aim_ref[...]
    dt = jnp.exp(ldt_ref[...])
    mag = jnp.exp(lam_re * dt)
    ang = lam_im * dt
    lb_re, lb_im = mag * jnp.cos(ang), mag * jnp.sin(ang)
    den = lam_re * lam_re + lam_im * lam_im
    n_re, n_im = lb_re - 1.0, lb_im
    coef_re = (n_re * lam_re + n_im * lam_im) / den
    coef_im = (n_im * lam_re - n_re * lam_im) / den
    lbr_ref[...] = lb_re
    lbi_ref[...] = lb_im
    for c in range(S5_CHUNKS):
        cr = coef_re[:, c * S5_CHUNK_ST:(c + 1) * S5_CHUNK_ST]
        ci = coef_im[:, c * S5_CHUNK_ST:(c + 1) * S5_CHUNK_ST]
        wre_ref[0, c], wre_ref[1, c] = _split_bf16(cr * bre_ref[c] - ci * bim_ref[c])
        wim_ref[0, c], wim_ref[1, c] = _split_bf16(cr * bim_ref[c] + ci * bre_ref[c])


def _s5_prep_call(a_re, a_im, log_dt, b_re, b_im):
    row = lambda a: a.reshape(1, S5_LANES)
    ldt = jnp.repeat(log_dt, S5_STATE).reshape(1, S5_LANES)
    w_shape = (2, S5_CHUNKS, S5_CHUNK_IN, S5_CHUNK_ST)
    return pl.pallas_call(
        _s5_prep_kernel,
        out_shape=[jax.ShapeDtypeStruct((1, S5_LANES), F32)] * 2 + [jax.ShapeDtypeStruct(w_shape, BF16)] * 2,
        compiler_params=pltpu.CompilerParams(vmem_limit_bytes=VMEM_LIMIT_BYTES),
        name="s5_discretise",
    )(row(a_re), row(a_im), ldt, _s5_in_blockdiag(b_re), _s5_in_blockdiag(b_im))


def _s5_in_blockdiag(b):
    gpc = S5_GROUPS // S5_CHUNKS
    bt = jnp.transpose(b, (0, 2, 1)).reshape(S5_CHUNKS, gpc, S5_GROUP, S5_STATE)
    eye = jnp.eye(gpc, dtype=b.dtype)
    bd = bt[:, :, :, None, :] * eye[None, :, None, :, None]
    return bd.reshape(S5_CHUNKS, S5_CHUNK_IN, S5_CHUNK_ST)


def _s5_out_blockdiag(c):
    gpc = S5_GROUPS // S5_CHUNKS
    ct = jnp.transpose(c, (0, 2, 1)).reshape(S5_CHUNKS, gpc, S5_STATE, S5_GROUP)
    eye = jnp.eye(gpc, dtype=c.dtype)
    bd = ct[:, :, :, None, :] * eye[None, :, None, :, None]
    return bd.reshape(S5_CHUNKS, S5_CHUNK_ST, S5_CHUNK_IN)


def _s5_kernel(u_ref, h0r_ref, h0i_ref, lbr_ref, lbi_ref, wre_ref, wim_ref, cre_ref, cim_ref, d_ref, wglu_ref,
               bglu_ref, y_ref, hr_out, hi_out, re_s, im_s, str_s, sti_s, *, n_seq, steps):
    chunk = pl.program_id(0)

    @pl.when(chunk == 0)
    def _():
        str_s[...] = h0r_ref[...]
        sti_s[...] = h0i_ref[...]

    u = u_ref[...]
    for c in range(S5_CHUNKS):
        uc = u[:, c * S5_CHUNK_IN:(c + 1) * S5_CHUNK_IN]
        re_s[:, c * S5_CHUNK_ST:(c + 1) * S5_CHUNK_ST] = _dot_split(uc, wre_ref[0, c], wre_ref[1, c])
        im_s[:, c * S5_CHUNK_ST:(c + 1) * S5_CHUNK_ST] = _dot_split(uc, wim_ref[0, c], wim_ref[1, c])

    def seq_block(rb, carry):
        r0 = pl.multiple_of(rb * SUBLANES, SUBLANES)
        for c in range(S5_CHUNKS):
            lanes = slice(c * S5_CHUNK_ST, (c + 1) * S5_CHUNK_ST)
            lr = jnp.broadcast_to(lbr_ref[:, lanes], (SUBLANES, S5_CHUNK_ST))
            li = jnp.broadcast_to(lbi_ref[:, lanes], (SUBLANES, S5_CHUNK_ST))

            def step(t, h):
                hr, hi = h
                row = pl.multiple_of(t * n_seq + r0, SUBLANES)
                nr = lr * hr - li * hi + re_s[pl.ds(row, SUBLANES), lanes]
                ni = lr * hi + li * hr + im_s[pl.ds(row, SUBLANES), lanes]
                re_s[pl.ds(row, SUBLANES), lanes] = nr
                im_s[pl.ds(row, SUBLANES), lanes] = ni
                return nr, ni

            h0 = (str_s[pl.ds(r0, SUBLANES), lanes], sti_s[pl.ds(r0, SUBLANES), lanes])
            hr, hi = lax.fori_loop(0, steps, step, h0, unroll=min(steps, 8))
            str_s[pl.ds(r0, SUBLANES), lanes] = hr
            sti_s[pl.ds(r0, SUBLANES), lanes] = hi
        return carry

    lax.fori_loop(0, n_seq // SUBLANES, seq_block, 0)

    ys = []
    for c in range(S5_CHUNKS):
        lanes = slice(c * S5_CHUNK_ST, (c + 1) * S5_CHUNK_ST)
        ys.append(_dot(re_s[:, lanes].astype(BF16), cre_ref[c]) - _dot(im_s[:, lanes].astype(BF16), cim_ref[c]))
    y = jnp.concatenate(ys, axis=-1) + d_ref[...] * u
    y = _gelu(y)
    y_ref[...] = y * jax.nn.sigmoid(_dot(y.astype(BF16), wglu_ref[...]) + bglu_ref[...])

    @pl.when(chunk == pl.num_programs(0) - 1)
    def _():
        hr_out[...] = str_s[...]
        hi_out[...] = sti_s[...]


def _s5_call(u_rows, h0_re, h0_im, consts, n_seq, steps):
    rows = u_rows.shape[0]
    tile = n_seq * steps
    lb_re, lb_im, w_re, w_im, c_re, c_im, d_skip, w_glu_bf, b_glu = consts
    state = jax.ShapeDtypeStruct((n_seq, S5_LANES), F32)
    args = (u_rows, h0_re, h0_im, lb_re, lb_im, w_re, w_im, c_re, c_im, d_skip, w_glu_bf, b_glu)
    in_specs = [pl.BlockSpec((tile, S5_WIDTH), lambda i: (i, 0))] + [_full(a.shape) for a in args[1:]]
    return pl.pallas_call(
        functools.partial(_s5_kernel, n_seq=n_seq, steps=steps),
        grid=(rows // tile,),
        in_specs=in_specs,
        out_specs=[pl.BlockSpec((tile, S5_WIDTH), lambda i: (i, 0)), _full(state.shape), _full(state.shape)],
        out_shape=[jax.ShapeDtypeStruct((rows, S5_WIDTH), F32), state, state],
        scratch_shapes=[pltpu.VMEM((tile, S5_LANES), F32), pltpu.VMEM((tile, S5_LANES), F32),
                        pltpu.VMEM((n_seq, S5_LANES), F32), pltpu.VMEM((n_seq, S5_LANES), F32)],
        compiler_params=_params(("arbitrary",)),
        name="s5_mixer",
    )(*args)


def _head_sum(x, ones_ref):
    hi, lo = _split_bf16(x)
    return _dot(hi, ones_ref[...]) + _dot(lo, ones_ref[...])


def _rwkv_pre_kernel(p_ref, prev_ref, shift_ref, mu_ref, w0_ref, w2_ref, a0_ref, a2_ref, g2_ref, kk_ref, ka_ref,
                     ones_ref, r_out, w_out, k_out, v_out, kk_out, kka_out, g_out, *, n_seq):
    p = p_ref[...]
    tile = p.shape[0]
    head = jnp.where(pl.program_id(0) == 0, shift_ref[...], prev_ref[...])
    p_prev = head if tile == n_seq else jnp.concatenate([head, p[:tile - n_seq]], axis=0)
    ps = p + (p_prev - p) * mu_ref[...]
    w = RWKV_WIDTH
    r, k, v = ps[:, :w], ps[:, w:2 * w], ps[:, 2 * w:3 * w]
    lo = ps[:, 3 * w:3 * w + RWKV_LORA]
    g_lo = ps[:, 3 * w + RWKV_LORA:]
    w_raw = -jax.nn.softplus(-(w0_ref[...] + _dot_split(jnp.tanh(lo), w2_ref[0], w2_ref[1]))) - 0.5
    a = jax.nn.sigmoid(a0_ref[...] + _dot_split(lo, a2_ref[0], a2_ref[1]))
    kk = k * kk_ref[...]
    norm = jnp.sqrt(_head_sum(kk * kk, ones_ref))
    kk = kk / jnp.maximum(norm, 1e-12)
    r_out[...] = r
    w_out[...] = jnp.exp(-jnp.exp(w_raw))
    k_out[...] = k * (1.0 + (a - 1.0) * ka_ref[...])
    v_out[...] = v
    kk_out[...] = kk
    kka_out[...] = kk * a
    g_out[...] = _dot_split(jax.nn.sigmoid(g_lo), g2_ref[0], g2_ref[1])


def _rwkv_pre_call(p_rows, shift0, consts, n_seq, tile):
    rows = p_rows.shape[0]
    per = tile // n_seq
    vec = jax.ShapeDtypeStruct((rows, RWKV_WIDTH), F32)
    in_specs = [pl.BlockSpec((tile, RWKV_COLS), lambda i: (i, 0)),
                pl.BlockSpec((n_seq, RWKV_COLS), lambda i: (jnp.maximum(i * per - 1, 0), 0)),
                _full(shift0.shape)] + [_full(a.shape) for a in consts]
    return pl.pallas_call(
        functools.partial(_rwkv_pre_kernel, n_seq=n_seq),
        grid=(rows // tile,),
        in_specs=in_specs,
        out_specs=[pl.BlockSpec((tile, RWKV_WIDTH), lambda i: (i, 0))] * 7,
        out_shape=[vec] * 7,
        compiler_params=_params(("arbitrary",)),
        name="rwkv_prologue",
    )(p_rows, p_rows, shift0, *consts)


def _sublane_allsum(p):
    p = p + pltpu.roll(p, 4, 0)
    p = p + pltpu.roll(p, 2, 0)
    return p + pltpu.roll(p, 1, 0)


def _pairs_from_rows(x, nb):
    cols = [x[:, c * LANES:(c + 1) * LANES] for c in range(RWKV_WIDTH // LANES)]
    m = jnp.concatenate(cols * (LANES // (4 * nb)), axis=0)
    mt = m.T
    lane = lax.broadcasted_iota(jnp.int32, (RWKV_HEAD, LANES), 1)
    even_head = (lane // nb) % RWKV_HEADS < RWKV_HEADS // 2
    return jnp.where(even_head, mt[:RWKV_HEAD], pltpu.roll(mt[RWKV_HEAD:], 4 * nb, 1))


def _rows_from_pairs(y, nb):
    pairs = RWKV_HEADS * nb
    shifts = [(-(copy * pairs + parity * pairs // 2)) % LANES
              for parity in range(2) for copy in range(LANES // pairs)]
    mt = jnp.concatenate([y if s == 0 else pltpu.roll(y, s, 1) for s in shifts], axis=0)
    m = mt.T
    return jnp.concatenate([m[c * nb:(c + 1) * nb, :] for c in range(RWKV_WIDTH // LANES)], axis=1)


def _rwkv_scan_kernel(r_ref, w_ref, k_ref, kk_ref, kka_ref, v_ref, s0_ref, y_ref, sout_ref, s_s, sa_s, ka_s, kb_s,
                      va_s, vb_s, *, steps, n_v, nb):
    tc = pl.program_id(1)

    @pl.when(tc == 0)
    def _():
        s_s[...] = s0_ref[...]

    row_id = lax.broadcasted_iota(jnp.int32, (SUBLANES, LANES), 0)
    lane_id = lax.broadcasted_iota(jnp.int32, (n_v, LANES), 1)
    split = lambda x: x.reshape(RWKV_HEAD // SUBLANES, SUBLANES, LANES)
    k_refs = (r_ref, w_ref, k_ref, kk_ref, kka_ref)

    def stage(t, k_s, v_s):
        for i, ref in enumerate(k_refs):
            k_s[i] = _pairs_from_rows(ref[t], nb)
        full = _pairs_from_rows(v_ref[t], nb)
        v_s[...] = full if n_v == RWKV_HEAD else jnp.where(lane_id < LANES // 2, full[:n_v], full[n_v:])

    def advance(t, k_s, v_s):
        kk = split(k_s[3])
        for v in range(n_v):
            sa_s[v] = -_sublane_allsum(jnp.sum(split(s_s[v]) * kk, axis=0))
        r, w, k, kka = split(k_s[0]), split(k_s[1]), split(k_s[2]), split(k_s[4])
        y_tiles = []
        for vb in range(n_v // SUBLANES):
            v_tile = v_s[vb * SUBLANES:(vb + 1) * SUBLANES, :]
            y_tile = jnp.zeros((SUBLANES, LANES), F32)
            for j in range(SUBLANES):
                v = vb * SUBLANES + j
                v_row = jnp.broadcast_to(v_tile[j:j + 1, :], (SUBLANES, LANES))
                s = split(s_s[v]) * w + sa_s[v][None] * kka + v_row[None] * k
                s_s[v] = s.reshape(RWKV_HEAD, LANES)
                y_tile = jnp.where(row_id == j, _sublane_allsum(jnp.sum(s * r, axis=0)), y_tile)
            y_tiles.append(y_tile)
        y_ref[t] = _rows_from_pairs(jnp.concatenate(y_tiles, axis=0), nb)

    stage(0, ka_s, va_s)

    def two_steps(i, carry):
        t = 2 * i
        stage(t + 1, kb_s, vb_s)
        advance(t, ka_s, va_s)
        stage(jnp.minimum(t + 2, steps - 1), ka_s, va_s)
        advance(t + 1, kb_s, vb_s)
        return carry

    lax.fori_loop(0, steps // 2, two_steps, 0)

    @pl.when(tc == pl.num_programs(1) - 1)
    def _():
        sout_ref[...] = s_s[...]


def _rwkv_scan_call(r, w, k, kk, kka, v, s0, n_seq, steps):
    rows = r.shape[0]
    n_t = rows // n_seq
    n_v, _, lanes = s0.shape
    nb = n_seq // (lanes // LANES)
    assert steps % 2 == 0 and n_t % steps == 0
    as_steps = lambda a: a.reshape(n_t, n_seq, RWKV_WIDTH)
    xspec = pl.BlockSpec((steps, nb, RWKV_WIDTH), lambda g, i: (i, g, 0))
    sspec = pl.BlockSpec((n_v, RWKV_HEAD, LANES), lambda g, i: (0, 0, g))
    stage_k = pltpu.VMEM((5, RWKV_HEAD, LANES), F32)
    stage_v = pltpu.VMEM((n_v, LANES), F32)
    y, s = pl.pallas_call(
        functools.partial(_rwkv_scan_kernel, steps=steps, n_v=n_v, nb=nb),
        grid=(lanes // LANES, n_t // steps),
        in_specs=[xspec] * 6 + [sspec],
        out_specs=[xspec, sspec],
        out_shape=[jax.ShapeDtypeStruct((n_t, n_seq, RWKV_WIDTH), F32), jax.ShapeDtypeStruct(s0.shape, F32)],
        scratch_shapes=[pltpu.VMEM((n_v, RWKV_HEAD, LANES), F32), pltpu.VMEM((n_v, SUBLANES, LANES), F32),
                        stage_k, stage_k, stage_v, stage_v],
        compiler_params=_params(("arbitrary", "arbitrary")),
        name="rwkv_recurrence",
    )(*[as_steps(a) for a in (r, w, k, kk, kka, v)], s0)
    return y.reshape(rows, RWKV_WIDTH), s


def _state_to_pairs(s, nb):
    n_seq = s.shape[0]
    v_split = LANES // (RWKV_HEADS * nb)
    n_v = RWKV_HEAD // v_split
    s = s.reshape(n_seq // nb, nb, RWKV_HEADS // 2, 2, v_split, n_v, RWKV_HEAD)
    return jnp.transpose(s, (5, 6, 0, 4, 3, 2, 1)).reshape(n_v, RWKV_HEAD, n_seq // nb * LANES)


def _state_from_pairs(s, n_seq, nb):
    v_split = LANES // (RWKV_HEADS * nb)
    n_v = RWKV_HEAD // v_split
    s = s.reshape(n_v, RWKV_HEAD, n_seq // nb, v_split, 2, RWKV_HEADS // 2, nb)
    return jnp.transpose(s, (2, 6, 5, 4, 3, 0, 1)).reshape(n_seq, RWKV_HEADS, RWKV_HEAD, RWKV_HEAD)


def _mixout_kernel(x_ref, ys5_ref, yrw_ref, r_ref, k_ref, v_ref, g_ref, mod_ref, gnw_ref, gnb_ref, rk_ref,
                   ones_ref, wout_ref, n2g_ref, x1_ref, h2_ref, *, n_seq):
    y = yrw_ref[...]
    inv_n = 1.0 / RWKV_HEAD
    mean = _head_sum(y, ones_ref) * inv_n
    yc = y - mean
    var = _head_sum(yc * yc, ones_ref) * inv_n
    y = yc * lax.rsqrt(var + GN_EPS) * gnw_ref[...] + gnb_ref[...]
    v = v_ref[...]
    y = y + _head_sum(r_ref[...] * k_ref[...] * rk_ref[...], ones_ref) * v
    y = y * g_ref[...]
    mix = jnp.concatenate([ys5_ref[...], y], axis=-1).astype(BF16)
    x1 = _time_major_rows(x_ref) + _gated(_dot(mix, wout_ref[...]), mod_ref[2], n_seq)
    x1_ref[...] = x1
    h2_ref[...] = _modulate(_rms(x1, n2g_ref[...]), mod_ref[3], mod_ref[4], n_seq)


def _mixout_call(x, y_s5, y_rw, r, k, v, g, mod6, consts, n_seq, tile):
    d = x.shape[-1]
    rows = x.size // d
    wide = pl.BlockSpec((tile, d), lambda i: (i, 0))
    half = pl.BlockSpec((tile, RWKV_WIDTH), lambda i: (i, 0))
    return pl.pallas_call(
        functools.partial(_mixout_kernel, n_seq=n_seq),
        grid=(rows // tile,),
        in_specs=[_token_spec(x, n_seq, tile)] + [half] * 6 + [_full(mod6.shape)] + [_full(a.shape) for a in consts],
        out_specs=[wide, wide],
        out_shape=[jax.ShapeDtypeStruct((rows, d), F32)] * 2,
        compiler_params=_params(("arbitrary",)),
        name="mixer_out_norm2",
    )(x, y_s5, y_rw, r, k, v, g, mod6, *consts)


def _sort16_pairs():
    pairs = []
    k = 2
    while k <= PEER_TOPK:
        j = k // 2
        while j >= 1:
            pairs += [(i, i ^ j, (i & k) == 0) for i in range(PEER_TOPK) if i ^ j > i]
            j //= 2
        k *= 2
    return pairs


def _bitonic_merge_desc(x):
    x = list(x)
    j = PEER_TOPK // 2
    while j >= 1:
        for i in range(PEER_TOPK):
            if not i & j:
                x[i], x[i | j] = jnp.maximum(x[i], x[i | j]), jnp.minimum(x[i], x[i | j])
        j //= 2
    return x


def _top16_desc(slabs):
    x = list(slabs)
    for i, l, i_max in _sort16_pairs():
        hi, lo = jnp.maximum(x[i], x[l]), jnp.minimum(x[i], x[l])
        x[i], x[l] = (hi, lo) if i_max else (lo, hi)
    return _merge_sublane_lists(x)


def _merge_top16(x, y):
    return _bitonic_merge_desc([jnp.maximum(x[v], y[PEER_TOPK - 1 - v]) for v in range(PEER_TOPK)])


def _merge_sublane_lists(x):
    for shift in (1, 2, 4):
        x = _merge_top16(x, [pltpu.roll(v, shift, 0) for v in x])
    return x


def _top16_pair_sums(a, b):
    sub = lax.broadcasted_iota(jnp.int32, a[0].shape, 0)
    b_lo = b[0]
    for s in range(1, SUBLANES):
        b_lo = jnp.where(sub == s, b[s], b_lo)
    low = [a[i] + b_lo for i in range(PEER_TOPK)]
    high = [a[0] + b[j] for j in range(SUBLANES, PEER_TOPK)]
    padded = high + [jnp.full_like(a[0], -jnp.inf)] * (PEER_TOPK - len(high))
    return _merge_top16(_merge_sublane_lists(low), padded), low, high


def _count_at_least(slabs, bound):
    count = jnp.zeros_like(bound)
    for s in slabs:
        count = count + jnp.where(s >= bound, 1.0, 0.0)
    return _sublane_allsum(count)


def _peer_select(xt, wq_ref, k1_ref, k2_ref, q_s, a_s, b_s, e1_s, e2_s, thr_s, tie_s):
    tg = xt.shape[1]
    q_s[...] = _dot_split_lhs(wq_ref[0], wq_ref[1], xt).reshape(PEER_HEADS, 2 * PEER_HALF, tg)
    slabs = lambda s: [s[v * SUBLANES:(v + 1) * SUBLANES, :] for v in range(N_KEYS // SUBLANES)]
    tiled = lambda row: jnp.concatenate([row] * (N_KEYS // SUBLANES), axis=0)

    def head(h, carry):
        q = q_s[h]
        s1 = _dot(k1_ref[h], q[:PEER_HALF], HIGHEST)
        s2 = _dot(k2_ref[h], q[PEER_HALF:], HIGHEST)
        top1 = _top16_desc(slabs(s1))
        top2 = _top16_desc(slabs(s2))
        topc, sums_low, sums_high = _top16_pair_sums(top1, top2)
        z = jnp.ones_like(topc[0])
        for i in range(1, PEER_TOPK):
            z = z + jnp.exp(topc[i] - topc[0])
        a = jnp.where(s1 >= tiled(top1[-1]), s1, -jnp.inf)
        b = jnp.where(s2 >= tiled(top2[-1]), s2, -jnp.inf)
        a_s[h] = a
        b_s[h] = b
        e1_s[h] = jnp.exp(a - tiled(top1[0])) / tiled(z)
        e2_s[h] = jnp.exp(b - tiled(top2[0]))
        thr_s[h] = topc[-1]
        n_pairs = _count_at_least(sums_low, topc[-1])
        for e in sums_high:
            n_pairs = n_pairs + jnp.where(e >= topc[-1], 1.0, 0.0)
        tied = jnp.where(n_pairs == PEER_TOPK, 0.0, 1.0)
        for values, bound in ((slabs(s1), top1[-1]), (slabs(s2), top2[-1])):
            tied = jnp.maximum(tied, jnp.where(_count_at_least(values, bound) == PEER_TOPK, 0.0, 1.0))
        tie_s[...] = jnp.maximum(tie_s[...], tied)
        return carry

    lax.fori_loop(0, PEER_HEADS, head, 0)


def _stable_top16(x, row_id):
    vals, rows = [], []
    for _ in range(PEER_TOPK):
        m = jnp.max(x, axis=0, keepdims=True)
        r = jnp.min(jnp.where(x == m, row_id, float(x.shape[0])), axis=0, keepdims=True)
        vals.append(m)
        rows.append(r)
        x = jnp.where(row_id == r, -jnp.inf, x)
    return vals, rows


def _peer_select_exact(k1_ref, k2_ref, q_s, a_s, b_s, e1_s, e2_s, thr_s, pa_s, pb_s, pthr_s, lst_s, cand_s):
    tg = q_s.shape[2]
    key_id = lax.broadcasted_iota(jnp.int32, (N_KEYS, tg), 0).astype(F32)
    pair_id = lax.broadcasted_iota(jnp.int32, (PEER_TOPK * PEER_TOPK, tg), 0).astype(F32)

    def head(h, carry):
        q = q_s[h]
        s1 = _dot(k1_ref[h], q[:PEER_HALF], HIGHEST)
        s2 = _dot(k2_ref[h], q[PEER_HALF:], HIGHEST)
        v1, i1 = _stable_top16(s1, key_id)
        v2, i2 = _stable_top16(s2, key_id)
        rank1 = jnp.full((N_KEYS, tg), float(PEER_TOPK), F32)
        rank2 = rank1
        for r in range(PEER_TOPK):
            rank1 = jnp.where(key_id == i1[r], float(r), rank1)
            rank2 = jnp.where(key_id == i2[r], float(r), rank2)
            lst_s[r:r + 1, :] = v2[r]
        best2 = lst_s[...]
        for r in range(PEER_TOPK):
            cand_s[r * PEER_TOPK:(r + 1) * PEER_TOPK, :] = v1[r] + best2
        vc, pc = _stable_top16(cand_s[...], pair_id)
        z = jnp.ones_like(vc[0])
        for r in range(1, PEER_TOPK):
            z = z + jnp.exp(vc[r] - vc[0])
        a = jnp.where(rank1 < PEER_TOPK, s1, -jnp.inf)
        b = jnp.where(rank2 < PEER_TOPK, s2, -jnp.inf)
        a_s[h] = a
        b_s[h] = b
        e1_s[h] = jnp.exp(a - v1[0]) / z
        e2_s[h] = jnp.exp(b - v2[0])
        pa_s[h] = rank1 * float(PEER_TOPK)
        pb_s[h] = rank2
        thr_s[h] = jnp.broadcast_to(vc[-1], (SUBLANES, tg))
        pthr_s[h] = jnp.broadcast_to(pc[-1], (SUBLANES, tg))
        return carry

    lax.fori_loop(0, PEER_HEADS, head, 0)


def _peer_gates(g, key0, act_ref, pt_ref, a_s, b_s, e1_s, e2_s, thr_s):
    tg = act_ref.shape[2]
    for lg in range(tg // LANES):
        ls = slice(lg * LANES, (lg + 1) * LANES)
        a_rows = [a_s[g, h, pl.ds(key0, PEER_KEYS_PER_BLOCK), ls] for h in range(PEER_HEADS)]
        e_rows = [e1_s[g, h, pl.ds(key0, PEER_KEYS_PER_BLOCK), ls] for h in range(PEER_HEADS)]
        for i in range(PEER_KEYS_PER_BLOCK):
            gate = jnp.zeros((N_KEYS, LANES), F32)
            for h in range(PEER_HEADS):
                score = b_s[g, h, :, ls] + a_rows[h][i:i + 1, :]
                weight = e2_s[g, h, :, ls] * e_rows[h][i:i + 1, :]
                gate = gate + jnp.where(score >= thr_s[g, h, 0:1, ls], weight, 0.0)
            act = _gelu(act_ref[g, i * N_KEYS:(i + 1) * N_KEYS, ls])
            pt_ref[g, i * N_KEYS:(i + 1) * N_KEYS, ls] = (gate * act).astype(BF16)


def _peer_gates_exact(g, key0, act_ref, pt_ref, a_s, b_s, e1_s, e2_s, thr_s, pa_s, pb_s, pthr_s):
    tg = act_ref.shape[2]
    rows = pl.ds(key0, PEER_KEYS_PER_BLOCK)
    for lg in range(tg // LANES):
        ls = slice(lg * LANES, (lg + 1) * LANES)
        for i in range(PEER_KEYS_PER_BLOCK):
            gate = jnp.zeros((N_KEYS, LANES), F32)
            for h in range(PEER_HEADS):
                score = b_s[g, h, :, ls] + a_s[g, h, rows, ls][i:i + 1, :]
                pos = pb_s[g, h, :, ls] + pa_s[g, h, rows, ls][i:i + 1, :]
                weight = e2_s[g, h, :, ls] * e1_s[g, h, rows, ls][i:i + 1, :]
                thr = thr_s[g, h, 0:1, ls]
                tied_in = jnp.where(score == thr, jnp.where(pos <= pthr_s[g, h, 0:1, ls], weight, 0.0), 0.0)
                gate = gate + jnp.where(score > thr, weight, tied_in)
            act = _gelu(act_ref[g, i * N_KEYS:(i + 1) * N_KEYS, ls])
            pt_ref[g, i * N_KEYS:(i + 1) * N_KEYS, ls] = (gate * act).astype(BF16)


def _peer_kernel(h2_ref, x1_ref, ga_ref, wq_ref, k1_ref, k2_ref, u_ref, vt_ref, fg_ref, o_ref,
                 xt_s, q_s, a_s, b_s, e1_s, e2_s, thr_s, pa_s, pb_s, pthr_s, tie_s, lst_s, cand_s, tied_s,
                 act0_s, act1_s, pt0_s, pt1_s, acc_s, *, n_seq, n_blocks):
    s = pl.program_id(1)
    n_groups = xt_s.shape[0]
    tg = xt_s.shape[2]
    acts, pts = (act0_s, act1_s), (pt0_s, pt1_s)
    sel = (a_s, b_s, e1_s, e2_s, thr_s)
    sel_exact = sel + (pa_s, pb_s, pthr_s)

    def stages(parity, first, gates, second, exact=False):
        key0 = pl.multiple_of((s - 1) * PEER_KEYS_PER_BLOCK, PEER_KEYS_PER_BLOCK)

        def group(g, carry):
            if first:
                acts[parity][g] = _dot(u_ref[...], xt_s[g])
            if gates and exact:
                _peer_gates_exact(g, key0, acts[1 - parity], pts[1 - parity], *sel_exact)
            elif gates:
                _peer_gates(g, key0, acts[1 - parity], pts[1 - parity], *sel)
            if second:
                acc_s[g] += _dot(vt_ref[...], pts[parity][g])
            return carry

        lax.fori_loop(0, n_groups, group, 0)

    @pl.when(s == 0)
    def _():
        tie_s[...] = jnp.zeros_like(tie_s)

        def prep(g, carry):
            t0 = pl.multiple_of(g * tg, tg)
            xt = h2_ref[pl.ds(t0, tg), :].T
            xt_s[g] = xt.astype(BF16)
            _peer_select(xt, wq_ref, k1_ref, k2_ref, q_s, a_s.at[g], b_s.at[g], e1_s.at[g], e2_s.at[g],
                         thr_s.at[g], tie_s)
            return carry

        lax.fori_loop(0, n_groups, prep, 0)
        tied_s[0] = (jnp.max(tie_s[...]) > 0.0).astype(jnp.int32)

        @pl.when(tied_s[0] != 0)
        def _():
            def prep_exact(g, carry):
                t0 = pl.multiple_of(g * tg, tg)
                xt = h2_ref[pl.ds(t0, tg), :].T
                q_s[...] = _dot_split_lhs(wq_ref[0], wq_ref[1], xt).reshape(q_s.shape)
                _peer_select_exact(k1_ref, k2_ref, q_s, a_s.at[g], b_s.at[g], e1_s.at[g], e2_s.at[g], thr_s.at[g],
                                   pa_s.at[g], pb_s.at[g], pthr_s.at[g], lst_s, cand_s)
                return carry

            lax.fori_loop(0, n_groups, prep_exact, 0)

        acc_s[...] = jnp.zeros_like(acc_s)
        pt1_s[...] = jnp.zeros_like(pt1_s)
        stages(0, True, False, False)

    for exact in (False, True):
        for parity in (0, 1):
            @pl.when((s >= 1) & (s < n_blocks) & (s % 2 == parity) & ((tied_s[0] != 0) == exact))
            def _(parity=parity, exact=exact):
                stages(parity, True, True, True, exact)

        @pl.when((s == n_blocks) & ((tied_s[0] != 0) == exact))
        def _(exact=exact):
            stages(n_blocks % 2, False, True, True, exact)

    @pl.when(s == n_blocks + 1)
    def _():
        stages((n_blocks + 1) % 2, False, False, True)
        for g in range(n_groups):
            rows = slice(g * tg, (g + 1) * tg)
            x2 = x1_ref[rows, :] + _gated(acc_s[g].T, ga_ref[...], min(n_seq, tg))
            y = _rms(x2, fg_ref[...])
            if len(o_ref.shape) == 2:
                o_ref[rows, :] = y
            else:
                sg = tg // n_seq
                o_ref[:, g * sg:(g + 1) * sg, :] = jnp.swapaxes(y.reshape(sg, n_seq, y.shape[1]), 0, 1)


def _peer_call(h2, x1, gate2, wq_t, keys1, keys2, u_bf, vt_bf, final_g, n_seq, tm, batch_major):
    rows, d = h2.shape
    n_exp = u_bf.shape[0]
    eb = PEER_KEYS_PER_BLOCK * N_KEYS
    n_blocks = n_exp // eb
    tg = PEER_TOKEN_GROUP
    n_groups = tm // tg
    assert tm % tg == 0 and (tg % n_seq == 0 or n_seq % tg == 0) and n_exp % eb == 0 and n_blocks >= 2
    if n_seq > tg:
        raise NotImplementedError("more than PEER_TOKEN_GROUP sequences per step")
    tok = pl.BlockSpec((tm, d), lambda i, s: (i, 0))
    out = jax.ShapeDtypeStruct((n_seq, rows // n_seq, d) if batch_major else (rows, d), F32)
    sel = pltpu.VMEM((n_groups, PEER_HEADS, N_KEYS, tg), F32)
    row = pltpu.VMEM((n_groups, PEER_HEADS, SUBLANES, tg), F32)
    act = pltpu.VMEM((n_groups, eb, tg), F32)
    pt = pltpu.VMEM((n_groups, eb, tg), BF16)
    return pl.pallas_call(
        functools.partial(_peer_kernel, n_seq=n_seq, n_blocks=n_blocks),
        grid=(rows // tm, n_blocks + 2),
        in_specs=[tok, tok, _full(gate2.shape), _full(wq_t.shape), _full(keys1.shape), _full(keys2.shape),
                  pl.BlockSpec((eb, d), lambda i, s: (jnp.minimum(s, n_blocks - 1), 0)),
                  pl.BlockSpec((d, eb), lambda i, s: (0, jnp.clip(s - 2, 0, n_blocks - 1))),
                  _full((1, d))],
        out_specs=_token_spec(out, n_seq, tm),
        out_shape=out,
        scratch_shapes=[pltpu.VMEM((n_groups, d, tg), BF16), pltpu.VMEM((PEER_HEADS, 2 * PEER_HALF, tg), F32),
                        sel, sel, sel, sel, row, sel, sel, row, pltpu.VMEM((SUBLANES, tg), F32),
                        pltpu.VMEM((PEER_TOPK, tg), F32), pltpu.VMEM((PEER_TOPK * PEER_TOPK, tg), F32),
                        pltpu.SMEM((1,), jnp.int32),
                        act, act, pt, pt, pltpu.VMEM((n_groups, d, tg), F32)],
        compiler_params=_params(("arbitrary", "arbitrary")),
        name="peer_final_norm",
    )(h2, x1, gate2, wq_t, keys1, keys2, u_bf, vt_bf, final_g.reshape(1, d))


def _pick(n, target):
    t = min(n, target)
    while n % t:
        t -= 1
    return t


def _layer(x_btd, mod, s5_re0, s5_im0, wkv0, shift0, prm):
    n_seq, n_t, d = x_btd.shape
    rows = n_seq * n_t
    mod6 = jnp.transpose(mod.reshape(n_seq, 6, d), (1, 0, 2))
    tile = n_seq * _pick(n_t, max(1, 512 // n_seq))
    reorder_in_kernel = (PEER_TOKEN_GROUP // n_seq) % SUBLANES == 0 and (tile // n_seq) % SUBLANES == 0
    x_in = x_btd if reorder_in_kernel else jnp.transpose(x_btd, (1, 0, 2)).reshape(rows, d)

    u_rows, p_rows = _inproj_call(x_in, mod6, prm["norm1_g"], prm["w_in_bf"], n_seq, tile)

    s5_steps = _pick(n_t, max(1, 512 // n_seq))
    y_s5, s5_re, s5_im = _s5_call(u_rows, s5_re0.reshape(n_seq, S5_LANES), s5_im0.reshape(n_seq, S5_LANES),
                                  prm["s5_consts"], n_seq, s5_steps)

    r, w, k, v, kk, kka, g = _rwkv_pre_call(p_rows, shift0, prm["rwkv_pre_consts"], n_seq, tile)
    nb = min(n_seq, LANES // RWKV_HEADS)
    y_rw, s_pairs = _rwkv_scan_call(r, w, k, kk, kka, v, _state_to_pairs(wkv0, nb), n_seq, _pick(n_t, 16))
    wkv = _state_from_pairs(s_pairs, n_seq, nb)

    x1, h2 = _mixout_call(x_in, y_s5, y_rw, r, k, v, g, mod6, prm["mixout_consts"], n_seq, tile)

    y = _peer_call(h2, x1, mod6[5], prm["peer_wq_t"], prm["peer_keys1"], prm["peer_keys2"], prm["peer_u_bf"],
                   prm["peer_vt_bf"], prm["final_norm_g"], n_seq, tile, reorder_in_kernel)
    if not reorder_in_kernel:
        y = jnp.transpose(y.reshape(n_t, n_seq, d), (1, 0, 2))
    shift = p_rows[rows - n_seq:]
    return (y, s5_re.reshape(n_seq, S5_GROUPS, S5_STATE), s5_im.reshape(n_seq, S5_GROUPS, S5_STATE), wkv, shift)


def kernel(x_prompt, x_sample, state_s5_re, state_s5_im, state_wkv, state_shift, c_prompt, c_sample, w_ada, b_ada, norm1_g, norm2_g, w_in, w_out, s5_a_re, s5_a_im, s5_log_dt, s5_b_re, s5_b_im, s5_c_re, s5_c_im, s5_d, w_glu, b_glu, rwkv_mu, rwkv_w0, rwkv_w2, rwkv_a0, rwkv_a2, rwkv_g2, rwkv_k_k, rwkv_k_a, rwkv_r_k, rwkv_gn_w, rwkv_gn_b, peer_w_q, peer_keys1, peer_keys2, peer_u, peer_v, final_norm_g):
    assert w_ada.shape[0] == 1, "single-layer model"
    nbp = x_prompt.shape[0]
    row = lambda a: a.reshape(1, -1)

    mod = _ada_call(jnp.concatenate([c_prompt, c_sample], axis=0).astype(F32), w_ada[0], b_ada[0])

    lb_re, lb_im, bw_re, bw_im = _s5_prep_call(s5_a_re[0], s5_a_im[0], s5_log_dt[0], s5_b_re[0], s5_b_im[0])
    head_ones = jnp.kron(jnp.eye(RWKV_HEADS, dtype=BF16), jnp.ones((RWKV_HEAD, RWKV_HEAD), BF16))
    lora_pad = jnp.zeros((RWKV_LORA // 2, RWKV_WIDTH), F32)
    prm = {
        "norm1_g": norm1_g[0],
        "w_in_bf": w_in[0].astype(BF16),
        "s5_consts": (lb_re, lb_im, bw_re, bw_im, _s5_out_blockdiag(s5_c_re[0]).astype(BF16),
                      _s5_out_blockdiag(s5_c_im[0]).astype(BF16), row(s5_d[0]), w_glu[0].astype(BF16),
                      row(b_glu[0])),
        "rwkv_pre_consts": (row(rwkv_mu[0]), row(rwkv_w0[0]),
                            _split_call(jnp.concatenate([rwkv_w2[0], lora_pad], axis=0)), row(rwkv_a0[0]),
                            _split_call(jnp.concatenate([lora_pad, rwkv_a2[0]], axis=0)), _split_call(rwkv_g2[0]),
                            row(rwkv_k_k[0]), row(rwkv_k_a[0]), head_ones),
        "mixout_consts": (row(rwkv_gn_w[0]), row(rwkv_gn_b[0]), row(rwkv_r_k[0]), head_ones,
                          w_out[0].astype(BF16), row(norm2_g[0])),
        "peer_wq_t": _split_call(peer_w_q[0].T),
        "peer_keys1": peer_keys1[0],
        "peer_keys2": peer_keys2[0],
        "peer_u_bf": peer_u[0].astype(BF16),
        "peer_vt_bf": peer_v[0].T.astype(BF16),
        "final_norm_g": final_norm_g,
    }

    z_s5 = jnp.zeros((nbp, S5_GROUPS, S5_STATE), F32)
    z_wkv = jnp.zeros((nbp, RWKV_HEADS, RWKV_HEAD, RWKV_HEAD), F32)
    z_sh = jnp.zeros((nbp, RWKV_COLS), F32)
    yp, pr, pi, pw, psh = _layer(x_prompt.astype(F32), mod[:nbp], z_s5, z_s5, z_wkv, z_sh, prm)
    ys, sr, si, sw, ssh = _layer(x_sample.astype(F32), mod[nbp:], state_s5_re[0].astype(F32),
                                 state_s5_im[0].astype(F32), state_wkv[0].astype(F32),
                                 state_shift[0].astype(F32), prm)
    return (yp.astype(x_prompt.dtype), ys.astype(x_sample.dtype), pr[None], pi[None], pw[None], psh[None],
            sr[None], si[None], sw[None], ssh[None])
```

```python
import functools

import jax
import jax.numpy as jnp
from jax import lax
from jax.experimental import pallas as pl
from jax.experimental.pallas import tpu as pltpu

F32 = jnp.float32
BF16 = jnp.bfloat16
HIGHEST = lax.Precision.HIGHEST

LANES = 128
SUBLANES = 8
VMEM_LIMIT_BYTES = 56 * 1024 * 1024

D_MODEL = 1024
S5_WIDTH = 512
S5_GROUP = 16
S5_GROUPS = 32
S5_STATE = 64
S5_LANES = S5_GROUPS * S5_STATE
S5_CHUNKS = 4
S5_CHUNK_IN = S5_WIDTH // S5_CHUNKS
S5_CHUNK_ST = S5_LANES // S5_CHUNKS
RWKV_WIDTH = 512
RWKV_HEAD = 64
RWKV_HEADS = 8
RWKV_LORA = 128
RWKV_COLS = 3 * RWKV_WIDTH + 64 + 64 + 128
PEER_HEADS = 8
N_KEYS = 128
PEER_TOPK = 16
PEER_HALF = 64
PEER_TOKEN_GROUP = 256
PEER_KEYS_PER_BLOCK = 8
NORM_EPS = 1e-6
GN_EPS = 64e-5


def _params(sem):
    return pltpu.CompilerParams(dimension_semantics=sem, vmem_limit_bytes=VMEM_LIMIT_BYTES)


def _full(shape):
    return pl.BlockSpec(shape, lambda *_: (0,) * len(shape))


def _dot(a, b, precision=None):
    return jnp.dot(a, b, precision=precision, preferred_element_type=F32)


def _split_bf16(x):
    hi = x.astype(BF16)
    return hi, (x - hi.astype(F32)).astype(BF16)


def _dot_split(a, b_hi, b_lo):
    a_hi, a_lo = _split_bf16(a)
    return _dot(a_hi, b_hi) + (_dot(a_lo, b_hi) + _dot(a_hi, b_lo))


def _dot_split_lhs(a_hi, a_lo, b):
    b_hi, b_lo = _split_bf16(b)
    return _dot(a_hi, b_hi) + (_dot(a_lo, b_hi) + _dot(a_hi, b_lo))


def _gelu(x):
    return 0.5 * x * (1.0 + lax.erf(x * (2.0 ** -0.5)))


def _rms(x, g):
    return x * lax.rsqrt(jnp.mean(x * x, axis=-1, keepdims=True) + NORM_EPS) * g


def _modulate(h, shift, scale, n_seq):
    rows, d = h.shape
    h3 = h.reshape(rows // n_seq, n_seq, d)
    return (h3 * (1.0 + scale)[None] + shift[None]).reshape(rows, d)


def _gated(h, gate, n_seq):
    rows, d = h.shape
    return (h.reshape(rows // n_seq, n_seq, d) * gate[None]).reshape(rows, d)


def _split_kernel(w_ref, o_ref):
    o_ref[0], o_ref[1] = _split_bf16(w_ref[...])


def _split_call(w):
    return pl.pallas_call(
        _split_kernel,
        out_shape=jax.ShapeDtypeStruct((2,) + w.shape, BF16),
        compiler_params=pltpu.CompilerParams(vmem_limit_bytes=VMEM_LIMIT_BYTES),
        name="split_weight",
    )(w)


def _ada_kernel(c_ref, w_ref, b_ref, o_ref):
    s = jax.nn.silu(c_ref[...])
    o_ref[...] = _dot(s, w_ref[...], HIGHEST) + b_ref[...]


def _ada_call(c_all, w_ada, b_ada):
    n, d = c_all.shape
    cols = w_ada.shape[1]
    return pl.pallas_call(
        _ada_kernel,
        grid=(cols // d,),
        in_specs=[_full((n, d)), pl.BlockSpec((d, d), lambda j: (0, j)), pl.BlockSpec((1, d), lambda j: (0, j))],
        out_specs=pl.BlockSpec((n, d), lambda j: (0, j)),
        out_shape=jax.ShapeDtypeStruct((n, cols), F32),
        compiler_params=_params(("arbitrary",)),
        name="adaln_mod",
    )(c_all, w_ada, b_ada.reshape(1, cols))


def _time_major_rows(x_ref):
    if len(x_ref.shape) == 2:
        return x_ref[...]
    n_seq, steps, d = x_ref.shape
    return jnp.swapaxes(x_ref[...], 0, 1).reshape(n_seq * steps, d)


def _token_spec(x, n_seq, tile):
    if x.ndim == 2:
        return pl.BlockSpec((tile, x.shape[1]), lambda i, *_: (i, 0))
    return pl.BlockSpec((n_seq, tile // n_seq, x.shape[2]), lambda i, *_: (0, i, 0))


def _inproj_kernel(x_ref, mod_ref, g_ref, w_ref, u_ref, p_ref, *, n_seq):
    h = _modulate(_rms(_time_major_rows(x_ref), g_ref[...]), mod_ref[0], mod_ref[1], n_seq)
    proj = _dot(h.astype(BF16), w_ref[...])
    u_ref[...] = proj[:, :S5_WIDTH]
    p_ref[...] = proj[:, S5_WIDTH:]


def _inproj_call(x, mod6, norm_g, w_in_bf, n_seq, tile):
    d = x.shape[-1]
    rows = x.size // d
    return pl.pallas_call(
        functools.partial(_inproj_kernel, n_seq=n_seq),
        grid=(rows // tile,),
        in_specs=[_token_spec(x, n_seq, tile), _full(mod6.shape), _full((1, d)),
                  _full(w_in_bf.shape)],
        out_specs=[pl.BlockSpec((tile, S5_WIDTH), lambda i: (i, 0)),
                   pl.BlockSpec((tile, RWKV_COLS), lambda i: (i, 0))],
        out_shape=[jax.ShapeDtypeStruct((rows, S5_WIDTH), F32), jax.ShapeDtypeStruct((rows, RWKV_COLS), F32)],
        compiler_params=_params(("arbitrary",)),
        name="norm1_inproj",
    )(x, mod6, norm_g.reshape(1, d), w_in_bf)


def _s5_prep_kernel(are_ref, aim_ref, ldt_ref, bre_ref, bim_ref, lbr_ref, lbi_ref, wre_ref, wim_ref):
    lam_re, lam_im = are_ref[...], aim_ref[...]
    dt = jnp.exp(ldt_ref[...])
    mag = jnp.exp(lam_re * dt)
    ang = lam_im * dt
    lb_re, lb_im = mag * jnp.cos(ang), mag * jnp.sin(ang)
    den = lam_re * lam_re + lam_im * lam_im
    n_re, n_im = lb_re - 1.0, lb_im
    coef_re = (n_re * lam_re + n_im * lam_im) / den
    coef_im = (n_im * lam_re - n_re * lam_im) / den
    lbr_ref[...] = lb_re
    lbi_ref[...] = lb_im
    for c in range(S5_CHUNKS):
        cr = coef_re[:, c * S5_CHUNK_ST:(c + 1) * S5_CHUNK_ST]
        ci = coef_im[:, c * S5_CHUNK_ST:(c + 1) * S5_CHUNK_ST]
        wre_ref[0, c], wre_ref[1, c] = _split_bf16(cr * bre_ref[c] - ci * bim_ref[c])
        wim_ref[0, c], wim_ref[1, c] = _split_bf16(cr * bim_ref[c] + ci * bre_ref[c])


def _s5_prep_call(a_re, a_im, log_dt, b_re, b_im):
    row = lambda a: a.reshape(1, S5_LANES)
    ldt = jnp.repeat(log_dt, S5_STATE).reshape(1, S5_LANES)
    w_shape = (2, S5_CHUNKS, S5_CHUNK_IN, S5_CHUNK_ST)
    return pl.pallas_call(
        _s5_prep_kernel,
        out_shape=[jax.ShapeDtypeStruct((1, S5_LANES), F32)] * 2 + [jax.ShapeDtypeStruct(w_shape, BF16)] * 2,
        compiler_params=pltpu.CompilerParams(vmem_limit_bytes=VMEM_LIMIT_BYTES),
        name="s5_discretise",
    )(row(a_re), row(a_im), ldt, _s5_in_blockdiag(b_re), _s5_in_blockdiag(b_im))


def _s5_in_blockdiag(b):
    gpc = S5_GROUPS // S5_CHUNKS
    bt = jnp.transpose(b, (0, 2, 1)).reshape(S5_CHUNKS, gpc, S5_GROUP, S5_STATE)
    eye = jnp.eye(gpc, dtype=b.dtype)
    bd = bt[:, :, :, None, :] * eye[None, :, None, :, None]
    return bd.reshape(S5_CHUNKS, S5_CHUNK_IN, S5_CHUNK_ST)


def _s5_out_blockdiag(c):
    gpc = S5_GROUPS // S5_CHUNKS
    ct = jnp.transpose(c, (0, 2, 1)).reshape(S5_CHUNKS, gpc, S5_STATE, S5_GROUP)
    eye = jnp.eye(gpc, dtype=c.dtype)
    bd = ct[:, :, :, None, :] * eye[None, :, None, :, None]
    return bd.reshape(S5_CHUNKS, S5_CHUNK_ST, S5_CHUNK_IN)


def _s5_kernel(u_ref, h0r_ref, h0i_ref, lbr_ref, lbi_ref, wre_ref, wim_ref, cre_ref, cim_ref, d_ref, wglu_ref,
               bglu_ref, y_ref, hr_out, hi_out, re_s, im_s, str_s, sti_s, *, n_seq, steps):
    chunk = pl.program_id(0)

    @pl.when(chunk == 0)
    def _():
        str_s[...] = h0r_ref[...]
        sti_s[...] = h0i_ref[...]

    u = u_ref[...]
    for c in range(S5_CHUNKS):
        uc = u[:, c * S5_CHUNK_IN:(c + 1) * S5_CHUNK_IN]
        re_s[:, c * S5_CHUNK_ST:(c + 1) * S5_CHUNK_ST] = _dot_split(uc, wre_ref[0, c], wre_ref[1, c])
        im_s[:, c * S5_CHUNK_ST:(c + 1) * S5_CHUNK_ST] = _dot_split(uc, wim_ref[0, c], wim_ref[1, c])

    def seq_block(rb, carry):
        r0 = pl.multiple_of(rb * SUBLANES, SUBLANES)
        for c in range(S5_CHUNKS):
            lanes = slice(c * S5_CHUNK_ST, (c + 1) * S5_CHUNK_ST)
            lr = jnp.broadcast_to(lbr_ref[:, lanes], (SUBLANES, S5_CHUNK_ST))
            li = jnp.broadcast_to(lbi_ref[:, lanes], (SUBLANES, S5_CHUNK_ST))

            def step(t, h):
                hr, hi = h
                row = pl.multiple_of(t * n_seq + r0, SUBLANES)
                nr = lr * hr - li * hi + re_s[pl.ds(row, SUBLANES), lanes]
                ni = lr * hi + li * hr + im_s[pl.ds(row, SUBLANES), lanes]
                re_s[pl.ds(row, SUBLANES), lanes] = nr
                im_s[pl.ds(row, SUBLANES), lanes] = ni
                return nr, ni

            h0 = (str_s[pl.ds(r0, SUBLANES), lanes], sti_s[pl.ds(r0, SUBLANES), lanes])
            hr, hi = lax.fori_loop(0, steps, step, h0, unroll=min(steps, 8))
            str_s[pl.ds(r0, SUBLANES), lanes] = hr
            sti_s[pl.ds(r0, SUBLANES), lanes] = hi
        return carry

    lax.fori_loop(0, n_seq // SUBLANES, seq_block, 0)

    ys = []
    for c in range(S5_CHUNKS):
        lanes = slice(c * S5_CHUNK_ST, (c + 1) * S5_CHUNK_ST)
        ys.append(_dot(re_s[:, lanes].astype(BF16), cre_ref[c]) - _dot(im_s[:, lanes].astype(BF16), cim_ref[c]))
    y = jnp.concatenate(ys, axis=-1) + d_ref[...] * u
    y = _gelu(y)
    y_ref[...] = y * jax.nn.sigmoid(_dot(y.astype(BF16), wglu_ref[...]) + bglu_ref[...])

    @pl.when(chunk == pl.num_programs(0) - 1)
    def _():
        hr_out[...] = str_s[...]
        hi_out[...] = sti_s[...]


def _s5_call(u_rows, h0_re, h0_im, consts, n_seq, steps):
    rows = u_rows.shape[0]
    tile = n_seq * steps
    lb_re, lb_im, w_re, w_im, c_re, c_im, d_skip, w_glu_bf, b_glu = consts
    state = jax.ShapeDtypeStruct((n_seq, S5_LANES), F32)
    args = (u_rows, h0_re, h0_im, lb_re, lb_im, w_re, w_im, c_re, c_im, d_skip, w_glu_bf, b_glu)
    in_specs = [pl.BlockSpec((tile, S5_WIDTH), lambda i: (i, 0))] + [_full(a.shape) for a in args[1:]]
    return pl.pallas_call(
        functools.partial(_s5_kernel, n_seq=n_seq, steps=steps),
        grid=(rows // tile,),
        in_specs=in_specs,
        out_specs=[pl.BlockSpec((tile, S5_WIDTH), lambda i: (i, 0)), _full(state.shape), _full(state.shape)],
        out_shape=[jax.ShapeDtypeStruct((rows, S5_WIDTH), F32), state, state],
        scratch_shapes=[pltpu.VMEM((tile, S5_LANES), F32), pltpu.VMEM((tile, S5_LANES), F32),
                        pltpu.VMEM((n_seq, S5_LANES), F32), pltpu.VMEM((n_seq, S5_LANES), F32)],
        compiler_params=_params(("arbitrary",)),
        name="s5_mixer",
    )(*args)


def _head_sum(x, ones_ref):
    hi, lo = _split_bf16(x)
    return _dot(hi, ones_ref[...]) + _dot(lo, ones_ref[...])


def _rwkv_pre_kernel(p_ref, prev_ref, shift_ref, mu_ref, w0_ref, w2_ref, a0_ref, a2_ref, g2_ref, kk_ref, ka_ref,
                     ones_ref, r_out, w_out, k_out, v_out, kk_out, kka_out, g_out, *, n_seq):
    p = p_ref[...]
    tile = p.shape[0]
    head = jnp.where(pl.program_id(0) == 0, shift_ref[...], prev_ref[...])
    p_prev = head if tile == n_seq else jnp.concatenate([head, p[:tile - n_seq]], axis=0)
    ps = p + (p_prev - p) * mu_ref[...]
    w = RWKV_WIDTH
    r, k, v = ps[:, :w], ps[:, w:2 * w], ps[:, 2 * w:3 * w]
    lo = ps[:, 3 * w:3 * w + RWKV_LORA]
    g_lo = ps[:, 3 * w + RWKV_LORA:]
    w_raw = -jax.nn.softplus(-(w0_ref[...] + _dot_split(jnp.tanh(lo), w2_ref[0], w2_ref[1]))) - 0.5
    a = jax.nn.sigmoid(a0_ref[...] + _dot_split(lo, a2_ref[0], a2_ref[1]))
    kk = k * kk_ref[...]
    norm = jnp.sqrt(_head_sum(kk * kk, ones_ref))
    kk = kk / jnp.maximum(norm, 1e-12)
    r_out[...] = r
    w_out[...] = jnp.exp(-jnp.exp(w_raw))
    k_out[...] = k * (1.0 + (a - 1.0) * ka_ref[...])
    v_out[...] = v
    kk_out[...] = kk
    kka_out[...] = kk * a
    g_out[...] = _dot_split(jax.nn.sigmoid(g_lo), g2_ref[0], g2_ref[1])


def _rwkv_pre_call(p_rows, shift0, consts, n_seq, tile):
    rows = p_rows.shape[0]
    per = tile // n_seq
    vec = jax.ShapeDtypeStruct((rows, RWKV_WIDTH), F32)
    in_specs = [pl.BlockSpec((tile, RWKV_COLS), lambda i: (i, 0)),
                pl.BlockSpec((n_seq, RWKV_COLS), lambda i: (jnp.maximum(i * per - 1, 0), 0)),
                _full(shift0.shape)] + [_full(a.shape) for a in consts]
    return pl.pallas_call(
        functools.partial(_rwkv_pre_kernel, n_seq=n_seq),
        grid=(rows // tile,),
        in_specs=in_specs,
        out_specs=[pl.BlockSpec((tile, RWKV_WIDTH), lambda i: (i, 0))] * 7,
        out_shape=[vec] * 7,
        compiler_params=_params(("arbitrary",)),
        name="rwkv_prologue",
    )(p_rows, p_rows, shift0, *consts)


def _sublane_allsum(p):
    p = p + pltpu.roll(p, 4, 0)
    p = p + pltpu.roll(p, 2, 0)
    return p + pltpu.roll(p, 1, 0)


def _pairs_from_rows(x, nb):
    cols = [x[:, c * LANES:(c + 1) * LANES] for c in range(RWKV_WIDTH // LANES)]
    m = jnp.concatenate(cols * (LANES // (4 * nb)), axis=0)
    mt = m.T
    lane = lax.broadcasted_iota(jnp.int32, (RWKV_HEAD, LANES), 1)
    even_head = (lane // nb) % RWKV_HEADS < RWKV_HEADS // 2
    return jnp.where(even_head, mt[:RWKV_HEAD], pltpu.roll(mt[RWKV_HEAD:], 4 * nb, 1))


def _rows_from_pairs(y, nb):
    pairs = RWKV_HEADS * nb
    shifts = [(-(copy * pairs + parity * pairs // 2)) % LANES
              for parity in range(2) for copy in range(LANES // pairs)]
    mt = jnp.concatenate([y if s == 0 else pltpu.roll(y, s, 1) for s in shifts], axis=0)
    m = mt.T
    return jnp.concatenate([m[c * nb:(c + 1) * nb, :] for c in range(RWKV_WIDTH // LANES)], axis=1)


def _rwkv_scan_kernel(r_ref, w_ref, k_ref, kk_ref, kka_ref, v_ref, s0_ref, y_ref, sout_ref, s_s, sa_s, ka_s, kb_s,
                      va_s, vb_s, *, steps, n_v, nb):
    tc = pl.program_id(1)

    @pl.when(tc == 0)
    def _():
        s_s[...] = s0_ref[...]

    row_id = lax.broadcasted_iota(jnp.int32, (SUBLANES, LANES), 0)
    lane_id = lax.broadcasted_iota(jnp.int32, (n_v, LANES), 1)
    split = lambda x: x.reshape(RWKV_HEAD // SUBLANES, SUBLANES, LANES)
    k_refs = (r_ref, w_ref, k_ref, kk_ref, kka_ref)

    def stage(t, k_s, v_s):
        for i, ref in enumerate(k_refs):
            k_s[i] = _pairs_from_rows(ref[t], nb)
        full = _pairs_from_rows(v_ref[t], nb)
        v_s[...] = full if n_v == RWKV_HEAD else jnp.where(lane_id < LANES // 2, full[:n_v], full[n_v:])

    def advance(t, k_s, v_s):
        kk = split(k_s[3])
        for v in range(n_v):
            sa_s[v] = -_sublane_allsum(jnp.sum(split(s_s[v]) * kk, axis=0))
        r, w, k, kka = split(k_s[0]), split(k_s[1]), split(k_s[2]), split(k_s[4])
        y_tiles = []
        for vb in range(n_v // SUBLANES):
            v_tile = v_s[vb * SUBLANES:(vb + 1) * SUBLANES, :]
            y_tile = jnp.zeros((SUBLANES, LANES), F32)
            for j in range(SUBLANES):
                v = vb * SUBLANES + j
                v_row = jnp.broadcast_to(v_tile[j:j + 1, :], (SUBLANES, LANES))
                s = split(s_s[v]) * w + sa_s[v][None] * kka + v_row[None] * k
                s_s[v] = s.reshape(RWKV_HEAD, LANES)
                y_tile = jnp.where(row_id == j, _sublane_allsum(jnp.sum(s * r, axis=0)), y_tile)
            y_tiles.append(y_tile)
        y_ref[t] = _rows_from_pairs(jnp.concatenate(y_tiles, axis=0), nb)

    stage(0, ka_s, va_s)

    def two_steps(i, carry):
        t = 2 * i
        stage(t + 1, kb_s, vb_s)
        advance(t, ka_s, va_s)
        stage(jnp.minimum(t + 2, steps - 1), ka_s, va_s)
        advance(t + 1, kb_s, vb_s)
        return carry

    lax.fori_loop(0, steps // 2, two_steps, 0)

    @pl.when(tc == pl.num_programs(1) - 1)
    def _():
        sout_ref[...] = s_s[...]


def _rwkv_scan_call(r, w, k, kk, kka, v, s0, n_seq, steps):
    rows = r.shape[0]
    n_t = rows // n_seq
    n_v, _, lanes = s0.shape
    nb = n_seq // (lanes // LANES)
    assert steps % 2 == 0 and n_t % steps == 0
    as_steps = lambda a: a.reshape(n_t, n_seq, RWKV_WIDTH)
    xspec = pl.BlockSpec((steps, nb, RWKV_WIDTH), lambda g, i: (i, g, 0))
    sspec = pl.BlockSpec((n_v, RWKV_HEAD, LANES), lambda g, i: (0, 0, g))
    stage_k = pltpu.VMEM((5, RWKV_HEAD, LANES), F32)
    stage_v = pltpu.VMEM((n_v, LANES), F32)
    y, s = pl.pallas_call(
        functools.partial(_rwkv_scan_kernel, steps=steps, n_v=n_v, nb=nb),
        grid=(lanes // LANES, n_t // steps),
        in_specs=[xspec] * 6 + [sspec],
        out_specs=[xspec, sspec],
        out_shape=[jax.ShapeDtypeStruct((n_t, n_seq, RWKV_WIDTH), F32), jax.ShapeDtypeStruct(s0.shape, F32)],
        scratch_shapes=[pltpu.VMEM((n_v, RWKV_HEAD, LANES), F32), pltpu.VMEM((n_v, SUBLANES, LANES), F32),
                        stage_k, stage_k, stage_v, stage_v],
        compiler_params=_params(("arbitrary", "arbitrary")),
        name="rwkv_recurrence",
    )(*[as_steps(a) for a in (r, w, k, kk, kka, v)], s0)
    return y.reshape(rows, RWKV_WIDTH), s


def _state_to_pairs(s, nb):
    n_seq = s.shape[0]
    v_split = LANES // (RWKV_HEADS * nb)
    n_v = RWKV_HEAD // v_split
    s = s.reshape(n_seq // nb, nb, RWKV_HEADS // 2, 2, v_split, n_v, RWKV_HEAD)
    return jnp.transpose(s, (5, 6, 0, 4, 3, 2, 1)).reshape(n_v, RWKV_HEAD, n_seq // nb * LANES)


def _state_from_pairs(s, n_seq, nb):
    v_split = LANES // (RWKV_HEADS * nb)
    n_v = RWKV_HEAD // v_split
    s = s.reshape(n_v, RWKV_HEAD, n_seq // nb, v_split, 2, RWKV_HEADS // 2, nb)
    return jnp.transpose(s, (2, 6, 5, 4, 3, 0, 1)).reshape(n_seq, RWKV_HEADS, RWKV_HEAD, RWKV_HEAD)


def _mixout_kernel(x_ref, ys5_ref, yrw_ref, r_ref, k_ref, v_ref, g_ref, mod_ref, gnw_ref, gnb_ref, rk_ref,
                   ones_ref, wout_ref, n2g_ref, x1_ref, h2_ref, *, n_seq):
    y = yrw_ref[...]
    inv_n = 1.0 / RWKV_HEAD
    mean = _head_sum(y, ones_ref) * inv_n
    yc = y - mean
    var = _head_sum(yc * yc, ones_ref) * inv_n
    y = yc * lax.rsqrt(var + GN_EPS) * gnw_ref[...] + gnb_ref[...]
    v = v_ref[...]
    y = y + _head_sum(r_ref[...] * k_ref[...] * rk_ref[...], ones_ref) * v
    y = y * g_ref[...]
    mix = jnp.concatenate([ys5_ref[...], y], axis=-1).astype(BF16)
    x1 = _time_major_rows(x_ref) + _gated(_dot(mix, wout_ref[...]), mod_ref[2], n_seq)
    x1_ref[...] = x1
    h2_ref[...] = _modulate(_rms(x1, n2g_ref[...]), mod_ref[3], mod_ref[4], n_seq)


def _mixout_call(x, y_s5, y_rw, r, k, v, g, mod6, consts, n_seq, tile):
    d = x.shape[-1]
    rows = x.size // d
    wide = pl.BlockSpec((tile, d), lambda i: (i, 0))
    half = pl.BlockSpec((tile, RWKV_WIDTH), lambda i: (i, 0))
    return pl.pallas_call(
        functools.partial(_mixout_kernel, n_seq=n_seq),
        grid=(rows // tile,),
        in_specs=[_token_spec(x, n_seq, tile)] + [half] * 6 + [_full(mod6.shape)] + [_full(a.shape) for a in consts],
        out_specs=[wide, wide],
        out_shape=[jax.ShapeDtypeStruct((rows, d), F32)] * 2,
        compiler_params=_params(("arbitrary",)),
        name="mixer_out_norm2",
    )(x, y_s5, y_rw, r, k, v, g, mod6, *consts)


def _sort16_pairs():
    pairs = []
    k = 2
    while k <= PEER_TOPK:
        j = k // 2
        while j >= 1:
            pairs += [(i, i ^ j, (i & k) == 0) for i in range(PEER_TOPK) if i ^ j > i]
            j //= 2
        k *= 2
    return pairs


def _bitonic_merge_desc(x):
    x = list(x)
    j = PEER_TOPK // 2
    while j >= 1:
        for i in range(PEER_TOPK):
            if not i & j:
                x[i], x[i | j] = jnp.maximum(x[i], x[i | j]), jnp.minimum(x[i], x[i | j])
        j //= 2
    return x


def _top16_desc(slabs):
    x = list(slabs)
    for i, l, i_max in _sort16_pairs():
        hi, lo = jnp.maximum(x[i], x[l]), jnp.minimum(x[i], x[l])
        x[i], x[l] = (hi, lo) if i_max else (lo, hi)
    return _merge_sublane_lists(x)


def _merge_top16(x, y):
    return _bitonic_merge_desc([jnp.maximum(x[v], y[PEER_TOPK - 1 - v]) for v in range(PEER_TOPK)])


def _merge_sublane_lists(x):
    for shift in (1, 2, 4):
        x = _merge_top16(x, [pltpu.roll(v, shift, 0) for v in x])
    return x


def _top16_pair_sums(a, b):
    sub = lax.broadcasted_iota(jnp.int32, a[0].shape, 0)
    b_lo = b[0]
    for s in range(1, SUBLANES):
        b_lo = jnp.where(sub == s, b[s], b_lo)
    low = [a[i] + b_lo for i in range(PEER_TOPK)]
    high = [a[0] + b[j] for j in range(SUBLANES, PEER_TOPK)]
    padded = high + [jnp.full_like(a[0], -jnp.inf)] * (PEER_TOPK - len(high))
    return _merge_top16(_merge_sublane_lists(low), padded), low, high


def _count_at_least(slabs, bound):
    count = jnp.zeros_like(bound)
    for s in slabs:
        count = count + jnp.where(s >= bound, 1.0, 0.0)
    return _sublane_allsum(count)


def _peer_select(xt, wq_ref, k1_ref, k2_ref, q_s, a_s, b_s, e1_s, e2_s, thr_s, tie_s):
    tg = xt.shape[1]
    q_s[...] = _dot_split_lhs(wq_ref[0], wq_ref[1], xt).reshape(PEER_HEADS, 2 * PEER_HALF, tg)
    slabs = lambda s: [s[v * SUBLANES:(v + 1) * SUBLANES, :] for v in range(N_KEYS // SUBLANES)]
    tiled = lambda row: jnp.concatenate([row] * (N_KEYS // SUBLANES), axis=0)

    def head(h, carry):
        q = q_s[h]
        s1 = _dot(k1_ref[h], q[:PEER_HALF], HIGHEST)
        s2 = _dot(k2_ref[h], q[PEER_HALF:], HIGHEST)
        top1 = _top16_desc(slabs(s1))
        top2 = _top16_desc(slabs(s2))
        topc, sums_low, sums_high = _top16_pair_sums(top1, top2)
        z = jnp.ones_like(topc[0])
        for i in range(1, PEER_TOPK):
            z = z + jnp.exp(topc[i] - topc[0])
        a = jnp.where(s1 >= tiled(top1[-1]), s1, -jnp.inf)
        b = jnp.where(s2 >= tiled(top2[-1]), s2, -jnp.inf)
        a_s[h] = a
        b_s[h] = b
        e1_s[h] = jnp.exp(a - tiled(top1[0])) / tiled(z)
        e2_s[h] = jnp.exp(b - tiled(top2[0]))
        thr_s[h] = topc[-1]
        n_pairs = _count_at_least(sums_low, topc[-1])
        for e in sums_high:
            n_pairs = n_pairs + jnp.where(e >= topc[-1], 1.0, 0.0)
        tied = jnp.where(n_pairs == PEER_TOPK, 0.0, 1.0)
        for values, bound in ((slabs(s1), top1[-1]), (slabs(s2), top2[-1])):
            tied = jnp.maximum(tied, jnp.where(_count_at_least(values, bound) == PEER_TOPK, 0.0, 1.0))
        tie_s[...] = jnp.maximum(tie_s[...], tied)
        return carry

    lax.fori_loop(0, PEER_HEADS, head, 0)


def _stable_top16(x, row_id):
    vals, rows = [], []
    for _ in range(PEER_TOPK):
        m = jnp.max(x, axis=0, keepdims=True)
        r = jnp.min(jnp.where(x == m, row_id, float(x.shape[0])), axis=0, keepdims=True)
        vals.append(m)
        rows.append(r)
        x = jnp.where(row_id == r, -jnp.inf, x)
    return vals, rows


def _peer_select_exact(k1_ref, k2_ref, q_s, a_s, b_s, e1_s, e2_s, thr_s, pa_s, pb_s, pthr_s, lst_s, cand_s):
    tg = q_s.shape[2]
    key_id = lax.broadcasted_iota(jnp.int32, (N_KEYS, tg), 0).astype(F32)
    pair_id = lax.broadcasted_iota(jnp.int32, (PEER_TOPK * PEER_TOPK, tg), 0).astype(F32)

    def head(h, carry):
        q = q_s[h]
        s1 = _dot(k1_ref[h], q[:PEER_HALF], HIGHEST)
        s2 = _dot(k2_ref[h], q[PEER_HALF:], HIGHEST)
        v1, i1 = _stable_top16(s1, key_id)
        v2, i2 = _stable_top16(s2, key_id)
        rank1 = jnp.full((N_KEYS, tg), float(PEER_TOPK), F32)
        rank2 = rank1
        for r in range(PEER_TOPK):
            rank1 = jnp.where(key_id == i1[r], float(r), rank1)
            rank2 = jnp.where(key_id == i2[r], float(r), rank2)
            lst_s[r:r + 1, :] = v2[r]
        best2 = lst_s[...]
        for r in range(PEER_TOPK):
            cand_s[r * PEER_TOPK:(r + 1) * PEER_TOPK, :] = v1[r] + best2
        vc, pc = _stable_top16(cand_s[...], pair_id)
        z = jnp.ones_like(vc[0])
        for r in range(1, PEER_TOPK):
            z = z + jnp.exp(vc[r] - vc[0])
        a = jnp.where(rank1 < PEER_TOPK, s1, -jnp.inf)
        b = jnp.where(rank2 < PEER_TOPK, s2, -jnp.inf)
        a_s[h] = a
        b_s[h] = b
        e1_s[h] = jnp.exp(a - v1[0]) / z
        e2_s[h] = jnp.exp(b - v2[0])
        pa_s[h] = rank1 * float(PEER_TOPK)
        pb_s[h] = rank2
        thr_s[h] = jnp.broadcast_to(vc[-1], (SUBLANES, tg))
        pthr_s[h] = jnp.broadcast_to(pc[-1], (SUBLANES, tg))
        return carry

    lax.fori_loop(0, PEER_HEADS, head, 0)


def _peer_gates(g, key0, act_ref, pt_ref, a_s, b_s, e1_s, e2_s, thr_s):
    tg = act_ref.shape[2]
    for lg in range(tg // LANES):
        ls = slice(lg * LANES, (lg + 1) * LANES)
        a_rows = [a_s[g, h, pl.ds(key0, PEER_KEYS_PER_BLOCK), ls] for h in range(PEER_HEADS)]
        e_rows = [e1_s[g, h, pl.ds(key0, PEER_KEYS_PER_BLOCK), ls] for h in range(PEER_HEADS)]
        for i in range(PEER_KEYS_PER_BLOCK):
            gate = jnp.zeros((N_KEYS, LANES), F32)
            for h in range(PEER_HEADS):
                score = b_s[g, h, :, ls] + a_rows[h][i:i + 1, :]
                weight = e2_s[g, h, :, ls] * e_rows[h][i:i + 1, :]
                gate = gate + jnp.where(score >= thr_s[g, h, 0:1, ls], weight, 0.0)
            act = act_ref[g, i * N_KEYS:(i + 1) * N_KEYS, ls]
            pt_ref[g, i * N_KEYS:(i + 1) * N_KEYS, ls] = (gate * act).astype(BF16)


def _peer_gates_exact(g, key0, act_ref, pt_ref, a_s, b_s, e1_s, e2_s, thr_s, pa_s, pb_s, pthr_s):
    tg = act_ref.shape[2]
    rows = pl.ds(key0, PEER_KEYS_PER_BLOCK)
    for lg in range(tg // LANES):
        ls = slice(lg * LANES, (lg + 1) * LANES)
        for i in range(PEER_KEYS_PER_BLOCK):
            gate = jnp.zeros((N_KEYS, LANES), F32)
            for h in range(PEER_HEADS):
                score = b_s[g, h, :, ls] + a_s[g, h, rows, ls][i:i + 1, :]
                pos = pb_s[g, h, :, ls] + pa_s[g, h, rows, ls][i:i + 1, :]
                weight = e2_s[g, h, :, ls] * e1_s[g, h, rows, ls][i:i + 1, :]
                thr = thr_s[g, h, 0:1, ls]
                tied_in = jnp.where(score == thr, jnp.where(pos <= pthr_s[g, h, 0:1, ls], weight, 0.0), 0.0)
                gate = gate + jnp.where(score > thr, weight, tied_in)
            act = act_ref[g, i * N_KEYS:(i + 1) * N_KEYS, ls]
            pt_ref[g, i * N_KEYS:(i + 1) * N_KEYS, ls] = (gate * act).astype(BF16)


def _first_match(cases):
    (pred, fn), rest = cases[0], cases[1:]

    def hit():
        fn()

    def miss():
        if rest:
            _first_match(rest)

    lax.cond(pred, hit, miss)


def _peer_kernel(h2_ref, x1_ref, ga_ref, wq_ref, k1_ref, k2_ref, u_ref, vt_ref, fg_ref, o_ref,
                 xt_s, q_s, a_s, b_s, e1_s, e2_s, thr_s, pa_s, pb_s, pthr_s, tie_s, lst_s, cand_s, tied_s,
                 act0_s, act1_s, pt0_s, pt1_s, acc_s, *, n_seq, n_blocks):
    s = pl.program_id(1)
    n_groups = xt_s.shape[0]
    tg = xt_s.shape[2]
    acts, pts = (act0_s, act1_s), (pt0_s, pt1_s)
    sel = (a_s, b_s, e1_s, e2_s, thr_s)
    sel_exact = sel + (pa_s, pb_s, pthr_s)

    def stages(parity, first, gates, second, exact=False):
        key0 = pl.multiple_of((s - 1) * PEER_KEYS_PER_BLOCK, PEER_KEYS_PER_BLOCK)

        def group(g, carry):
            if first:
                acts[parity][g] = _gelu(_dot(u_ref[...], xt_s[g]))
            if gates and exact:
                _peer_gates_exact(g, key0, acts[1 - parity], pts[1 - parity], *sel_exact)
            elif gates:
                _peer_gates(g, key0, acts[1 - parity], pts[1 - parity], *sel)
            if second:
                acc_s[g] += _dot(vt_ref[...], pts[parity][g])
            return carry

        lax.fori_loop(0, n_groups, group, 0)

    def first_step():
        tie_s[...] = jnp.zeros_like(tie_s)

        def prep(g, carry):
            t0 = pl.multiple_of(g * tg, tg)
            xt = h2_ref[pl.ds(t0, tg), :].T
            xt_s[g] = xt.astype(BF16)
            _peer_select(xt, wq_ref, k1_ref, k2_ref, q_s, a_s.at[g], b_s.at[g], e1_s.at[g], e2_s.at[g],
                         thr_s.at[g], tie_s)
            return carry

        lax.fori_loop(0, n_groups, prep, 0)
        tied_s[0] = (jnp.max(tie_s[...]) > 0.0).astype(jnp.int32)

        @pl.when(tied_s[0] != 0)
        def _():
            def prep_exact(g, carry):
                t0 = pl.multiple_of(g * tg, tg)
                xt = h2_ref[pl.ds(t0, tg), :].T
                q_s[...] = _dot_split_lhs(wq_ref[0], wq_ref[1], xt).reshape(q_s.shape)
                _peer_select_exact(k1_ref, k2_ref, q_s, a_s.at[g], b_s.at[g], e1_s.at[g], e2_s.at[g], thr_s.at[g],
                                   pa_s.at[g], pb_s.at[g], pthr_s.at[g], lst_s, cand_s)
                return carry

            lax.fori_loop(0, n_groups, prep_exact, 0)

        acc_s[...] = jnp.zeros_like(acc_s)
        pt1_s[...] = jnp.zeros_like(pt1_s)
        stages(0, True, False, False)

    def by_tie(parity, first, gates, second):
        return lambda: lax.cond(tied_s[0] == 0, lambda: stages(parity, first, gates, second, False),
                                lambda: stages(parity, first, gates, second, True))

    def last_step():
        stages((n_blocks + 1) % 2, False, False, True)
        for g in range(n_groups):
            rows = slice(g * tg, (g + 1) * tg)
            x2 = x1_ref[rows, :] + _gated(acc_s[g].T, ga_ref[...], min(n_seq, tg))
            y = _rms(x2, fg_ref[...])
            if len(o_ref.shape) == 2:
                o_ref[rows, :] = y
            else:
                sg = tg // n_seq
                o_ref[:, g * sg:(g + 1) * sg, :] = jnp.swapaxes(y.reshape(sg, n_seq, y.shape[1]), 0, 1)

    middle = (s >= 1) & (s < n_blocks)
    _first_match([(middle & (s % 2 == 0), by_tie(0, True, True, True)),
                  (middle, by_tie(1, True, True, True)),
                  (s == 0, first_step),
                  (s == n_blocks, by_tie(n_blocks % 2, False, True, True)),
                  (s == n_blocks + 1, last_step)])


def _peer_call(h2, x1, gate2, wq_t, keys1, keys2, u_bf, vt_bf, final_g, n_seq, tm, batch_major):
    rows, d = h2.shape
    n_exp = u_bf.shape[0]
    eb = PEER_KEYS_PER_BLOCK * N_KEYS
    n_blocks = n_exp // eb
    tg = PEER_TOKEN_GROUP
    n_groups = tm // tg
    assert tm % tg == 0 and (tg % n_seq == 0 or n_seq % tg == 0) and n_exp % eb == 0 and n_blocks >= 2
    if n_seq > tg:
        raise NotImplementedError("more than PEER_TOKEN_GROUP sequences per step")
    tok = pl.BlockSpec((tm, d), lambda i, s: (i, 0))
    out = jax.ShapeDtypeStruct((n_seq, rows // n_seq, d) if batch_major else (rows, d), F32)
    sel = pltpu.VMEM((n_groups, PEER_HEADS, N_KEYS, tg), F32)
    row = pltpu.VMEM((n_groups, PEER_HEADS, SUBLANES, tg), F32)
    act = pltpu.VMEM((n_groups, eb, tg), F32)
    pt = pltpu.VMEM((n_groups, eb, tg), BF16)
    return pl.pallas_call(
        functools.partial(_peer_kernel, n_seq=n_seq, n_blocks=n_blocks),
        grid=(rows // tm, n_blocks + 2),
        in_specs=[tok, tok, _full(gate2.shape), _full(wq_t.shape), _full(keys1.shape), _full(keys2.shape),
                  pl.BlockSpec((eb, d), lambda i, s: (jnp.minimum(s, n_blocks - 1), 0)),
                  pl.BlockSpec((d, eb), lambda i, s: (0, jnp.clip(s - 2, 0, n_blocks - 1))),
                  _full((1, d))],
        out_specs=_token_spec(out, n_seq, tm),
        out_shape=out,
        scratch_shapes=[pltpu.VMEM((n_groups, d, tg), BF16), pltpu.VMEM((PEER_HEADS, 2 * PEER_HALF, tg), F32),
                        sel, sel, sel, sel, row, sel, sel, row, pltpu.VMEM((SUBLANES, tg), F32),
                        pltpu.VMEM((PEER_TOPK, tg), F32), pltpu.VMEM((PEER_TOPK * PEER_TOPK, tg), F32),
                        pltpu.SMEM((1,), jnp.int32),
                        act, act, pt, pt, pltpu.VMEM((n_groups, d, tg), F32)],
        compiler_params=_params(("arbitrary", "arbitrary")),
        name="peer_final_norm",
    )(h2, x1, gate2, wq_t, keys1, keys2, u_bf, vt_bf, final_g.reshape(1, d))


def _pick(n, target):
    t = min(n, target)
    while n % t:
        t -= 1
    return t


def _layer(x_btd, mod, s5_re0, s5_im0, wkv0, shift0, prm):
    n_seq, n_t, d = x_btd.shape
    rows = n_seq * n_t
    mod6 = jnp.transpose(mod.reshape(n_seq, 6, d), (1, 0, 2))
    tile = n_seq * _pick(n_t, max(1, 512 // n_seq))
    reorder_in_kernel = (PEER_TOKEN_GROUP // n_seq) % SUBLANES == 0 and (tile // n_seq) % SUBLANES == 0
    x_in = x_btd if reorder_in_kernel else jnp.transpose(x_btd, (1, 0, 2)).reshape(rows, d)

    u_rows, p_rows = _inproj_call(x_in, mod6, prm["norm1_g"], prm["w_in_bf"], n_seq, tile)

    s5_steps = _pick(n_t, max(1, 512 // n_seq))
    y_s5, s5_re, s5_im = _s5_call(u_rows, s5_re0.reshape(n_seq, S5_LANES), s5_im0.reshape(n_seq, S5_LANES),
                                  prm["s5_consts"], n_seq, s5_steps)

    r, w, k, v, kk, kka, g = _rwkv_pre_call(p_rows, shift0, prm["rwkv_pre_consts"], n_seq, tile)
    nb = min(n_seq, LANES // RWKV_HEADS)
    y_rw, s_pairs = _rwkv_scan_call(r, w, k, kk, kka, v, _state_to_pairs(wkv0, nb), n_seq, _pick(n_t, 16))
    wkv = _state_from_pairs(s_pairs, n_seq, nb)

    x1, h2 = _mixout_call(x_in, y_s5, y_rw, r, k, v, g, mod6, prm["mixout_consts"], n_seq, tile)

    y = _peer_call(h2, x1, mod6[5], prm["peer_wq_t"], prm["peer_keys1"], prm["peer_keys2"], prm["peer_u_bf"],
                   prm["peer_vt_bf"], prm["final_norm_g"], n_seq, tile, reorder_in_kernel)
    if not reorder_in_kernel:
        y = jnp.transpose(y.reshape(n_t, n_seq, d), (1, 0, 2))
    shift = p_rows[rows - n_seq:]
    return (y, s5_re.reshape(n_seq, S5_GROUPS, S5_STATE), s5_im.reshape(n_seq, S5_GROUPS, S5_STATE), wkv, shift)


def kernel(x_prompt, x_sample, state_s5_re, state_s5_im, state_wkv, state_shift, c_prompt, c_sample, w_ada, b_ada, norm1_g, norm2_g, w_in, w_out, s5_a_re, s5_a_im, s5_log_dt, s5_b_re, s5_b_im, s5_c_re, s5_c_im, s5_d, w_glu, b_glu, rwkv_mu, rwkv_w0, rwkv_w2, rwkv_a0, rwkv_a2, rwkv_g2, rwkv_k_k, rwkv_k_a, rwkv_r_k, rwkv_gn_w, rwkv_gn_b, peer_w_q, peer_keys1, peer_keys2, peer_u, peer_v, final_norm_g):
    assert w_ada.shape[0] == 1, "single-layer model"
    nbp = x_prompt.shape[0]
    row = lambda a: a.reshape(1, -1)

    mod = _ada_call(jnp.concatenate([c_prompt, c_sample], axis=0).astype(F32), w_ada[0], b_ada[0])

    lb_re, lb_im, bw_re, bw_im = _s5_prep_call(s5_a_re[0], s5_a_im[0], s5_log_dt[0], s5_b_re[0], s5_b_im[0])
    head_ones = jnp.kron(jnp.eye(RWKV_HEADS, dtype=BF16), jnp.ones((RWKV_HEAD, RWKV_HEAD), BF16))
    lora_pad = jnp.zeros((RWKV_LORA // 2, RWKV_WIDTH), F32)
    prm = {
        "norm1_g": norm1_g[0],
        "w_in_bf": w_in[0].astype(BF16),
        "s5_consts": (lb_re, lb_im, bw_re, bw_im, _s5_out_blockdiag(s5_c_re[0]).astype(BF16),
                      _s5_out_blockdiag(s5_c_im[0]).astype(BF16), row(s5_d[0]), w_glu[0].astype(BF16),
                      row(b_glu[0])),
        "rwkv_pre_consts": (row(rwkv_mu[0]), row(rwkv_w0[0]),
                            _split_call(jnp.concatenate([rwkv_w2[0], lora_pad], axis=0)), row(rwkv_a0[0]),
                            _split_call(jnp.concatenate([lora_pad, rwkv_a2[0]], axis=0)), _split_call(rwkv_g2[0]),
                            row(rwkv_k_k[0]), row(rwkv_k_a[0]), head_ones),
        "mixout_consts": (row(rwkv_gn_w[0]), row(rwkv_gn_b[0]), row(rwkv_r_k[0]), head_ones,
                          w_out[0].astype(BF16), row(norm2_g[0])),
        "peer_wq_t": _split_call(peer_w_q[0].T),
        "peer_keys1": peer_keys1[0],
        "peer_keys2": peer_keys2[0],
        "peer_u_bf": peer_u[0].astype(BF16),
        "peer_vt_bf": peer_v[0].T.astype(BF16),
        "final_norm_g": final_norm_g,
    }

    z_s5 = jnp.zeros((nbp, S5_GROUPS, S5_STATE), F32)
    z_wkv = jnp.zeros((nbp, RWKV_HEADS, RWKV_HEAD, RWKV_HEAD), F32)
    z_sh = jnp.zeros((nbp, RWKV_COLS), F32)
    yp, pr, pi, pw, psh = _layer(x_prompt.astype(F32), mod[:nbp], z_s5, z_s5, z_wkv, z_sh, prm)
    ys, sr, si, sw, ssh = _layer(x_sample.astype(F32), mod[nbp:], state_s5_re[0].astype(F32),
                                 state_s5_im[0].astype(F32), state_wkv[0].astype(F32),
                                 state_shift[0].astype(F32), prm)
    return (yp.astype(x_prompt.dtype), ys.astype(x_sample.dtype), pr[None], pi[None], pw[None], psh[None],
            sr[None], si[None], sw[None], ssh[None])
```

```python
import functools

import jax
import jax.numpy as jnp
from jax import lax
from jax.experimental import pallas as pl
from jax.experimental.pallas import tpu as pltpu

F32 = jnp.float32
BF16 = jnp.bfloat16
HIGHEST = lax.Precision.HIGHEST

LANES = 128
SUBLANES = 8
VMEM_LIMIT_BYTES = 56 * 1024 * 1024

D_MODEL = 1024
S5_WIDTH = 512
S5_GROUP = 16
S5_GROUPS = 32
S5_STATE = 64
S5_LANES = S5_GROUPS * S5_STATE
S5_CHUNKS = 4
S5_CHUNK_IN = S5_WIDTH // S5_CHUNKS
S5_CHUNK_ST = S5_LANES // S5_CHUNKS
RWKV_WIDTH = 512
RWKV_HEAD = 64
RWKV_HEADS = 8
RWKV_LORA = 128
RWKV_COLS = 3 * RWKV_WIDTH + 64 + 64 + 128
PEER_HEADS = 8
N_KEYS = 128
PEER_TOPK = 16
PEER_HALF = 64
PEER_TOKEN_GROUP = 256
PEER_KEYS_PER_BLOCK = 8
NORM_EPS = 1e-6
GN_EPS = 64e-5


def _params(sem):
    return pltpu.CompilerParams(dimension_semantics=sem, vmem_limit_bytes=VMEM_LIMIT_BYTES)


def _full(shape):
    return pl.BlockSpec(shape, lambda *_: (0,) * len(shape))


def _dot(a, b, precision=None):
    return jnp.dot(a, b, precision=precision, preferred_element_type=F32)


def _split_bf16(x):
    hi = x.astype(BF16)
    return hi, (x - hi.astype(F32)).astype(BF16)


def _dot_split(a, b_hi, b_lo):
    a_hi, a_lo = _split_bf16(a)
    return _dot(a_hi, b_hi) + (_dot(a_lo, b_hi) + _dot(a_hi, b_lo))


def _gelu(x):
    return 0.5 * x * (1.0 + lax.erf(x * (2.0 ** -0.5)))


def _rms(x, g):
    return x * lax.rsqrt(jnp.mean(x * x, axis=-1, keepdims=True) + NORM_EPS) * g


def _modulate(h, shift, scale, n_seq):
    rows, d = h.shape
    h3 = h.reshape(rows // n_seq, n_seq, d)
    return (h3 * (1.0 + scale)[None] + shift[None]).reshape(rows, d)


def _gated(h, gate, n_seq):
    rows, d = h.shape
    return (h.reshape(rows // n_seq, n_seq, d) * gate[None]).reshape(rows, d)


def _split_kernel(w_ref, o_ref):
    o_ref[0], o_ref[1] = _split_bf16(w_ref[...])


def _split_call(w):
    return pl.pallas_call(
        _split_kernel,
        out_shape=jax.ShapeDtypeStruct((2,) + w.shape, BF16),
        compiler_params=pltpu.CompilerParams(vmem_limit_bytes=VMEM_LIMIT_BYTES),
        name="split_weight",
    )(w)


def _ada_kernel(c_ref, w_ref, b_ref, o_ref):
    s = jax.nn.silu(c_ref[...])
    o_ref[...] = _dot(s, w_ref[...], HIGHEST) + b_ref[...]


def _ada_call(c_all, w_ada, b_ada):
    n, d = c_all.shape
    cols = w_ada.shape[1]
    return pl.pallas_call(
        _ada_kernel,
        grid=(cols // d,),
        in_specs=[_full((n, d)), pl.BlockSpec((d, d), lambda j: (0, j)), pl.BlockSpec((1, d), lambda j: (0, j))],
        out_specs=pl.BlockSpec((n, d), lambda j: (0, j)),
        out_shape=jax.ShapeDtypeStruct((n, cols), F32),
        compiler_params=_params(("arbitrary",)),
        name="adaln_mod",
    )(c_all, w_ada, b_ada.reshape(1, cols))


def _time_major_rows(x_ref):
    if len(x_ref.shape) == 2:
        return x_ref[...]
    n_seq, steps, d = x_ref.shape
    return jnp.swapaxes(x_ref[...], 0, 1).reshape(n_seq * steps, d)


def _token_spec(x, n_seq, tile):
    if x.ndim == 2:
        return pl.BlockSpec((tile, x.shape[1]), lambda i, *_: (i, 0))
    return pl.BlockSpec((n_seq, tile // n_seq, x.shape[2]), lambda i, *_: (0, i, 0))


def _inproj_kernel(x_ref, mod_ref, g_ref, w_ref, u_ref, p_ref, *, n_seq):
    h = _modulate(_rms(_time_major_rows(x_ref), g_ref[...]), mod_ref[0], mod_ref[1], n_seq)
    proj = _dot(h.astype(BF16), w_ref[...])
    u_ref[...] = proj[:, :S5_WIDTH]
    p_ref[...] = proj[:, S5_WIDTH:]


def _inproj_call(x, mod6, norm_g, w_in_bf, n_seq, tile):
    d = x.shape[-1]
    rows = x.size // d
    return pl.pallas_call(
        functools.partial(_inproj_kernel, n_seq=n_seq),
        grid=(rows // tile,),
        in_specs=[_token_spec(x, n_seq, tile), _full(mod6.shape), _full((1, d)),
                  _full(w_in_bf.shape)],
        out_specs=[pl.BlockSpec((tile, S5_WIDTH), lambda i: (i, 0)),
                   pl.BlockSpec((tile, RWKV_COLS), lambda i: (i, 0))],
        out_shape=[jax.ShapeDtypeStruct((rows, S5_WIDTH), F32), jax.ShapeDtypeStruct((rows, RWKV_COLS), F32)],
        compiler_params=_params(("arbitrary",)),
        name="norm1_inproj",
    )(x, mod6, norm_g.reshape(1, d), w_in_bf)


def _s5_prep_kernel(are_ref, aim_ref, ldt_ref, bre_ref, bim_ref, lbr_ref, lbi_ref, wre_ref, wim_ref):
    lam_re, lam_im = are_ref[...], aim_ref[...]
    dt = jnp.exp(ldt_ref[...])
    mag = jnp.exp(lam_re * dt)
    ang = lam_im * dt
    lb_re, lb_im = mag * jnp.cos(ang), mag * jnp.sin(ang)
    den = lam_re * lam_re + lam_im * lam_im
    n_re, n_im = lb_re - 1.0, lb_im
    coef_re = (n_re * lam_re + n_im * lam_im) / den
    coef_im = (n_im * lam_re - n_re * lam_im) / den
    lbr_ref[...] = lb_re
    lbi_ref[...] = lb_im
    for c in range(S5_CHUNKS):
        cr = coef_re[:, c * S5_CHUNK_ST:(c + 1) * S5_CHUNK_ST]
        ci = coef_im[:, c * S5_CHUNK_ST:(c + 1) * S5_CHUNK_ST]
        wre_ref[c] = (cr * bre_ref[c] - ci * bim_ref[c]).astype(BF16)
        wim_ref[c] = (cr * bim_ref[c] + ci * bre_ref[c]).astype(BF16)


def _s5_prep_call(a_re, a_im, log_dt, b_re, b_im):
    row = lambda a: a.reshape(1, S5_LANES)
    ldt = jnp.repeat(log_dt, S5_STATE).reshape(1, S5_LANES)
    w_shape = (S5_CHUNKS, S5_CHUNK_IN, S5_CHUNK_ST)
    return pl.pallas_call(
        _s5_prep_kernel,
        out_shape=[jax.ShapeDtypeStruct((1, S5_LANES), F32)] * 2 + [jax.ShapeDtypeStruct(w_shape, BF16)] * 2,
        compiler_params=pltpu.CompilerParams(vmem_limit_bytes=VMEM_LIMIT_BYTES),
        name="s5_discretise",
    )(row(a_re), row(a_im), ldt, _s5_in_blockdiag(b_re), _s5_in_blockdiag(b_im))


def _s5_in_blockdiag(b):
    gpc = S5_GROUPS // S5_CHUNKS
    bt = jnp.transpose(b, (0, 2, 1)).reshape(S5_CHUNKS, gpc, S5_GROUP, S5_STATE)
    eye = jnp.eye(gpc, dtype=b.dtype)
    bd = bt[:, :, :, None, :] * eye[None, :, None, :, None]
    return bd.reshape(S5_CHUNKS, S5_CHUNK_IN, S5_CHUNK_ST)


def _s5_out_blockdiag(c):
    gpc = S5_GROUPS // S5_CHUNKS
    ct = jnp.transpose(c, (0, 2, 1)).reshape(S5_CHUNKS, gpc, S5_STATE, S5_GROUP)
    eye = jnp.eye(gpc, dtype=c.dtype)
    bd = ct[:, :, :, None, :] * eye[None, :, None, :, None]
    return bd.reshape(S5_CHUNKS, S5_CHUNK_ST, S5_CHUNK_IN)


def _s5_kernel(u_ref, h0r_ref, h0i_ref, lbr_ref, lbi_ref, wre_ref, wim_ref, cre_ref, cim_ref, d_ref, wglu_ref,
               bglu_ref, y_ref, hr_out, hi_out, re_s, im_s, str_s, sti_s, *, n_seq, steps):
    chunk = pl.program_id(0)

    @pl.when(chunk == 0)
    def _():
        str_s[...] = h0r_ref[...]
        sti_s[...] = h0i_ref[...]

    u = u_ref[...]
    for c in range(S5_CHUNKS):
        uc = u[:, c * S5_CHUNK_IN:(c + 1) * S5_CHUNK_IN].astype(BF16)
        re_s[:, c * S5_CHUNK_ST:(c + 1) * S5_CHUNK_ST] = _dot(uc, wre_ref[c])
        im_s[:, c * S5_CHUNK_ST:(c + 1) * S5_CHUNK_ST] = _dot(uc, wim_ref[c])

    def seq_block(rb, carry):
        r0 = pl.multiple_of(rb * SUBLANES, SUBLANES)
        for c in range(S5_CHUNKS):
            lanes = slice(c * S5_CHUNK_ST, (c + 1) * S5_CHUNK_ST)
            lr = jnp.broadcast_to(lbr_ref[:, lanes], (SUBLANES, S5_CHUNK_ST))
            li = jnp.broadcast_to(lbi_ref[:, lanes], (SUBLANES, S5_CHUNK_ST))

            def step(t, h):
                hr, hi = h
                row = pl.multiple_of(t * n_seq + r0, SUBLANES)
                nr = lr * hr - li * hi + re_s[pl.ds(row, SUBLANES), lanes]
                ni = lr * hi + li * hr + im_s[pl.ds(row, SUBLANES), lanes]
                re_s[pl.ds(row, SUBLANES), lanes] = nr
                im_s[pl.ds(row, SUBLANES), lanes] = ni
                return nr, ni

            h0 = (str_s[pl.ds(r0, SUBLANES), lanes], sti_s[pl.ds(r0, SUBLANES), lanes])
            hr, hi = lax.fori_loop(0, steps, step, h0, unroll=min(steps, 8))
            str_s[pl.ds(r0, SUBLANES), lanes] = hr
            sti_s[pl.ds(r0, SUBLANES), lanes] = hi
        return carry

    lax.fori_loop(0, n_seq // SUBLANES, seq_block, 0)

    ys = []
    for c in range(S5_CHUNKS):
        lanes = slice(c * S5_CHUNK_ST, (c + 1) * S5_CHUNK_ST)
        ys.append(_dot(re_s[:, lanes].astype(BF16), cre_ref[c]) - _dot(im_s[:, lanes].astype(BF16), cim_ref[c]))
    y = jnp.concatenate(ys, axis=-1) + d_ref[...] * u
    y = _gelu(y)
    y_ref[...] = y * jax.nn.sigmoid(_dot(y.astype(BF16), wglu_ref[...]) + bglu_ref[...])

    @pl.when(chunk == pl.num_programs(0) - 1)
    def _():
        hr_out[...] = str_s[...]
        hi_out[...] = sti_s[...]


def _s5_call(u_rows, h0_re, h0_im, consts, n_seq, steps):
    rows = u_rows.shape[0]
    tile = n_seq * steps
    lb_re, lb_im, w_re, w_im, c_re, c_im, d_skip, w_glu_bf, b_glu = consts
    state = jax.ShapeDtypeStruct((n_seq, S5_LANES), F32)
    args = (u_rows, h0_re, h0_im, lb_re, lb_im, w_re, w_im, c_re, c_im, d_skip, w_glu_bf, b_glu)
    in_specs = [pl.BlockSpec((tile, S5_WIDTH), lambda i: (i, 0))] + [_full(a.shape) for a in args[1:]]
    return pl.pallas_call(
        functools.partial(_s5_kernel, n_seq=n_seq, steps=steps),
        grid=(rows // tile,),
        in_specs=in_specs,
        out_specs=[pl.BlockSpec((tile, S5_WIDTH), lambda i: (i, 0)), _full(state.shape), _full(state.shape)],
        out_shape=[jax.ShapeDtypeStruct((rows, S5_WIDTH), F32), state, state],
        scratch_shapes=[pltpu.VMEM((tile, S5_LANES), F32), pltpu.VMEM((tile, S5_LANES), F32),
                        pltpu.VMEM((n_seq, S5_LANES), F32), pltpu.VMEM((n_seq, S5_LANES), F32)],
        compiler_params=_params(("arbitrary",)),
        name="s5_mixer",
    )(*args)


def _head_sum(x, ones_ref):
    hi, lo = _split_bf16(x)
    return _dot(hi, ones_ref[...]) + _dot(lo, ones_ref[...])


def _rwkv_pre_kernel(p_ref, prev_ref, shift_ref, mu_ref, w0_ref, w2_ref, a0_ref, a2_ref, g2_ref, kk_ref, ka_ref,
                     ones_ref, r_out, w_out, k_out, v_out, kk_out, kka_out, g_out, *, n_seq):
    p = p_ref[...]
    tile = p.shape[0]
    head = jnp.where(pl.program_id(0) == 0, shift_ref[...], prev_ref[...])
    p_prev = head if tile == n_seq else jnp.concatenate([head, p[:tile - n_seq]], axis=0)
    ps = p + (p_prev - p) * mu_ref[...]
    w = RWKV_WIDTH
    r, k, v = ps[:, :w], ps[:, w:2 * w], ps[:, 2 * w:3 * w]
    lo = ps[:, 3 * w:3 * w + RWKV_LORA]
    g_lo = ps[:, 3 * w + RWKV_LORA:]
    w_raw = -jax.nn.softplus(-(w0_ref[...] + _dot_split(jnp.tanh(lo), w2_ref[0], w2_ref[1]))) - 0.5
    a = jax.nn.sigmoid(a0_ref[...] + _dot_split(lo, a2_ref[0], a2_ref[1]))
    kk = k * kk_ref[...]
    norm = jnp.sqrt(_head_sum(kk * kk, ones_ref))
    kk = kk / jnp.maximum(norm, 1e-12)
    r_out[...] = r
    w_out[...] = jnp.exp(-jnp.exp(w_raw))
    k_out[...] = k * (1.0 + (a - 1.0) * ka_ref[...])
    v_out[...] = v
    kk_out[...] = kk
    kka_out[...] = kk * a
    g_out[...] = _dot_split(jax.nn.sigmoid(g_lo), g2_ref[0], g2_ref[1])


def _rwkv_pre_call(p_rows, shift0, consts, n_seq, tile):
    rows = p_rows.shape[0]
    per = tile // n_seq
    vec = jax.ShapeDtypeStruct((rows, RWKV_WIDTH), F32)
    in_specs = [pl.BlockSpec((tile, RWKV_COLS), lambda i: (i, 0)),
                pl.BlockSpec((n_seq, RWKV_COLS), lambda i: (jnp.maximum(i * per - 1, 0), 0)),
                _full(shift0.shape)] + [_full(a.shape) for a in consts]
    return pl.pallas_call(
        functools.partial(_rwkv_pre_kernel, n_seq=n_seq),
        grid=(rows // tile,),
        in_specs=in_specs,
        out_specs=[pl.BlockSpec((tile, RWKV_WIDTH), lambda i: (i, 0))] * 7,
        out_shape=[vec] * 7,
        compiler_params=_params(("arbitrary",)),
        name="rwkv_prologue",
    )(p_rows, p_rows, shift0, *consts)


def _sublane_allsum(p):
    p = p + pltpu.roll(p, 4, 0)
    p = p + pltpu.roll(p, 2, 0)
    return p + pltpu.roll(p, 1, 0)


def _pairs_from_rows(x, nb):
    cols = [x[:, c * LANES:(c + 1) * LANES] for c in range(RWKV_WIDTH // LANES)]
    m = jnp.concatenate(cols * (LANES // (4 * nb)), axis=0)
    mt = m.T
    lane = lax.broadcasted_iota(jnp.int32, (RWKV_HEAD, LANES), 1)
    even_head = (lane // nb) % RWKV_HEADS < RWKV_HEADS // 2
    return jnp.where(even_head, mt[:RWKV_HEAD], pltpu.roll(mt[RWKV_HEAD:], 4 * nb, 1))


def _rows_from_pairs(y, nb):
    pairs = RWKV_HEADS * nb
    shifts = [(-(copy * pairs + parity * pairs // 2)) % LANES
              for parity in range(2) for copy in range(LANES // pairs)]
    mt = jnp.concatenate([y if s == 0 else pltpu.roll(y, s, 1) for s in shifts], axis=0)
    m = mt.T
    return jnp.concatenate([m[c * nb:(c + 1) * nb, :] for c in range(RWKV_WIDTH // LANES)], axis=1)


def _rwkv_scan_kernel(r_ref, w_ref, k_ref, kk_ref, kka_ref, v_ref, s0_ref, y_ref, sout_ref, s_s, sa_s, ka_s, kb_s,
                      va_s, vb_s, *, steps, n_v, nb):
    tc = pl.program_id(1)

    @pl.when(tc == 0)
    def _():
        s_s[...] = s0_ref[...]

    row_id = lax.broadcasted_iota(jnp.int32, (SUBLANES, LANES), 0)
    lane_id = lax.broadcasted_iota(jnp.int32, (n_v, LANES), 1)
    split = lambda x: x.reshape(RWKV_HEAD // SUBLANES, SUBLANES, LANES)
    k_refs = (r_ref, w_ref, k_ref, kk_ref, kka_ref)

    def stage(t, k_s, v_s):
        for i, ref in enumerate(k_refs):
            k_s[i] = _pairs_from_rows(ref[t], nb)
        full = _pairs_from_rows(v_ref[t], nb)
        v_s[...] = full if n_v == RWKV_HEAD else jnp.where(lane_id < LANES // 2, full[:n_v], full[n_v:])

    def advance(t, k_s, v_s):
        kk = split(k_s[3])
        for v in range(n_v):
            sa_s[v] = -_sublane_allsum(jnp.sum(split(s_s[v]) * kk, axis=0))
        r, w, k, kka = split(k_s[0]), split(k_s[1]), split(k_s[2]), split(k_s[4])
        y_tiles = []
        for vb in range(n_v // SUBLANES):
            v_tile = v_s[vb * SUBLANES:(vb + 1) * SUBLANES, :]
            y_tile = jnp.zeros((SUBLANES, LANES), F32)
            for j in range(SUBLANES):
                v = vb * SUBLANES + j
                v_row = jnp.broadcast_to(v_tile[j:j + 1, :], (SUBLANES, LANES))
                s = split(s_s[v]) * w + sa_s[v][None] * kka + v_row[None] * k
                s_s[v] = s.reshape(RWKV_HEAD, LANES)
                y_tile = jnp.where(row_id == j, _sublane_allsum(jnp.sum(s * r, axis=0)), y_tile)
            y_tiles.append(y_tile)
        y_ref[t] = _rows_from_pairs(jnp.concatenate(y_tiles, axis=0), nb)

    stage(0, ka_s, va_s)

    def two_steps(i, carry):
        t = 2 * i
        stage(t + 1, kb_s, vb_s)
        advance(t, ka_s, va_s)
        stage(jnp.minimum(t + 2, steps - 1), ka_s, va_s)
        advance(t + 1, kb_s, vb_s)
        return carry

    lax.fori_loop(0, steps // 2, two_steps, 0)

    @pl.when(tc == pl.num_programs(1) - 1)
    def _():
        sout_ref[...] = s_s[...]


def _rwkv_scan_call(r, w, k, kk, kka, v, s0, n_seq, steps):
    rows = r.shape[0]
    n_t = rows // n_seq
    n_v, _, lanes = s0.shape
    nb = n_seq // (lanes // LANES)
    assert steps % 2 == 0 and n_t % steps == 0
    as_steps = lambda a: a.reshape(n_t, n_seq, RWKV_WIDTH)
    xspec = pl.BlockSpec((steps, nb, RWKV_WIDTH), lambda g, i: (i, g, 0))
    sspec = pl.BlockSpec((n_v, RWKV_HEAD, LANES), lambda g, i: (0, 0, g))
    stage_k = pltpu.VMEM((5, RWKV_HEAD, LANES), F32)
    stage_v = pltpu.VMEM((n_v, LANES), F32)
    y, s = pl.pallas_call(
        functools.partial(_rwkv_scan_kernel, steps=steps, n_v=n_v, nb=nb),
        grid=(lanes // LANES, n_t // steps),
        in_specs=[xspec] * 6 + [sspec],
        out_specs=[xspec, sspec],
        out_shape=[jax.ShapeDtypeStruct((n_t, n_seq, RWKV_WIDTH), F32), jax.ShapeDtypeStruct(s0.shape, F32)],
        scratch_shapes=[pltpu.VMEM((n_v, RWKV_HEAD, LANES), F32), pltpu.VMEM((n_v, SUBLANES, LANES), F32),
                        stage_k, stage_k, stage_v, stage_v],
        compiler_params=_params(("arbitrary", "arbitrary")),
        name="rwkv_recurrence",
    )(*[as_steps(a) for a in (r, w, k, kk, kka, v)], s0)
    return y.reshape(rows, RWKV_WIDTH), s


def _state_to_pairs(s, nb):
    n_seq = s.shape[0]
    v_split = LANES // (RWKV_HEADS * nb)
    n_v = RWKV_HEAD // v_split
    s = s.reshape(n_seq // nb, nb, RWKV_HEADS // 2, 2, v_split, n_v, RWKV_HEAD)
    return jnp.transpose(s, (5, 6, 0, 4, 3, 2, 1)).reshape(n_v, RWKV_HEAD, n_seq // nb * LANES)


def _state_from_pairs(s, n_seq, nb):
    v_split = LANES // (RWKV_HEADS * nb)
    n_v = RWKV_HEAD // v_split
    s = s.reshape(n_v, RWKV_HEAD, n_seq // nb, v_split, 2, RWKV_HEADS // 2, nb)
    return jnp.transpose(s, (2, 6, 5, 4, 3, 0, 1)).reshape(n_seq, RWKV_HEADS, RWKV_HEAD, RWKV_HEAD)


def _mixout_kernel(x_ref, ys5_ref, yrw_ref, r_ref, k_ref, v_ref, g_ref, mod_ref, gnw_ref, gnb_ref, rk_ref,
                   ones_ref, wout_ref, x1_ref, *, n_seq):
    y = yrw_ref[...]
    inv_n = 1.0 / RWKV_HEAD
    mean = _head_sum(y, ones_ref) * inv_n
    yc = y - mean
    var = _head_sum(yc * yc, ones_ref) * inv_n
    y = yc * lax.rsqrt(var + GN_EPS) * gnw_ref[...] + gnb_ref[...]
    v = v_ref[...]
    y = y + _head_sum(r_ref[...] * k_ref[...] * rk_ref[...], ones_ref) * v
    y = y * g_ref[...]
    mix = jnp.concatenate([ys5_ref[...], y], axis=-1).astype(BF16)
    x1_ref[...] = _time_major_rows(x_ref) + _gated(_dot(mix, wout_ref[...]), mod_ref[2], n_seq)


def _mixout_call(x, y_s5, y_rw, r, k, v, g, mod6, consts, n_seq, tile):
    d = x.shape[-1]
    rows = x.size // d
    wide = pl.BlockSpec((tile, d), lambda i: (i, 0))
    half = pl.BlockSpec((tile, RWKV_WIDTH), lambda i: (i, 0))
    return pl.pallas_call(
        functools.partial(_mixout_kernel, n_seq=n_seq),
        grid=(rows // tile,),
        in_specs=[_token_spec(x, n_seq, tile)] + [half] * 6 + [_full(mod6.shape)] + [_full(a.shape) for a in consts],
        out_specs=wide,
        out_shape=jax.ShapeDtypeStruct((rows, d), F32),
        compiler_params=_params(("arbitrary",)),
        name="mixer_out",
    )(x, y_s5, y_rw, r, k, v, g, mod6, *consts)


def _sort16_pairs():
    pairs = []
    k = 2
    while k <= PEER_TOPK:
        j = k // 2
        while j >= 1:
            pairs += [(i, i ^ j, (i & k) == 0) for i in range(PEER_TOPK) if i ^ j > i]
            j //= 2
        k *= 2
    return pairs


def _bitonic_merge_desc(x):
    x = list(x)
    j = PEER_TOPK // 2
    while j >= 1:
        for i in range(PEER_TOPK):
            if not i & j:
                x[i], x[i | j] = jnp.maximum(x[i], x[i | j]), jnp.minimum(x[i], x[i | j])
        j //= 2
    return x


def _top16_desc(slabs):
    x = list(slabs)
    for i, l, i_max in _sort16_pairs():
        hi, lo = jnp.maximum(x[i], x[l]), jnp.minimum(x[i], x[l])
        x[i], x[l] = (hi, lo) if i_max else (lo, hi)
    return _merge_sublane_lists(x)


def _merge_top16(x, y):
    return _bitonic_merge_desc([jnp.maximum(x[v], y[PEER_TOPK - 1 - v]) for v in range(PEER_TOPK)])


def _merge_sublane_lists(x):
    for shift in (1, 2, 4):
        x = _merge_top16(x, [pltpu.roll(v, shift, 0) for v in x])
    return x


def _top16_pair_sums(a, b):
    sub = lax.broadcasted_iota(jnp.int32, a[0].shape, 0)
    b_lo = b[0]
    for s in range(1, SUBLANES):
        b_lo = jnp.where(sub == s, b[s], b_lo)
    low = [a[i] + b_lo for i in range(PEER_TOPK)]
    high = [a[0] + b[j] for j in range(SUBLANES, PEER_TOPK)]
    padded = high + [jnp.full_like(a[0], -jnp.inf)] * (PEER_TOPK - len(high))
    return _merge_top16(_merge_sublane_lists(low), padded), low, high


def _count_at_least(slabs, bound):
    count = jnp.zeros_like(bound)
    for s in slabs:
        count = count + jnp.where(s >= bound, 1.0, 0.0)
    return _sublane_allsum(count)


def _peer_select(k1_ref, k2_ref, q_s, a_s, b_s, e1_s, e2_s, thr_s, tie_s):
    tg = q_s.shape[2]
    slabs = lambda s: [s[v * SUBLANES:(v + 1) * SUBLANES, :] for v in range(N_KEYS // SUBLANES)]
    tiled = lambda row: jnp.concatenate([row] * (N_KEYS // SUBLANES), axis=0)

    def head(h, carry):
        q = q_s[h]
        s1 = _dot(k1_ref[h], q[:PEER_HALF], HIGHEST)
        s2 = _dot(k2_ref[h], q[PEER_HALF:], HIGHEST)
        top1 = _top16_desc(slabs(s1))
        top2 = _top16_desc(slabs(s2))
        topc, sums_low, sums_high = _top16_pair_sums(top1, top2)
        z = jnp.ones_like(topc[0])
        for i in range(1, PEER_TOPK):
            z = z + jnp.exp(topc[i] - topc[0])
        a = jnp.where(s1 >= tiled(top1[-1]), s1, -jnp.inf)
        b = jnp.where(s2 >= tiled(top2[-1]), s2, -jnp.inf)
        a_s[h] = a
        b_s[h] = b
        e1_s[h] = jnp.exp(a - tiled(top1[0])) / tiled(z)
        e2_s[h] = jnp.exp(b - tiled(top2[0]))
        thr_s[h] = topc[-1]
        n_pairs = _count_at_least(sums_low, topc[-1])
        for e in sums_high:
            n_pairs = n_pairs + jnp.where(e >= topc[-1], 1.0, 0.0)
        tied = jnp.where(n_pairs == PEER_TOPK, 0.0, 1.0)
        for values, bound in ((slabs(s1), top1[-1]), (slabs(s2), top2[-1])):
            tied = jnp.maximum(tied, jnp.where(_count_at_least(values, bound) == PEER_TOPK, 0.0, 1.0))
        tie_s[...] = jnp.maximum(tie_s[...], tied)
        return carry

    lax.fori_loop(0, PEER_HEADS, head, 0)


def _stable_top16(x, row_id):
    vals, rows = [], []
    for _ in range(PEER_TOPK):
        m = jnp.max(x, axis=0, keepdims=True)
        r = jnp.min(jnp.where(x == m, row_id, float(x.shape[0])), axis=0, keepdims=True)
        vals.append(m)
        rows.append(r)
        x = jnp.where(row_id == r, -jnp.inf, x)
    return vals, rows


def _peer_select_exact(k1_ref, k2_ref, q_s, a_s, b_s, e1_s, e2_s, thr_s, pa_s, pb_s, pthr_s, lst_s, cand_s):
    tg = q_s.shape[2]
    key_id = lax.broadcasted_iota(jnp.int32, (N_KEYS, tg), 0).astype(F32)
    pair_id = lax.broadcasted_iota(jnp.int32, (PEER_TOPK * PEER_TOPK, tg), 0).astype(F32)

    def head(h, carry):
        q = q_s[h]
        s1 = _dot(k1_ref[h], q[:PEER_HALF], HIGHEST)
        s2 = _dot(k2_ref[h], q[PEER_HALF:], HIGHEST)
        v1, i1 = _stable_top16(s1, key_id)
        v2, i2 = _stable_top16(s2, key_id)
        rank1 = jnp.full((N_KEYS, tg), float(PEER_TOPK), F32)
        rank2 = rank1
        for r in range(PEER_TOPK):
            rank1 = jnp.where(key_id == i1[r], float(r), rank1)
            rank2 = jnp.where(key_id == i2[r], float(r), rank2)
            lst_s[r:r + 1, :] = v2[r]
        best2 = lst_s[...]
        for r in range(PEER_TOPK):
            cand_s[r * PEER_TOPK:(r + 1) * PEER_TOPK, :] = v1[r] + best2
        vc, pc = _stable_top16(cand_s[...], pair_id)
        z = jnp.ones_like(vc[0])
        for r in range(1, PEER_TOPK):
            z = z + jnp.exp(vc[r] - vc[0])
        a = jnp.where(rank1 < PEER_TOPK, s1, -jnp.inf)
        b = jnp.where(rank2 < PEER_TOPK, s2, -jnp.inf)
        a_s[h] = a
        b_s[h] = b
        e1_s[h] = jnp.exp(a - v1[0]) / z
        e2_s[h] = jnp.exp(b - v2[0])
        pa_s[h] = rank1 * float(PEER_TOPK)
        pb_s[h] = rank2
        thr_s[h] = jnp.broadcast_to(vc[-1], (SUBLANES, tg))
        pthr_s[h] = jnp.broadcast_to(pc[-1], (SUBLANES, tg))
        return carry

    lax.fori_loop(0, PEER_HEADS, head, 0)


def _peer_gates(g, key0, act_ref, pt_ref, a_s, b_s, e1_s, e2_s, thr_s):
    tg = act_ref.shape[2]
    for lg in range(tg // LANES):
        ls = slice(lg * LANES, (lg + 1) * LANES)
        a_rows = [a_s[g, h, pl.ds(key0, PEER_KEYS_PER_BLOCK), ls] for h in range(PEER_HEADS)]
        e_rows = [e1_s[g, h, pl.ds(key0, PEER_KEYS_PER_BLOCK), ls] for h in range(PEER_HEADS)]
        for i in range(PEER_KEYS_PER_BLOCK):
            gate = jnp.zeros((N_KEYS, LANES), F32)
            for h in range(PEER_HEADS):
                score = b_s[g, h, :, ls] + a_rows[h][i:i + 1, :]
                weight = e2_s[g, h, :, ls] * e_rows[h][i:i + 1, :]
                gate = gate + jnp.where(score >= thr_s[g, h, 0:1, ls], weight, 0.0)
            act = act_ref[g, i * N_KEYS:(i + 1) * N_KEYS, ls]
            pt_ref[g, i * N_KEYS:(i + 1) * N_KEYS, ls] = (gate * act).astype(BF16)


def _peer_gates_exact(g, key0, act_ref, pt_ref, a_s, b_s, e1_s, e2_s, thr_s, pa_s, pb_s, pthr_s):
    tg = act_ref.shape[2]
    rows = pl.ds(key0, PEER_KEYS_PER_BLOCK)
    for lg in range(tg // LANES):
        ls = slice(lg * LANES, (lg + 1) * LANES)
        for i in range(PEER_KEYS_PER_BLOCK):
            gate = jnp.zeros((N_KEYS, LANES), F32)
            for h in range(PEER_HEADS):
                score = b_s[g, h, :, ls] + a_s[g, h, rows, ls][i:i + 1, :]
                pos = pb_s[g, h, :, ls] + pa_s[g, h, rows, ls][i:i + 1, :]
                weight = e2_s[g, h, :, ls] * e1_s[g, h, rows, ls][i:i + 1, :]
                thr = thr_s[g, h, 0:1, ls]
                tied_in = jnp.where(score == thr, jnp.where(pos <= pthr_s[g, h, 0:1, ls], weight, 0.0), 0.0)
                gate = gate + jnp.where(score > thr, weight, tied_in)
            act = act_ref[g, i * N_KEYS:(i + 1) * N_KEYS, ls]
            pt_ref[g, i * N_KEYS:(i + 1) * N_KEYS, ls] = (gate * act).astype(BF16)


def _first_match(cases):
    (pred, fn), rest = cases[0], cases[1:]

    def hit():
        fn()

    def miss():
        if rest:
            _first_match(rest)

    lax.cond(pred, hit, miss)


def _peer_kernel(x1_ref, sh_ref, sc_ref, ga_ref, n2g_ref, wq_ref, k1_ref, k2_ref, u_ref, vt_ref, fg_ref, o_ref,
                 xt_s, q_s, a_s, b_s, e1_s, e2_s, thr_s, pa_s, pb_s, pthr_s, tie_s, lst_s, cand_s, tied_s,
                 act0_s, act1_s, pt0_s, pt1_s, acc_s, *, n_seq, n_blocks):
    s = pl.program_id(1)
    n_groups = xt_s.shape[0]
    tg = xt_s.shape[2]
    acts, pts = (act0_s, act1_s), (pt0_s, pt1_s)
    sel = (a_s, b_s, e1_s, e2_s, thr_s)
    sel_exact = sel + (pa_s, pb_s, pthr_s)

    def stages(parity, first, gates, second, exact=False):
        key0 = pl.multiple_of((s - 1) * PEER_KEYS_PER_BLOCK, PEER_KEYS_PER_BLOCK)

        def group(g, carry):
            if first:
                acts[parity][g] = _gelu(_dot(u_ref[...], xt_s[g]))
            if gates and exact:
                _peer_gates_exact(g, key0, acts[1 - parity], pts[1 - parity], *sel_exact)
            elif gates:
                _peer_gates(g, key0, acts[1 - parity], pts[1 - parity], *sel)
            if second:
                acc_s[g] += _dot(vt_ref[...], pts[parity][g])
            return carry

        lax.fori_loop(0, n_groups, group, 0)

    def first_step():
        tie_s[...] = jnp.zeros_like(tie_s)

        def queries(g):
            t0 = pl.multiple_of(g * tg, tg)
            h2 = _modulate(_rms(x1_ref[pl.ds(t0, tg), :], n2g_ref[...]), sh_ref[...], sc_ref[...], min(n_seq, tg))
            xt_s[g] = h2.T.astype(BF16)
            q_s[...] = _dot(wq_ref[...], xt_s[g]).reshape(q_s.shape)

        def prep(g, carry):
            queries(g)
            _peer_select(k1_ref, k2_ref, q_s, a_s.at[g], b_s.at[g], e1_s.at[g], e2_s.at[g], thr_s.at[g], tie_s)
            return carry

        lax.fori_loop(0, n_groups, prep, 0)
        tied_s[0] = (jnp.max(tie_s[...]) > 0.0).astype(jnp.int32)

        @pl.when(tied_s[0] != 0)
        def _():
            def prep_exact(g, carry):
                queries(g)
                _peer_select_exact(k1_ref, k2_ref, q_s, a_s.at[g], b_s.at[g], e1_s.at[g], e2_s.at[g], thr_s.at[g],
                                   pa_s.at[g], pb_s.at[g], pthr_s.at[g], lst_s, cand_s)
                return carry

            lax.fori_loop(0, n_groups, prep_exact, 0)

        acc_s[...] = jnp.zeros_like(acc_s)
        pt1_s[...] = jnp.zeros_like(pt1_s)
        stages(0, True, False, False)

    def by_tie(parity, first, gates, second):
        return lambda: lax.cond(tied_s[0] == 0, lambda: stages(parity, first, gates, second, False),
                                lambda: stages(parity, first, gates, second, True))

    def last_step():
        stages((n_blocks + 1) % 2, False, False, True)
        for g in range(n_groups):
            rows = slice(g * tg, (g + 1) * tg)
            x2 = x1_ref[rows, :] + _gated(acc_s[g].T, ga_ref[...], min(n_seq, tg))
            y = _rms(x2, fg_ref[...])
            if len(o_ref.shape) == 2:
                o_ref[rows, :] = y
            else:
                sg = tg // n_seq
                o_ref[:, g * sg:(g + 1) * sg, :] = jnp.swapaxes(y.reshape(sg, n_seq, y.shape[1]), 0, 1)

    middle = (s >= 1) & (s < n_blocks)
    _first_match([(middle & (s % 2 == 0), by_tie(0, True, True, True)),
                  (middle, by_tie(1, True, True, True)),
                  (s == 0, first_step),
                  (s == n_blocks, by_tie(n_blocks % 2, False, True, True)),
                  (s == n_blocks + 1, last_step)])


def _peer_call(x1, mod6, norm2_g, wq_t, keys1, keys2, u_bf, vt_bf, final_g, n_seq, tm, batch_major):
    rows, d = x1.shape
    shift2, scale2, gate2 = mod6[3], mod6[4], mod6[5]
    n_exp = u_bf.shape[0]
    eb = PEER_KEYS_PER_BLOCK * N_KEYS
    n_blocks = n_exp // eb
    tg = PEER_TOKEN_GROUP
    n_groups = tm // tg
    assert tm % tg == 0 and (tg % n_seq == 0 or n_seq % tg == 0) and n_exp % eb == 0 and n_blocks >= 2
    if n_seq > tg:
        raise NotImplementedError("more than PEER_TOKEN_GROUP sequences per step")
    tok = pl.BlockSpec((tm, d), lambda i, s: (i, 0))
    out = jax.ShapeDtypeStruct((n_seq, rows // n_seq, d) if batch_major else (rows, d), F32)
    sel = pltpu.VMEM((n_groups, PEER_HEADS, N_KEYS, tg), F32)
    row = pltpu.VMEM((n_groups, PEER_HEADS, SUBLANES, tg), F32)
    act = pltpu.VMEM((n_groups, eb, tg), F32)
    pt = pltpu.VMEM((n_groups, eb, tg), BF16)
    return pl.pallas_call(
        functools.partial(_peer_kernel, n_seq=n_seq, n_blocks=n_blocks),
        grid=(rows // tm, n_blocks + 2),
        in_specs=[tok, _full(gate2.shape), _full(gate2.shape), _full(gate2.shape), _full((1, d)),
                  _full(wq_t.shape), _full(keys1.shape), _full(keys2.shape),
                  pl.BlockSpec((eb, d), lambda i, s: (jnp.minimum(s, n_blocks - 1), 0)),
                  pl.BlockSpec((d, eb), lambda i, s: (0, jnp.clip(s - 2, 0, n_blocks - 1))),
                  _full((1, d))],
        out_specs=_token_spec(out, n_seq, tm),
        out_shape=out,
        scratch_shapes=[pltpu.VMEM((n_groups, d, tg), BF16), pltpu.VMEM((PEER_HEADS, 2 * PEER_HALF, tg), F32),
                        sel, sel, sel, sel, row, sel, sel, row, pltpu.VMEM((SUBLANES, tg), F32),
                        pltpu.VMEM((PEER_TOPK, tg), F32), pltpu.VMEM((PEER_TOPK * PEER_TOPK, tg), F32),
                        pltpu.SMEM((1,), jnp.int32),
                        act, act, pt, pt, pltpu.VMEM((n_groups, d, tg), F32)],
        compiler_params=_params(("arbitrary", "arbitrary")),
        name="peer_final_norm",
    )(x1, shift2, scale2, gate2, norm2_g.reshape(1, d), wq_t, keys1, keys2, u_bf, vt_bf, final_g.reshape(1, d))


def _pick(n, target):
    t = min(n, target)
    while n % t:
        t -= 1
    return t


def _layer(x_btd, mod, s5_re0, s5_im0, wkv0, shift0, prm):
    n_seq, n_t, d = x_btd.shape
    rows = n_seq * n_t
    mod6 = jnp.transpose(mod.reshape(n_seq, 6, d), (1, 0, 2))
    tile = n_seq * _pick(n_t, max(1, 512 // n_seq))
    reorder_in_kernel = (PEER_TOKEN_GROUP // n_seq) % SUBLANES == 0 and (tile // n_seq) % SUBLANES == 0
    x_in = x_btd if reorder_in_kernel else jnp.transpose(x_btd, (1, 0, 2)).reshape(rows, d)

    u_rows, p_rows = _inproj_call(x_in, mod6, prm["norm1_g"], prm["w_in_bf"], n_seq, tile)

    s5_steps = _pick(n_t, max(1, 512 // n_seq))
    y_s5, s5_re, s5_im = _s5_call(u_rows, s5_re0.reshape(n_seq, S5_LANES), s5_im0.reshape(n_seq, S5_LANES),
                                  prm["s5_consts"], n_seq, s5_steps)

    r, w, k, v, kk, kka, g = _rwkv_pre_call(p_rows, shift0, prm["rwkv_pre_consts"], n_seq, tile)
    nb = min(n_seq, LANES // RWKV_HEADS)
    y_rw, s_pairs = _rwkv_scan_call(r, w, k, kk, kka, v, _state_to_pairs(wkv0, nb), n_seq, _pick(n_t, 16))
    wkv = _state_from_pairs(s_pairs, n_seq, nb)

    x1 = _mixout_call(x_in, y_s5, y_rw, r, k, v, g, mod6, prm["mixout_consts"], n_seq, tile)

    y = _peer_call(x1, mod6, prm["norm2_g"], prm["peer_wq_t"], prm["peer_keys1"], prm["peer_keys2"],
                   prm["peer_u_bf"], prm["peer_vt_bf"], prm["final_norm_g"], n_seq, tile, reorder_in_kernel)
    if not reorder_in_kernel:
        y = jnp.transpose(y.reshape(n_t, n_seq, d), (1, 0, 2))
    shift = p_rows[rows - n_seq:]
    return (y, s5_re.reshape(n_seq, S5_GROUPS, S5_STATE), s5_im.reshape(n_seq, S5_GROUPS, S5_STATE), wkv, shift)


def kernel(x_prompt, x_sample, state_s5_re, state_s5_im, state_wkv, state_shift, c_prompt, c_sample, w_ada, b_ada, norm1_g, norm2_g, w_in, w_out, s5_a_re, s5_a_im, s5_log_dt, s5_b_re, s5_b_im, s5_c_re, s5_c_im, s5_d, w_glu, b_glu, rwkv_mu, rwkv_w0, rwkv_w2, rwkv_a0, rwkv_a2, rwkv_g2, rwkv_k_k, rwkv_k_a, rwkv_r_k, rwkv_gn_w, rwkv_gn_b, peer_w_q, peer_keys1, peer_keys2, peer_u, peer_v, final_norm_g):
    assert w_ada.shape[0] == 1, "single-layer model"
    nbp = x_prompt.shape[0]
    row = lambda a: a.reshape(1, -1)

    mod = _ada_call(jnp.concatenate([c_prompt, c_sample], axis=0).astype(F32), w_ada[0], b_ada[0])

    lb_re, lb_im, bw_re, bw_im = _s5_prep_call(s5_a_re[0], s5_a_im[0], s5_log_dt[0], s5_b_re[0], s5_b_im[0])
    head_ones = jnp.kron(jnp.eye(RWKV_HEADS, dtype=BF16), jnp.ones((RWKV_HEAD, RWKV_HEAD), BF16))
    lora_pad = jnp.zeros((RWKV_LORA // 2, RWKV_WIDTH), F32)
    prm = {
        "norm1_g": norm1_g[0],
        "w_in_bf": w_in[0].astype(BF16),
        "s5_consts": (lb_re, lb_im, bw_re, bw_im, _s5_out_blockdiag(s5_c_re[0]).astype(BF16),
                      _s5_out_blockdiag(s5_c_im[0]).astype(BF16), row(s5_d[0]), w_glu[0].astype(BF16),
                      row(b_glu[0])),
        "rwkv_pre_consts": (row(rwkv_mu[0]), row(rwkv_w0[0]),
                            _split_call(jnp.concatenate([rwkv_w2[0], lora_pad], axis=0)), row(rwkv_a0[0]),
                            _split_call(jnp.concatenate([lora_pad, rwkv_a2[0]], axis=0)), _split_call(rwkv_g2[0]),
                            row(rwkv_k_k[0]), row(rwkv_k_a[0]), head_ones),
        "mixout_consts": (row(rwkv_gn_w[0]), row(rwkv_gn_b[0]), row(rwkv_r_k[0]), head_ones,
                          w_out[0].astype(BF16)),
        "norm2_g": norm2_g[0],
        "peer_wq_t": peer_w_q[0].T.astype(BF16),
        "peer_keys1": peer_keys1[0],
        "peer_keys2": peer_keys2[0],
        "peer_u_bf": peer_u[0].astype(BF16),
        "peer_vt_bf": peer_v[0].T.astype(BF16),
        "final_norm_g": final_norm_g,
    }

    z_s5 = jnp.zeros((nbp, S5_GROUPS, S5_STATE), F32)
    z_wkv = jnp.zeros((nbp, RWKV_HEADS, RWKV_HEAD, RWKV_HEAD), F32)
    z_sh = jnp.zeros((nbp, RWKV_COLS), F32)
    yp, pr, pi, pw, psh = _layer(x_prompt.astype(F32), mod[:nbp], z_s5, z_s5, z_wkv, z_sh, prm)
    ys, sr, si, sw, ssh = _layer(x_sample.astype(F32), mod[nbp:], state_s5_re[0].astype(F32),
                                 state_s5_im[0].astype(F32), state_wkv[0].astype(F32),
                                 state_shift[0].astype(F32), prm)
    return (yp.astype(x_prompt.dtype), ys.astype(x_sample.dtype), pr[None], pi[None], pw[None], psh[None],
            sr[None], si[None], sw[None], ssh[None])
```

```python
import functools

import jax
import jax.numpy as jnp
from jax import lax
from jax.experimental import pallas as pl
from jax.experimental.pallas import tpu as pltpu

F32 = jnp.float32
BF16 = jnp.bfloat16
HIGHEST = lax.Precision.HIGHEST

LANES = 128
SUBLANES = 8
VMEM_LIMIT_BYTES = 56 * 1024 * 1024

D_MODEL = 1024
S5_WIDTH = 512
S5_GROUP = 16
S5_GROUPS = 32
S5_STATE = 64
S5_LANES = S5_GROUPS * S5_STATE
S5_CHUNKS = 4
S5_CHUNK_IN = S5_WIDTH // S5_CHUNKS
S5_CHUNK_ST = S5_LANES // S5_CHUNKS
RWKV_WIDTH = 512
RWKV_HEAD = 64
RWKV_HEADS = 8
RWKV_LORA = 128
RWKV_COLS = 3 * RWKV_WIDTH + 64 + 64 + 128
PEER_HEADS = 8
N_KEYS = 128
PEER_TOPK = 16
PEER_HALF = 64
PEER_TOKEN_GROUP = 256
PEER_KEYS_PER_BLOCK = 8
NORM_EPS = 1e-6
GN_EPS = 64e-5


def _params(sem):
    return pltpu.CompilerParams(dimension_semantics=sem, vmem_limit_bytes=VMEM_LIMIT_BYTES)


def _full(shape):
    return pl.BlockSpec(shape, lambda *_: (0,) * len(shape))


def _dot(a, b, precision=None):
    return jnp.dot(a, b, precision=precision, preferred_element_type=F32)


def _split_bf16(x):
    hi = x.astype(BF16)
    return hi, (x - hi.astype(F32)).astype(BF16)


def _dot_split(a, b_hi, b_lo):
    a_hi, a_lo = _split_bf16(a)
    return _dot(a_hi, b_hi) + (_dot(a_lo, b_hi) + _dot(a_hi, b_lo))


def _gelu(x):
    return 0.5 * x * (1.0 + lax.erf(x * (2.0 ** -0.5)))


def _rms(x, g):
    return x * lax.rsqrt(jnp.mean(x * x, axis=-1, keepdims=True) + NORM_EPS) * g


def _modulate(h, shift, scale, n_seq):
    rows, d = h.shape
    h3 = h.reshape(rows // n_seq, n_seq, d)
    return (h3 * (1.0 + scale)[None] + shift[None]).reshape(rows, d)


def _gated(h, gate, n_seq):
    rows, d = h.shape
    return (h.reshape(rows // n_seq, n_seq, d) * gate[None]).reshape(rows, d)


def _split_kernel(w_ref, o_ref):
    o_ref[0], o_ref[1] = _split_bf16(w_ref[...])


def _split_call(w):
    return pl.pallas_call(
        _split_kernel,
        out_shape=jax.ShapeDtypeStruct((2,) + w.shape, BF16),
        compiler_params=pltpu.CompilerParams(vmem_limit_bytes=VMEM_LIMIT_BYTES),
        name="split_weight",
    )(w)


def _ada_kernel(c_ref, w_ref, b_ref, o_ref):
    s = jax.nn.silu(c_ref[...])
    o_ref[...] = _dot(s, w_ref[...], HIGHEST) + b_ref[...]


def _ada_call(c_all, w_ada, b_ada):
    n, d = c_all.shape
    cols = w_ada.shape[1]
    return pl.pallas_call(
        _ada_kernel,
        grid=(cols // d,),
        in_specs=[_full((n, d)), pl.BlockSpec((d, d), lambda j: (0, j)), pl.BlockSpec((1, d), lambda j: (0, j))],
        out_specs=pl.BlockSpec((n, d), lambda j: (0, j)),
        out_shape=jax.ShapeDtypeStruct((n, cols), F32),
        compiler_params=_params(("arbitrary",)),
        name="adaln_mod",
    )(c_all, w_ada, b_ada.reshape(1, cols))


def _time_major_rows(x_ref):
    if len(x_ref.shape) == 2:
        return x_ref[...]
    n_seq, steps, d = x_ref.shape
    return jnp.swapaxes(x_ref[...], 0, 1).reshape(n_seq * steps, d)


def _token_spec(x, n_seq, tile):
    if x.ndim == 2:
        return pl.BlockSpec((tile, x.shape[1]), lambda i, *_: (i, 0))
    return pl.BlockSpec((n_seq, tile // n_seq, x.shape[2]), lambda i, *_: (0, i, 0))


def _inproj_kernel(x_ref, mod_ref, g_ref, w_ref, u_ref, p_ref, *, n_seq):
    h = _modulate(_rms(_time_major_rows(x_ref), g_ref[...]), mod_ref[0], mod_ref[1], n_seq)
    proj = _dot(h.astype(BF16), w_ref[...])
    u_ref[...] = proj[:, :S5_WIDTH]
    p_ref[...] = proj[:, S5_WIDTH:]


def _inproj_call(x, mod6, norm_g, w_in_bf, n_seq, tile):
    d = x.shape[-1]
    rows = x.size // d
    return pl.pallas_call(
        functools.partial(_inproj_kernel, n_seq=n_seq),
        grid=(rows // tile,),
        in_specs=[_token_spec(x, n_seq, tile), _full(mod6.shape), _full((1, d)),
                  _full(w_in_bf.shape)],
        out_specs=[pl.BlockSpec((tile, S5_WIDTH), lambda i: (i, 0)),
                   pl.BlockSpec((tile, RWKV_COLS), lambda i: (i, 0))],
        out_shape=[jax.ShapeDtypeStruct((rows, S5_WIDTH), F32), jax.ShapeDtypeStruct((rows, RWKV_COLS), F32)],
        compiler_params=_params(("arbitrary",)),
        name="norm1_inproj",
    )(x, mod6, norm_g.reshape(1, d), w_in_bf)


def _s5_prep_kernel(are_ref, aim_ref, ldt_ref, bre_ref, bim_ref, lbr_ref, lbi_ref, wre_ref, wim_ref):
    lam_re, lam_im = are_ref[...], aim_ref[...]
    dt = jnp.exp(ldt_ref[...])
    mag = jnp.exp(lam_re * dt)
    ang = lam_im * dt
    lb_re, lb_im = mag * jnp.cos(ang), mag * jnp.sin(ang)
    den = lam_re * lam_re + lam_im * lam_im
    n_re, n_im = lb_re - 1.0, lb_im
    coef_re = (n_re * lam_re + n_im * lam_im) / den
    coef_im = (n_im * lam_re - n_re * lam_im) / den
    lbr_ref[...] = lb_re
    lbi_ref[...] = lb_im
    for c in range(S5_CHUNKS):
        cr = coef_re[:, c * S5_CHUNK_ST:(c + 1) * S5_CHUNK_ST]
        ci = coef_im[:, c * S5_CHUNK_ST:(c + 1) * S5_CHUNK_ST]
        wre_ref[c] = (cr * bre_ref[c] - ci * bim_ref[c]).astype(BF16)
        wim_ref[c] = (cr * bim_ref[c] + ci * bre_ref[c]).astype(BF16)


def _s5_prep_call(a_re, a_im, log_dt, b_re, b_im):
    row = lambda a: a.reshape(1, S5_LANES)
    ldt = jnp.repeat(log_dt, S5_STATE).reshape(1, S5_LANES)
    w_shape = (S5_CHUNKS, S5_CHUNK_IN, S5_CHUNK_ST)
    return pl.pallas_call(
        _s5_prep_kernel,
        out_shape=[jax.ShapeDtypeStruct((1, S5_LANES), F32)] * 2 + [jax.ShapeDtypeStruct(w_shape, BF16)] * 2,
        compiler_params=pltpu.CompilerParams(vmem_limit_bytes=VMEM_LIMIT_BYTES),
        name="s5_discretise",
    )(row(a_re), row(a_im), ldt, _s5_in_blockdiag(b_re), _s5_in_blockdiag(b_im))


def _s5_in_blockdiag(b):
    gpc = S5_GROUPS // S5_CHUNKS
    bt = jnp.transpose(b, (0, 2, 1)).reshape(S5_CHUNKS, gpc, S5_GROUP, S5_STATE)
    eye = jnp.eye(gpc, dtype=b.dtype)
    bd = bt[:, :, :, None, :] * eye[None, :, None, :, None]
    return bd.reshape(S5_CHUNKS, S5_CHUNK_IN, S5_CHUNK_ST)


def _s5_out_blockdiag(c):
    gpc = S5_GROUPS // S5_CHUNKS
    ct = jnp.transpose(c, (0, 2, 1)).reshape(S5_CHUNKS, gpc, S5_STATE, S5_GROUP)
    eye = jnp.eye(gpc, dtype=c.dtype)
    bd = ct[:, :, :, None, :] * eye[None, :, None, :, None]
    return bd.reshape(S5_CHUNKS, S5_CHUNK_ST, S5_CHUNK_IN)


def _s5_kernel(u_ref, h0r_ref, h0i_ref, lbr_ref, lbi_ref, wre_ref, wim_ref, cre_ref, cim_ref, d_ref, wglu_ref,
               bglu_ref, y_ref, hr_out, hi_out, re_s, im_s, str_s, sti_s, *, n_seq, steps):
    chunk = pl.program_id(0)

    @pl.when(chunk == 0)
    def _():
        str_s[...] = h0r_ref[...]
        sti_s[...] = h0i_ref[...]

    u = u_ref[...]
    for c in range(S5_CHUNKS):
        uc = u[:, c * S5_CHUNK_IN:(c + 1) * S5_CHUNK_IN].astype(BF16)
        re_s[:, c * S5_CHUNK_ST:(c + 1) * S5_CHUNK_ST] = _dot(uc, wre_ref[c])
        im_s[:, c * S5_CHUNK_ST:(c + 1) * S5_CHUNK_ST] = _dot(uc, wim_ref[c])

    def seq_block(rb, carry):
        r0 = pl.multiple_of(rb * SUBLANES, SUBLANES)
        for c in range(S5_CHUNKS):
            lanes = slice(c * S5_CHUNK_ST, (c + 1) * S5_CHUNK_ST)
            lr = jnp.broadcast_to(lbr_ref[:, lanes], (SUBLANES, S5_CHUNK_ST))
            li = jnp.broadcast_to(lbi_ref[:, lanes], (SUBLANES, S5_CHUNK_ST))

            def step(t, h):
                hr, hi = h
                row = pl.multiple_of(t * n_seq + r0, SUBLANES)
                nr = lr * hr - li * hi + re_s[pl.ds(row, SUBLANES), lanes]
                ni = lr * hi + li * hr + im_s[pl.ds(row, SUBLANES), lanes]
                re_s[pl.ds(row, SUBLANES), lanes] = nr
                im_s[pl.ds(row, SUBLANES), lanes] = ni
                return nr, ni

            h0 = (str_s[pl.ds(r0, SUBLANES), lanes], sti_s[pl.ds(r0, SUBLANES), lanes])
            hr, hi = lax.fori_loop(0, steps, step, h0, unroll=min(steps, 8))
            str_s[pl.ds(r0, SUBLANES), lanes] = hr
            sti_s[pl.ds(r0, SUBLANES), lanes] = hi
        return carry

    lax.fori_loop(0, n_seq // SUBLANES, seq_block, 0)

    ys = []
    for c in range(S5_CHUNKS):
        lanes = slice(c * S5_CHUNK_ST, (c + 1) * S5_CHUNK_ST)
        ys.append(_dot(re_s[:, lanes].astype(BF16), cre_ref[c]) - _dot(im_s[:, lanes].astype(BF16), cim_ref[c]))
    y = jnp.concatenate(ys, axis=-1) + d_ref[...] * u
    y = _gelu(y)
    y_ref[...] = y * jax.nn.sigmoid(_dot(y.astype(BF16), wglu_ref[...]) + bglu_ref[...])

    @pl.when(chunk == pl.num_programs(0) - 1)
    def _():
        hr_out[...] = str_s[...]
        hi_out[...] = sti_s[...]


def _s5_call(u_rows, h0_re, h0_im, consts, n_seq, steps):
    rows = u_rows.shape[0]
    tile = n_seq * steps
    lb_re, lb_im, w_re, w_im, c_re, c_im, d_skip, w_glu_bf, b_glu = consts
    state = jax.ShapeDtypeStruct((n_seq, S5_LANES), F32)
    args = (u_rows, h0_re, h0_im, lb_re, lb_im, w_re, w_im, c_re, c_im, d_skip, w_glu_bf, b_glu)
    in_specs = [pl.BlockSpec((tile, S5_WIDTH), lambda i: (i, 0))] + [_full(a.shape) for a in args[1:]]
    return pl.pallas_call(
        functools.partial(_s5_kernel, n_seq=n_seq, steps=steps),
        grid=(rows // tile,),
        in_specs=in_specs,
        out_specs=[pl.BlockSpec((tile, S5_WIDTH), lambda i: (i, 0)), _full(state.shape), _full(state.shape)],
        out_shape=[jax.ShapeDtypeStruct((rows, S5_WIDTH), F32), state, state],
        scratch_shapes=[pltpu.VMEM((tile, S5_LANES), F32), pltpu.VMEM((tile, S5_LANES), F32),
                        pltpu.VMEM((n_seq, S5_LANES), F32), pltpu.VMEM((n_seq, S5_LANES), F32)],
        compiler_params=_params(("arbitrary",)),
        name="s5_mixer",
    )(*args)


def _head_sum(x, ones_ref):
    hi, lo = _split_bf16(x)
    return _dot(hi, ones_ref[...]) + _dot(lo, ones_ref[...])


def _rwkv_pre_kernel(p_ref, prev_ref, shift_ref, mu_ref, w0_ref, w2_ref, a0_ref, a2_ref, g2_ref, kk_ref, ka_ref,
                     ones_ref, r_out, w_out, k_out, v_out, kk_out, kka_out, g_out, *, n_seq):
    p = p_ref[...]
    tile = p.shape[0]
    head = jnp.where(pl.program_id(0) == 0, shift_ref[...], prev_ref[...])
    p_prev = head if tile == n_seq else jnp.concatenate([head, p[:tile - n_seq]], axis=0)
    ps = p + (p_prev - p) * mu_ref[...]
    w = RWKV_WIDTH
    r, k, v = ps[:, :w], ps[:, w:2 * w], ps[:, 2 * w:3 * w]
    lo = ps[:, 3 * w:3 * w + RWKV_LORA]
    g_lo = ps[:, 3 * w + RWKV_LORA:]
    w_raw = -jax.nn.softplus(-(w0_ref[...] + _dot_split(jnp.tanh(lo), w2_ref[0], w2_ref[1]))) - 0.5
    a = jax.nn.sigmoid(a0_ref[...] + _dot_split(lo, a2_ref[0], a2_ref[1]))
    kk = k * kk_ref[...]
    norm = jnp.sqrt(_head_sum(kk * kk, ones_ref))
    kk = kk / jnp.maximum(norm, 1e-12)
    r_out[...] = r
    w_out[...] = jnp.exp(-jnp.exp(w_raw))
    k_out[...] = k * (1.0 + (a - 1.0) * ka_ref[...])
    v_out[...] = v
    kk_out[...] = kk
    kka_out[...] = kk * a
    g_out[...] = _dot_split(jax.nn.sigmoid(g_lo), g2_ref[0], g2_ref[1])


def _rwkv_pre_call(p_rows, shift0, consts, n_seq, tile):
    rows = p_rows.shape[0]
    per = tile // n_seq
    vec = jax.ShapeDtypeStruct((rows, RWKV_WIDTH), F32)
    in_specs = [pl.BlockSpec((tile, RWKV_COLS), lambda i: (i, 0)),
                pl.BlockSpec((n_seq, RWKV_COLS), lambda i: (jnp.maximum(i * per - 1, 0), 0)),
                _full(shift0.shape)] + [_full(a.shape) for a in consts]
    return pl.pallas_call(
        functools.partial(_rwkv_pre_kernel, n_seq=n_seq),
        grid=(rows // tile,),
        in_specs=in_specs,
        out_specs=[pl.BlockSpec((tile, RWKV_WIDTH), lambda i: (i, 0))] * 7,
        out_shape=[vec] * 7,
        compiler_params=_params(("arbitrary",)),
        name="rwkv_prologue",
    )(p_rows, p_rows, shift0, *consts)


def _sublane_allsum(p):
    p = p + pltpu.roll(p, 4, 0)
    p = p + pltpu.roll(p, 2, 0)
    return p + pltpu.roll(p, 1, 0)


def _pairs_from_rows(x, nb):
    cols = [x[:, c * LANES:(c + 1) * LANES] for c in range(RWKV_WIDTH // LANES)]
    m = jnp.concatenate(cols * (LANES // (4 * nb)), axis=0)
    mt = m.T
    lane = lax.broadcasted_iota(jnp.int32, (RWKV_HEAD, LANES), 1)
    even_head = (lane // nb) % RWKV_HEADS < RWKV_HEADS // 2
    return jnp.where(even_head, mt[:RWKV_HEAD], pltpu.roll(mt[RWKV_HEAD:], 4 * nb, 1))


def _rows_from_pairs(y, nb):
    pairs = RWKV_HEADS * nb
    shifts = [(-(copy * pairs + parity * pairs // 2)) % LANES
              for parity in range(2) for copy in range(LANES // pairs)]
    mt = jnp.concatenate([y if s == 0 else pltpu.roll(y, s, 1) for s in shifts], axis=0)
    m = mt.T
    return jnp.concatenate([m[c * nb:(c + 1) * nb, :] for c in range(RWKV_WIDTH // LANES)], axis=1)


def _rwkv_scan_kernel(r_ref, w_ref, k_ref, kk_ref, kka_ref, v_ref, s0_ref, y_ref, sout_ref, s_s, sa_s, ka_s, kb_s,
                      va_s, vb_s, *, steps, n_v, nb):
    tc = pl.program_id(1)

    @pl.when(tc == 0)
    def _():
        s_s[...] = s0_ref[...]

    row_id = lax.broadcasted_iota(jnp.int32, (SUBLANES, LANES), 0)
    lane_id = lax.broadcasted_iota(jnp.int32, (n_v, LANES), 1)
    split = lambda x: x.reshape(RWKV_HEAD // SUBLANES, SUBLANES, LANES)
    k_refs = (r_ref, w_ref, k_ref, kk_ref, kka_ref)

    def stage(t, k_s, v_s):
        for i, ref in enumerate(k_refs):
            k_s[i] = _pairs_from_rows(ref[t], nb)
        full = _pairs_from_rows(v_ref[t], nb)
        v_s[...] = full if n_v == RWKV_HEAD else jnp.where(lane_id < LANES // 2, full[:n_v], full[n_v:])

    def advance(t, k_s, v_s):
        kk = split(k_s[3])
        for v in range(n_v):
            sa_s[v] = -_sublane_allsum(jnp.sum(split(s_s[v]) * kk, axis=0))
        r, w, k, kka = split(k_s[0]), split(k_s[1]), split(k_s[2]), split(k_s[4])
        y_tiles = []
        for vb in range(n_v // SUBLANES):
            v_tile = v_s[vb * SUBLANES:(vb + 1) * SUBLANES, :]
            y_tile = jnp.zeros((SUBLANES, LANES), F32)
            for j in range(SUBLANES):
                v = vb * SUBLANES + j
                v_row = jnp.broadcast_to(v_tile[j:j + 1, :], (SUBLANES, LANES))
                s = split(s_s[v]) * w + sa_s[v][None] * kka + v_row[None] * k
                s_s[v] = s.reshape(RWKV_HEAD, LANES)
                y_tile = jnp.where(row_id == j, _sublane_allsum(jnp.sum(s * r, axis=0)), y_tile)
            y_tiles.append(y_tile)
        y_ref[t] = _rows_from_pairs(jnp.concatenate(y_tiles, axis=0), nb)

    stage(0, ka_s, va_s)

    def two_steps(i, carry):
        t = 2 * i
        stage(t + 1, kb_s, vb_s)
        advance(t, ka_s, va_s)
        stage(jnp.minimum(t + 2, steps - 1), ka_s, va_s)
        advance(t + 1, kb_s, vb_s)
        return carry

    lax.fori_loop(0, steps // 2, two_steps, 0)

    @pl.when(tc == pl.num_programs(1) - 1)
    def _():
        sout_ref[...] = s_s[...]


def _rwkv_scan_call(r, w, k, kk, kka, v, s0, n_seq, steps):
    rows = r.shape[0]
    n_t = rows // n_seq
    n_v, _, lanes = s0.shape
    nb = n_seq // (lanes // LANES)
    assert steps % 2 == 0 and n_t % steps == 0
    as_steps = lambda a: a.reshape(n_t, n_seq, RWKV_WIDTH)
    xspec = pl.BlockSpec((steps, nb, RWKV_WIDTH), lambda g, i: (i, g, 0))
    sspec = pl.BlockSpec((n_v, RWKV_HEAD, LANES), lambda g, i: (0, 0, g))
    stage_k = pltpu.VMEM((5, RWKV_HEAD, LANES), F32)
    stage_v = pltpu.VMEM((n_v, LANES), F32)
    y, s = pl.pallas_call(
        functools.partial(_rwkv_scan_kernel, steps=steps, n_v=n_v, nb=nb),
        grid=(lanes // LANES, n_t // steps),
        in_specs=[xspec] * 6 + [sspec],
        out_specs=[xspec, sspec],
        out_shape=[jax.ShapeDtypeStruct((n_t, n_seq, RWKV_WIDTH), F32), jax.ShapeDtypeStruct(s0.shape, F32)],
        scratch_shapes=[pltpu.VMEM((n_v, RWKV_HEAD, LANES), F32), pltpu.VMEM((n_v, SUBLANES, LANES), F32),
                        stage_k, stage_k, stage_v, stage_v],
        compiler_params=_params(("arbitrary", "arbitrary")),
        name="rwkv_recurrence",
    )(*[as_steps(a) for a in (r, w, k, kk, kka, v)], s0)
    return y.reshape(rows, RWKV_WIDTH), s


def _state_to_pairs(s, nb):
    n_seq = s.shape[0]
    v_split = LANES // (RWKV_HEADS * nb)
    n_v = RWKV_HEAD // v_split
    s = s.reshape(n_seq // nb, nb, RWKV_HEADS // 2, 2, v_split, n_v, RWKV_HEAD)
    return jnp.transpose(s, (5, 6, 0, 4, 3, 2, 1)).reshape(n_v, RWKV_HEAD, n_seq // nb * LANES)


def _state_from_pairs(s, n_seq, nb):
    v_split = LANES // (RWKV_HEADS * nb)
    n_v = RWKV_HEAD // v_split
    s = s.reshape(n_v, RWKV_HEAD, n_seq // nb, v_split, 2, RWKV_HEADS // 2, nb)
    return jnp.transpose(s, (2, 6, 5, 4, 3, 0, 1)).reshape(n_seq, RWKV_HEADS, RWKV_HEAD, RWKV_HEAD)


def _mixout_kernel(x_ref, ys5_ref, yrw_ref, r_ref, k_ref, v_ref, g_ref, mod_ref, gnw_ref, gnb_ref, rk_ref,
                   ones_ref, wout_ref, x1_ref, *, n_seq):
    y = yrw_ref[...]
    inv_n = 1.0 / RWKV_HEAD
    mean = _head_sum(y, ones_ref) * inv_n
    yc = y - mean
    var = _head_sum(yc * yc, ones_ref) * inv_n
    y = yc * lax.rsqrt(var + GN_EPS) * gnw_ref[...] + gnb_ref[...]
    v = v_ref[...]
    y = y + _head_sum(r_ref[...] * k_ref[...] * rk_ref[...], ones_ref) * v
    y = y * g_ref[...]
    mix = jnp.concatenate([ys5_ref[...], y], axis=-1).astype(BF16)
    x1_ref[...] = _time_major_rows(x_ref) + _gated(_dot(mix, wout_ref[...]), mod_ref[2], n_seq)


def _mixout_call(x, y_s5, y_rw, r, k, v, g, mod6, consts, n_seq, tile):
    d = x.shape[-1]
    rows = x.size // d
    wide = pl.BlockSpec((tile, d), lambda i: (i, 0))
    half = pl.BlockSpec((tile, RWKV_WIDTH), lambda i: (i, 0))
    return pl.pallas_call(
        functools.partial(_mixout_kernel, n_seq=n_seq),
        grid=(rows // tile,),
        in_specs=[_token_spec(x, n_seq, tile)] + [half] * 6 + [_full(mod6.shape)] + [_full(a.shape) for a in consts],
        out_specs=wide,
        out_shape=jax.ShapeDtypeStruct((rows, d), F32),
        compiler_params=_params(("arbitrary",)),
        name="mixer_out",
    )(x, y_s5, y_rw, r, k, v, g, mod6, *consts)


def _sort16_pairs():
    pairs = []
    k = 2
    while k <= PEER_TOPK:
        j = k // 2
        while j >= 1:
            pairs += [(i, i ^ j, (i & k) == 0) for i in range(PEER_TOPK) if i ^ j > i]
            j //= 2
        k *= 2
    return pairs


def _bitonic_merge_desc(x):
    x = list(x)
    j = PEER_TOPK // 2
    while j >= 1:
        for i in range(PEER_TOPK):
            if not i & j:
                x[i], x[i | j] = jnp.maximum(x[i], x[i | j]), jnp.minimum(x[i], x[i | j])
        j //= 2
    return x


def _top16_desc(slabs):
    x = list(slabs)
    for i, l, i_max in _sort16_pairs():
        hi, lo = jnp.maximum(x[i], x[l]), jnp.minimum(x[i], x[l])
        x[i], x[l] = (hi, lo) if i_max else (lo, hi)
    return _merge_sublane_lists(x)


def _merge_top16(x, y):
    return _bitonic_merge_desc([jnp.maximum(x[v], y[PEER_TOPK - 1 - v]) for v in range(PEER_TOPK)])


def _merge_sublane_lists(x):
    for shift in (1, 2, 4):
        x = _merge_top16(x, [pltpu.roll(v, shift, 0) for v in x])
    return x


def _top16_pair_sums(a, b):
    sub = lax.broadcasted_iota(jnp.int32, a[0].shape, 0)
    b_lo = b[0]
    for s in range(1, SUBLANES):
        b_lo = jnp.where(sub == s, b[s], b_lo)
    low = [a[i] + b_lo for i in range(PEER_TOPK)]
    high = [a[0] + b[j] for j in range(SUBLANES, PEER_TOPK)]
    padded = high + [jnp.full_like(a[0], -jnp.inf)] * (PEER_TOPK - len(high))
    return _merge_top16(_merge_sublane_lists(low), padded), low, high


def _count_at_least(slabs, bound):
    count = jnp.zeros_like(bound)
    for s in slabs:
        count = count + jnp.where(s >= bound, 1.0, 0.0)
    return _sublane_allsum(count)


def _peer_select(k1_ref, k2_ref, q_s, a_s, b_s, e1_s, e2_s, thr_s, tie_s):
    tg = q_s.shape[2]
    slabs = lambda s: [s[v * SUBLANES:(v + 1) * SUBLANES, :] for v in range(N_KEYS // SUBLANES)]
    tiled = lambda row: jnp.concatenate([row] * (N_KEYS // SUBLANES), axis=0)

    def head(h, carry):
        q = q_s[h]
        s1 = _dot(k1_ref[h], q[:PEER_HALF], HIGHEST)
        s2 = _dot(k2_ref[h], q[PEER_HALF:], HIGHEST)
        top1 = _top16_desc(slabs(s1))
        top2 = _top16_desc(slabs(s2))
        topc, sums_low, sums_high = _top16_pair_sums(top1, top2)
        z = jnp.ones_like(topc[0])
        for i in range(1, PEER_TOPK):
            z = z + jnp.exp(topc[i] - topc[0])
        a = jnp.where(s1 >= tiled(top1[-1]), s1, -jnp.inf)
        b = jnp.where(s2 >= tiled(top2[-1]), s2, -jnp.inf)
        a_s[h] = a
        b_s[h] = b
        e1_s[h] = jnp.exp(a - tiled(top1[0])) / tiled(z)
        e2_s[h] = jnp.exp(b - tiled(top2[0]))
        thr_s[h] = topc[-1]
        n_pairs = _count_at_least(sums_low, topc[-1])
        for e in sums_high:
            n_pairs = n_pairs + jnp.where(e >= topc[-1], 1.0, 0.0)
        tied = jnp.where(n_pairs == PEER_TOPK, 0.0, 1.0)
        for values, bound in ((slabs(s1), top1[-1]), (slabs(s2), top2[-1])):
            tied = jnp.maximum(tied, jnp.where(_count_at_least(values, bound) == PEER_TOPK, 0.0, 1.0))
        tie_s[...] = jnp.maximum(tie_s[...], tied)
        return carry

    lax.fori_loop(0, PEER_HEADS, head, 0)


def _stable_top16(x, row_id):
    vals, rows = [], []
    for _ in range(PEER_TOPK):
        m = jnp.max(x, axis=0, keepdims=True)
        r = jnp.min(jnp.where(x == m, row_id, float(x.shape[0])), axis=0, keepdims=True)
        vals.append(m)
        rows.append(r)
        x = jnp.where(row_id == r, -jnp.inf, x)
    return vals, rows


def _peer_select_exact(k1_ref, k2_ref, q_s, a_s, b_s, e1_s, e2_s, thr_s, pa_s, pb_s, pthr_s, lst_s, cand_s):
    tg = q_s.shape[2]
    key_id = lax.broadcasted_iota(jnp.int32, (N_KEYS, tg), 0).astype(F32)
    pair_id = lax.broadcasted_iota(jnp.int32, (PEER_TOPK * PEER_TOPK, tg), 0).astype(F32)

    def head(h, carry):
        q = q_s[h]
        s1 = _dot(k1_ref[h], q[:PEER_HALF], HIGHEST)
        s2 = _dot(k2_ref[h], q[PEER_HALF:], HIGHEST)
        v1, i1 = _stable_top16(s1, key_id)
        v2, i2 = _stable_top16(s2, key_id)
        rank1 = jnp.full((N_KEYS, tg), float(PEER_TOPK), F32)
        rank2 = rank1
        for r in range(PEER_TOPK):
            rank1 = jnp.where(key_id == i1[r], float(r), rank1)
            rank2 = jnp.where(key_id == i2[r], float(r), rank2)
            lst_s[r:r + 1, :] = v2[r]
        best2 = lst_s[...]
        for r in range(PEER_TOPK):
            cand_s[r * PEER_TOPK:(r + 1) * PEER_TOPK, :] = v1[r] + best2
        vc, pc = _stable_top16(cand_s[...], pair_id)
        z = jnp.ones_like(vc[0])
        for r in range(1, PEER_TOPK):
            z = z + jnp.exp(vc[r] - vc[0])
        a = jnp.where(rank1 < PEER_TOPK, s1, -jnp.inf)
        b = jnp.where(rank2 < PEER_TOPK, s2, -jnp.inf)
        a_s[h] = a
        b_s[h] = b
        e1_s[h] = jnp.exp(a - v1[0]) / z
        e2_s[h] = jnp.exp(b - v2[0])
        pa_s[h] = rank1 * float(PEER_TOPK)
        pb_s[h] = rank2
        thr_s[h] = jnp.broadcast_to(vc[-1], (SUBLANES, tg))
        pthr_s[h] = jnp.broadcast_to(pc[-1], (SUBLANES, tg))
        return carry

    lax.fori_loop(0, PEER_HEADS, head, 0)


def _peer_gates(g, key0, act_ref, pt_ref, a_s, b_s, e1_s, e2_s, thr_s):
    tg = act_ref.shape[2]
    for lg in range(tg // LANES):
        ls = slice(lg * LANES, (lg + 1) * LANES)
        a_rows = [a_s[g, h, pl.ds(key0, PEER_KEYS_PER_BLOCK), ls] for h in range(PEER_HEADS)]
        e_rows = [e1_s[g, h, pl.ds(key0, PEER_KEYS_PER_BLOCK), ls] for h in range(PEER_HEADS)]
        for i in range(PEER_KEYS_PER_BLOCK):
            gate = jnp.zeros((N_KEYS, LANES), F32)
            for h in range(PEER_HEADS):
                score = b_s[g, h, :, ls] + a_rows[h][i:i + 1, :]
                weight = e2_s[g, h, :, ls] * e_rows[h][i:i + 1, :]
                gate = gate + jnp.where(score >= thr_s[g, h, 0:1, ls], weight, 0.0)
            act = act_ref[g, i * N_KEYS:(i + 1) * N_KEYS, ls]
            pt_ref[g, i * N_KEYS:(i + 1) * N_KEYS, ls] = (gate * act).astype(BF16)


def _peer_gates_exact(g, key0, act_ref, pt_ref, a_s, b_s, e1_s, e2_s, thr_s, pa_s, pb_s, pthr_s):
    tg = act_ref.shape[2]
    rows = pl.ds(key0, PEER_KEYS_PER_BLOCK)
    for lg in range(tg // LANES):
        ls = slice(lg * LANES, (lg + 1) * LANES)
        for i in range(PEER_KEYS_PER_BLOCK):
            def head(h, gate):
                score = b_s[g, h, :, ls] + a_s[g, h, rows, ls][i:i + 1, :]
                pos = pb_s[g, h, :, ls] + pa_s[g, h, rows, ls][i:i + 1, :]
                weight = e2_s[g, h, :, ls] * e1_s[g, h, rows, ls][i:i + 1, :]
                thr = thr_s[g, h, 0:1, ls]
                tied_in = jnp.where(score == thr, jnp.where(pos <= pthr_s[g, h, 0:1, ls], weight, 0.0), 0.0)
                return gate + jnp.where(score > thr, weight, tied_in)

            gate = lax.fori_loop(0, PEER_HEADS, head, jnp.zeros((N_KEYS, LANES), F32))
            act = act_ref[g, i * N_KEYS:(i + 1) * N_KEYS, ls]
            pt_ref[g, i * N_KEYS:(i + 1) * N_KEYS, ls] = (gate * act).astype(BF16)


def _first_match(cases):
    (pred, fn), rest = cases[0], cases[1:]

    def hit():
        fn()

    def miss():
        if rest:
            _first_match(rest)

    lax.cond(pred, hit, miss)


def _peer_kernel(x1_ref, sh_ref, sc_ref, ga_ref, n2g_ref, wq_ref, k1_ref, k2_ref, u_ref, vt_ref, fg_ref, o_ref,
                 xt_s, q_s, a_s, b_s, e1_s, e2_s, thr_s, pa_s, pb_s, pthr_s, tie_s, lst_s, cand_s, tied_s,
                 act0_s, act1_s, pt0_s, pt1_s, acc_s, *, n_seq, n_blocks):
    s = pl.program_id(1)
    n_groups = xt_s.shape[0]
    tg = xt_s.shape[2]
    acts, pts = (act0_s, act1_s), (pt0_s, pt1_s)
    sel = (a_s, b_s, e1_s, e2_s, thr_s)
    sel_exact = sel + (pa_s, pb_s, pthr_s)

    def stages(parity, first, gates, second, exact=False):
        key0 = pl.multiple_of((s - 1) * PEER_KEYS_PER_BLOCK, PEER_KEYS_PER_BLOCK)

        def group(g, carry):
            if first:
                acts[parity][g] = _gelu(_dot(u_ref[...], xt_s[g]))
            if gates and exact:
                _peer_gates_exact(g, key0, acts[1 - parity], pts[1 - parity], *sel_exact)
            elif gates:
                _peer_gates(g, key0, acts[1 - parity], pts[1 - parity], *sel)
            if second:
                acc_s[g] += _dot(vt_ref[...], pts[parity][g])
            return carry

        lax.fori_loop(0, n_groups, group, 0)

    def first_step():
        tie_s[...] = jnp.zeros_like(tie_s)

        def queries(g):
            t0 = pl.multiple_of(g * tg, tg)
            h2 = _modulate(_rms(x1_ref[pl.ds(t0, tg), :], n2g_ref[...]), sh_ref[...], sc_ref[...], min(n_seq, tg))
            xt_s[g] = h2.T.astype(BF16)
            q_s[...] = _dot(wq_ref[...], xt_s[g]).reshape(q_s.shape)

        def prep(g, carry):
            queries(g)
            _peer_select(k1_ref, k2_ref, q_s, a_s.at[g], b_s.at[g], e1_s.at[g], e2_s.at[g], thr_s.at[g], tie_s)
            return carry

        lax.fori_loop(0, n_groups, prep, 0)
        tied_s[0] = (jnp.max(tie_s[...]) > 0.0).astype(jnp.int32)

        @pl.when(tied_s[0] != 0)
        def _():
            def prep_exact(g, carry):
                queries(g)
                _peer_select_exact(k1_ref, k2_ref, q_s, a_s.at[g], b_s.at[g], e1_s.at[g], e2_s.at[g], thr_s.at[g],
                                   pa_s.at[g], pb_s.at[g], pthr_s.at[g], lst_s, cand_s)
                return carry

            lax.fori_loop(0, n_groups, prep_exact, 0)

        acc_s[...] = jnp.zeros_like(acc_s)
        pt1_s[...] = jnp.zeros_like(pt1_s)
        stages(0, True, False, False)

    def by_tie(parity, first, gates, second):
        return lambda: lax.cond(tied_s[0] == 0, lambda: stages(parity, first, gates, second, False),
                                lambda: stages(parity, first, gates, second, True))

    def last_step():
        stages((n_blocks + 1) % 2, False, False, True)
        for g in range(n_groups):
            rows = slice(g * tg, (g + 1) * tg)
            x2 = x1_ref[rows, :] + _gated(acc_s[g].T, ga_ref[...], min(n_seq, tg))
            y = _rms(x2, fg_ref[...])
            if len(o_ref.shape) == 2:
                o_ref[rows, :] = y
            else:
                sg = tg // n_seq
                o_ref[:, g * sg:(g + 1) * sg, :] = jnp.swapaxes(y.reshape(sg, n_seq, y.shape[1]), 0, 1)

    middle = (s >= 1) & (s < n_blocks)
    _first_match([(middle & (s % 2 == 0), by_tie(0, True, True, True)),
                  (middle, by_tie(1, True, True, True)),
                  (s == 0, first_step),
                  (s == n_blocks, by_tie(n_blocks % 2, False, True, True)),
                  (s == n_blocks + 1, last_step)])


def _peer_call(x1, mod6, norm2_g, wq_t, keys1, keys2, u_bf, vt_bf, final_g, n_seq, tm, batch_major):
    rows, d = x1.shape
    shift2, scale2, gate2 = mod6[3], mod6[4], mod6[5]
    n_exp = u_bf.shape[0]
    eb = PEER_KEYS_PER_BLOCK * N_KEYS
    n_blocks = n_exp // eb
    tg = PEER_TOKEN_GROUP
    n_groups = tm // tg
    assert tm % tg == 0 and (tg % n_seq == 0 or n_seq % tg == 0) and n_exp % eb == 0 and n_blocks >= 2
    if n_seq > tg:
        raise NotImplementedError("more than PEER_TOKEN_GROUP sequences per step")
    tok = pl.BlockSpec((tm, d), lambda i, s: (i, 0))
    out = jax.ShapeDtypeStruct((n_seq, rows // n_seq, d) if batch_major else (rows, d), F32)
    sel = pltpu.VMEM((n_groups, PEER_HEADS, N_KEYS, tg), F32)
    row = pltpu.VMEM((n_groups, PEER_HEADS, SUBLANES, tg), F32)
    act = pltpu.VMEM((n_groups, eb, tg), F32)
    pt = pltpu.VMEM((n_groups, eb, tg), BF16)
    return pl.pallas_call(
        functools.partial(_peer_kernel, n_seq=n_seq, n_blocks=n_blocks),
        grid=(rows // tm, n_blocks + 2),
        in_specs=[tok, _full(gate2.shape), _full(gate2.shape), _full(gate2.shape), _full((1, d)),
                  _full(wq_t.shape), _full(keys1.shape), _full(keys2.shape),
                  pl.BlockSpec((eb, d), lambda i, s: (jnp.minimum(s, n_blocks - 1), 0)),
                  pl.BlockSpec((d, eb), lambda i, s: (0, jnp.clip(s - 2, 0, n_blocks - 1))),
                  _full((1, d))],
        out_specs=_token_spec(out, n_seq, tm),
        out_shape=out,
        scratch_shapes=[pltpu.VMEM((n_groups, d, tg), BF16), pltpu.VMEM((PEER_HEADS, 2 * PEER_HALF, tg), F32),
                        sel, sel, sel, sel, row, sel, sel, row, pltpu.VMEM((SUBLANES, tg), F32),
                        pltpu.VMEM((PEER_TOPK, tg), F32), pltpu.VMEM((PEER_TOPK * PEER_TOPK, tg), F32),
                        pltpu.SMEM((1,), jnp.int32),
                        act, act, pt, pt, pltpu.VMEM((n_groups, d, tg), F32)],
        compiler_params=_params(("arbitrary", "arbitrary")),
        name="peer_final_norm",
    )(x1, shift2, scale2, gate2, norm2_g.reshape(1, d), wq_t, keys1, keys2, u_bf, vt_bf, final_g.reshape(1, d))


def _pick(n, target):
    t = min(n, target)
    while n % t:
        t -= 1
    return t


def _layer(x_btd, mod, s5_re0, s5_im0, wkv0, shift0, prm):
    n_seq, n_t, d = x_btd.shape
    rows = n_seq * n_t
    mod6 = jnp.transpose(mod.reshape(n_seq, 6, d), (1, 0, 2))
    tile = n_seq * _pick(n_t, max(1, 512 // n_seq))
    reorder_in_kernel = (PEER_TOKEN_GROUP // n_seq) % SUBLANES == 0 and (tile // n_seq) % SUBLANES == 0
    x_in = x_btd if reorder_in_kernel else jnp.transpose(x_btd, (1, 0, 2)).reshape(rows, d)

    u_rows, p_rows = _inproj_call(x_in, mod6, prm["norm1_g"], prm["w_in_bf"], n_seq, tile)

    s5_steps = _pick(n_t, max(1, 512 // n_seq))
    y_s5, s5_re, s5_im = _s5_call(u_rows, s5_re0.reshape(n_seq, S5_LANES), s5_im0.reshape(n_seq, S5_LANES),
                                  prm["s5_consts"], n_seq, s5_steps)

    r, w, k, v, kk, kka, g = _rwkv_pre_call(p_rows, shift0, prm["rwkv_pre_consts"], n_seq, tile)
    nb = min(n_seq, LANES // RWKV_HEADS)
    y_rw, s_pairs = _rwkv_scan_call(r, w, k, kk, kka, v, _state_to_pairs(wkv0, nb), n_seq, _pick(n_t, 16))
    wkv = _state_from_pairs(s_pairs, n_seq, nb)

    x1 = _mixout_call(x_in, y_s5, y_rw, r, k, v, g, mod6, prm["mixout_consts"], n_seq, tile)

    y = _peer_call(x1, mod6, prm["norm2_g"], prm["peer_wq_t"], prm["peer_keys1"], prm["peer_keys2"],
                   prm["peer_u_bf"], prm["peer_vt_bf"], prm["final_norm_g"], n_seq, tile, reorder_in_kernel)
    if not reorder_in_kernel:
        y = jnp.transpose(y.reshape(n_t, n_seq, d), (1, 0, 2))
    shift = p_rows[rows - n_seq:]
    return (y, s5_re.reshape(n_seq, S5_GROUPS, S5_STATE), s5_im.reshape(n_seq, S5_GROUPS, S5_STATE), wkv, shift)


def kernel(x_prompt, x_sample, state_s5_re, state_s5_im, state_wkv, state_shift, c_prompt, c_sample, w_ada, b_ada, norm1_g, norm2_g, w_in, w_out, s5_a_re, s5_a_im, s5_log_dt, s5_b_re, s5_b_im, s5_c_re, s5_c_im, s5_d, w_glu, b_glu, rwkv_mu, rwkv_w0, rwkv_w2, rwkv_a0, rwkv_a2, rwkv_g2, rwkv_k_k, rwkv_k_a, rwkv_r_k, rwkv_gn_w, rwkv_gn_b, peer_w_q, peer_keys1, peer_keys2, peer_u, peer_v, final_norm_g):
    assert w_ada.shape[0] == 1, "single-layer model"
    nbp = x_prompt.shape[0]
    row = lambda a: a.reshape(1, -1)

    mod = _ada_call(jnp.concatenate([c_prompt, c_sample], axis=0).astype(F32), w_ada[0], b_ada[0])

    lb_re, lb_im, bw_re, bw_im = _s5_prep_call(s5_a_re[0], s5_a_im[0], s5_log_dt[0], s5_b_re[0], s5_b_im[0])
    head_ones = jnp.kron(jnp.eye(RWKV_HEADS, dtype=BF16), jnp.ones((RWKV_HEAD, RWKV_HEAD), BF16))
    lora_pad = jnp.zeros((RWKV_LORA // 2, RWKV_WIDTH), F32)
    prm = {
        "norm1_g": norm1_g[0],
        "w_in_bf": w_in[0].astype(BF16),
        "s5_consts": (lb_re, lb_im, bw_re, bw_im, _s5_out_blockdiag(s5_c_re[0]).astype(BF16),
                      _s5_out_blockdiag(s5_c_im[0]).astype(BF16), row(s5_d[0]), w_glu[0].astype(BF16),
                      row(b_glu[0])),
        "rwkv_pre_consts": (row(rwkv_mu[0]), row(rwkv_w0[0]),
                            _split_call(jnp.concatenate([rwkv_w2[0], lora_pad], axis=0)), row(rwkv_a0[0]),
                            _split_call(jnp.concatenate([lora_pad, rwkv_a2[0]], axis=0)), _split_call(rwkv_g2[0]),
                            row(rwkv_k_k[0]), row(rwkv_k_a[0]), head_ones),
        "mixout_consts": (row(rwkv_gn_w[0]), row(rwkv_gn_b[0]), row(rwkv_r_k[0]), head_ones,
                          w_out[0].astype(BF16)),
        "norm2_g": norm2_g[0],
        "peer_wq_t": peer_w_q[0].T.astype(BF16),
        "peer_keys1": peer_keys1[0],
        "peer_keys2": peer_keys2[0],
        "peer_u_bf": peer_u[0].astype(BF16),
        "peer_vt_bf": peer_v[0].T.astype(BF16),
        "final_norm_g": final_norm_g,
    }

    z_s5 = jnp.zeros((nbp, S5_GROUPS, S5_STATE), F32)
    z_wkv = jnp.zeros((nbp, RWKV_HEADS, RWKV_HEAD, RWKV_HEAD), F32)
    z_sh = jnp.zeros((nbp, RWKV_COLS), F32)
    yp, pr, pi, pw, psh = _layer(x_prompt.astype(F32), mod[:nbp], z_s5, z_s5, z_wkv, z_sh, prm)
    ys, sr, si, sw, ssh = _layer(x_sample.astype(F32), mod[nbp:], state_s5_re[0].astype(F32),
                                 state_s5_im[0].astype(F32), state_wkv[0].astype(F32),
                                 state_shift[0].astype(F32), prm)
    return (yp.astype(x_prompt.dtype), ys.astype(x_sample.dtype), pr[None], pi[None], pw[None], psh[None],
            sr[None], si[None], sw[None], ssh[None])
```

```python
import functools

import jax
import jax.numpy as jnp
from jax import lax
from jax.experimental import pallas as pl
from jax.experimental.pallas import tpu as pltpu

F32 = jnp.float32
BF16 = jnp.bfloat16
HIGHEST = lax.Precision.HIGHEST

LANES = 128
SUBLANES = 8
VMEM_LIMIT_BYTES = 56 * 1024 * 1024

D_MODEL = 1024
S5_WIDTH = 512
S5_GROUP = 16
S5_GROUPS = 32
S5_STATE = 64
S5_LANES = S5_GROUPS * S5_STATE
S5_CHUNKS = 4
S5_CHUNK_IN = S5_WIDTH // S5_CHUNKS
S5_CHUNK_ST = S5_LANES // S5_CHUNKS
RWKV_WIDTH = 512
RWKV_HEAD = 64
RWKV_HEADS = 8
RWKV_LORA = 128
RWKV_COLS = 3 * RWKV_WIDTH + 64 + 64 + 128
PEER_HEADS = 8
N_KEYS = 128
PEER_TOPK = 16
PEER_HALF = 64
PEER_TOKEN_GROUP = 256
PEER_KEYS_PER_BLOCK = 8
NORM_EPS = 1e-6
GN_EPS = 64e-5


def _params(sem):
    return pltpu.CompilerParams(dimension_semantics=sem, vmem_limit_bytes=VMEM_LIMIT_BYTES)


def _full(shape):
    return pl.BlockSpec(shape, lambda *_: (0,) * len(shape))


def _dot(a, b, precision=None):
    return jnp.dot(a, b, precision=precision, preferred_element_type=F32)


def _split_bf16(x):
    hi = x.astype(BF16)
    return hi, (x - hi.astype(F32)).astype(BF16)


def _dot_split(a, b_hi, b_lo):
    a_hi, a_lo = _split_bf16(a)
    return _dot(a_hi, b_hi) + (_dot(a_lo, b_hi) + _dot(a_hi, b_lo))


def _dot_split_lhs(a_hi, a_lo, b):
    b_hi, b_lo = _split_bf16(b)
    return _dot(a_hi, b_hi) + (_dot(a_lo, b_hi) + _dot(a_hi, b_lo))


def _gelu(x):
    return 0.5 * x * (1.0 + lax.erf(x * (2.0 ** -0.5)))


def _rms(x, g):
    return x * lax.rsqrt(jnp.mean(x * x, axis=-1, keepdims=True) + NORM_EPS) * g


def _modulate(h, shift, scale, n_seq):
    rows, d = h.shape
    h3 = h.reshape(rows // n_seq, n_seq, d)
    return (h3 * (1.0 + scale)[None] + shift[None]).reshape(rows, d)


def _gated(h, gate, n_seq):
    rows, d = h.shape
    return (h.reshape(rows // n_seq, n_seq, d) * gate[None]).reshape(rows, d)


def _split_kernel(w_ref, o_ref):
    o_ref[0], o_ref[1] = _split_bf16(w_ref[...])


def _split_call(w):
    return pl.pallas_call(
        _split_kernel,
        out_shape=jax.ShapeDtypeStruct((2,) + w.shape, BF16),
        compiler_params=pltpu.CompilerParams(vmem_limit_bytes=VMEM_LIMIT_BYTES),
        name="split_weight",
    )(w)


def _ada_kernel(c_ref, w_ref, b_ref, o_ref):
    s = jax.nn.silu(c_ref[...])
    o_ref[...] = _dot(s, w_ref[...], HIGHEST) + b_ref[...]


def _ada_call(c_all, w_ada, b_ada):
    n, d = c_all.shape
    cols = w_ada.shape[1]
    return pl.pallas_call(
        _ada_kernel,
        grid=(cols // d,),
        in_specs=[_full((n, d)), pl.BlockSpec((d, d), lambda j: (0, j)), pl.BlockSpec((1, d), lambda j: (0, j))],
        out_specs=pl.BlockSpec((n, d), lambda j: (0, j)),
        out_shape=jax.ShapeDtypeStruct((n, cols), F32),
        compiler_params=_params(("arbitrary",)),
        name="adaln_mod",
    )(c_all, w_ada, b_ada.reshape(1, cols))


def _time_major_rows(x_ref):
    if len(x_ref.shape) == 2:
        return x_ref[...]
    n_seq, steps, d = x_ref.shape
    return jnp.swapaxes(x_ref[...], 0, 1).reshape(n_seq * steps, d)


def _token_spec(x, n_seq, tile):
    if x.ndim == 2:
        return pl.BlockSpec((tile, x.shape[1]), lambda i, *_: (i, 0))
    return pl.BlockSpec((n_seq, tile // n_seq, x.shape[2]), lambda i, *_: (0, i, 0))


def _inproj_kernel(x_ref, mod_ref, g_ref, w_ref, u_ref, p_ref, *, n_seq):
    h = _modulate(_rms(_time_major_rows(x_ref), g_ref[...]), mod_ref[0], mod_ref[1], n_seq)
    proj = _dot(h.astype(BF16), w_ref[...])
    u_ref[...] = proj[:, :S5_WIDTH]
    p_ref[...] = proj[:, S5_WIDTH:]


def _inproj_call(x, mod6, norm_g, w_in_bf, n_seq, tile):
    d = x.shape[-1]
    rows = x.size // d
    return pl.pallas_call(
        functools.partial(_inproj_kernel, n_seq=n_seq),
        grid=(rows // tile,),
        in_specs=[_token_spec(x, n_seq, tile), _full(mod6.shape), _full((1, d)),
                  _full(w_in_bf.shape)],
        out_specs=[pl.BlockSpec((tile, S5_WIDTH), lambda i: (i, 0)),
                   pl.BlockSpec((tile, RWKV_COLS), lambda i: (i, 0))],
        out_shape=[jax.ShapeDtypeStruct((rows, S5_WIDTH), F32), jax.ShapeDtypeStruct((rows, RWKV_COLS), F32)],
        compiler_params=_params(("arbitrary",)),
        name="norm1_inproj",
    )(x, mod6, norm_g.reshape(1, d), w_in_bf)


def _s5_prep_kernel(are_ref, aim_ref, ldt_ref, bre_ref, bim_ref, lbr_ref, lbi_ref, wre_ref, wim_ref):
    lam_re, lam_im = are_ref[...], aim_ref[...]
    dt = jnp.exp(ldt_ref[...])
    mag = jnp.exp(lam_re * dt)
    ang = lam_im * dt
    lb_re, lb_im = mag * jnp.cos(ang), mag * jnp.sin(ang)
    den = lam_re * lam_re + lam_im * lam_im
    n_re, n_im = lb_re - 1.0, lb_im
    coef_re = (n_re * lam_re + n_im * lam_im) / den
    coef_im = (n_im * lam_re - n_re * lam_im) / den
    lbr_ref[...] = lb_re
    lbi_ref[...] = lb_im
    for c in range(S5_CHUNKS):
        cr = coef_re[:, c * S5_CHUNK_ST:(c + 1) * S5_CHUNK_ST]
        ci = coef_im[:, c * S5_CHUNK_ST:(c + 1) * S5_CHUNK_ST]
        wre_ref[c] = (cr * bre_ref[c] - ci * bim_ref[c]).astype(BF16)
        wim_ref[c] = (cr * bim_ref[c] + ci * bre_ref[c]).astype(BF16)


def _s5_prep_call(a_re, a_im, log_dt, b_re, b_im):
    row = lambda a: a.reshape(1, S5_LANES)
    ldt = jnp.repeat(log_dt, S5_STATE).reshape(1, S5_LANES)
    w_shape = (S5_CHUNKS, S5_CHUNK_IN, S5_CHUNK_ST)
    return pl.pallas_call(
        _s5_prep_kernel,
        out_shape=[jax.ShapeDtypeStruct((1, S5_LANES), F32)] * 2 + [jax.ShapeDtypeStruct(w_shape, BF16)] * 2,
        compiler_params=pltpu.CompilerParams(vmem_limit_bytes=VMEM_LIMIT_BYTES),
        name="s5_discretise",
    )(row(a_re), row(a_im), ldt, _s5_in_blockdiag(b_re), _s5_in_blockdiag(b_im))


def _s5_in_blockdiag(b):
    gpc = S5_GROUPS // S5_CHUNKS
    bt = jnp.transpose(b, (0, 2, 1)).reshape(S5_CHUNKS, gpc, S5_GROUP, S5_STATE)
    eye = jnp.eye(gpc, dtype=b.dtype)
    bd = bt[:, :, :, None, :] * eye[None, :, None, :, None]
    return bd.reshape(S5_CHUNKS, S5_CHUNK_IN, S5_CHUNK_ST)


def _s5_out_blockdiag(c):
    gpc = S5_GROUPS // S5_CHUNKS
    ct = jnp.transpose(c, (0, 2, 1)).reshape(S5_CHUNKS, gpc, S5_STATE, S5_GROUP)
    eye = jnp.eye(gpc, dtype=c.dtype)
    bd = ct[:, :, :, None, :] * eye[None, :, None, :, None]
    return bd.reshape(S5_CHUNKS, S5_CHUNK_ST, S5_CHUNK_IN)


def _s5_kernel(u_ref, h0r_ref, h0i_ref, lbr_ref, lbi_ref, wre_ref, wim_ref, cre_ref, cim_ref, d_ref, wglu_ref,
               bglu_ref, y_ref, hr_out, hi_out, re_s, im_s, str_s, sti_s, *, n_seq, steps):
    chunk = pl.program_id(0)

    @pl.when(chunk == 0)
    def _():
        str_s[...] = h0r_ref[...]
        sti_s[...] = h0i_ref[...]

    u = u_ref[...]
    for c in range(S5_CHUNKS):
        uc = u[:, c * S5_CHUNK_IN:(c + 1) * S5_CHUNK_IN].astype(BF16)
        re_s[:, c * S5_CHUNK_ST:(c + 1) * S5_CHUNK_ST] = _dot(uc, wre_ref[c])
        im_s[:, c * S5_CHUNK_ST:(c + 1) * S5_CHUNK_ST] = _dot(uc, wim_ref[c])

    def seq_block(rb, carry):
        r0 = pl.multiple_of(rb * SUBLANES, SUBLANES)
        for c in range(S5_CHUNKS):
            lanes = slice(c * S5_CHUNK_ST, (c + 1) * S5_CHUNK_ST)
            lr = jnp.broadcast_to(lbr_ref[:, lanes], (SUBLANES, S5_CHUNK_ST))
            li = jnp.broadcast_to(lbi_ref[:, lanes], (SUBLANES, S5_CHUNK_ST))

            def step(t, h):
                hr, hi = h
                row = pl.multiple_of(t * n_seq + r0, SUBLANES)
                nr = lr * hr - li * hi + re_s[pl.ds(row, SUBLANES), lanes]
                ni = lr * hi + li * hr + im_s[pl.ds(row, SUBLANES), lanes]
                re_s[pl.ds(row, SUBLANES), lanes] = nr
                im_s[pl.ds(row, SUBLANES), lanes] = ni
                return nr, ni

            h0 = (str_s[pl.ds(r0, SUBLANES), lanes], sti_s[pl.ds(r0, SUBLANES), lanes])
            hr, hi = lax.fori_loop(0, steps, step, h0, unroll=min(steps, 8))
            str_s[pl.ds(r0, SUBLANES), lanes] = hr
            sti_s[pl.ds(r0, SUBLANES), lanes] = hi
        return carry

    lax.fori_loop(0, n_seq // SUBLANES, seq_block, 0)

    ys = []
    for c in range(S5_CHUNKS):
        lanes = slice(c * S5_CHUNK_ST, (c + 1) * S5_CHUNK_ST)
        ys.append(_dot(re_s[:, lanes].astype(BF16), cre_ref[c]) - _dot(im_s[:, lanes].astype(BF16), cim_ref[c]))
    y = jnp.concatenate(ys, axis=-1) + d_ref[...] * u
    y = _gelu(y)
    y_ref[...] = y * jax.nn.sigmoid(_dot(y.astype(BF16), wglu_ref[...]) + bglu_ref[...])

    @pl.when(chunk == pl.num_programs(0) - 1)
    def _():
        hr_out[...] = str_s[...]
        hi_out[...] = sti_s[...]


def _s5_call(u_rows, h0_re, h0_im, consts, n_seq, steps):
    rows = u_rows.shape[0]
    tile = n_seq * steps
    lb_re, lb_im, w_re, w_im, c_re, c_im, d_skip, w_glu_bf, b_glu = consts
    state = jax.ShapeDtypeStruct((n_seq, S5_LANES), F32)
    args = (u_rows, h0_re, h0_im, lb_re, lb_im, w_re, w_im, c_re, c_im, d_skip, w_glu_bf, b_glu)
    in_specs = [pl.BlockSpec((tile, S5_WIDTH), lambda i: (i, 0))] + [_full(a.shape) for a in args[1:]]
    return pl.pallas_call(
        functools.partial(_s5_kernel, n_seq=n_seq, steps=steps),
        grid=(rows // tile,),
        in_specs=in_specs,
        out_specs=[pl.BlockSpec((tile, S5_WIDTH), lambda i: (i, 0)), _full(state.shape), _full(state.shape)],
        out_shape=[jax.ShapeDtypeStruct((rows, S5_WIDTH), F32), state, state],
        scratch_shapes=[pltpu.VMEM((tile, S5_LANES), F32), pltpu.VMEM((tile, S5_LANES), F32),
                        pltpu.VMEM((n_seq, S5_LANES), F32), pltpu.VMEM((n_seq, S5_LANES), F32)],
        compiler_params=_params(("arbitrary",)),
        name="s5_mixer",
    )(*args)


def _head_sum(x, ones_ref):
    hi, lo = _split_bf16(x)
    return _dot(hi, ones_ref[...]) + _dot(lo, ones_ref[...])


def _rwkv_pre_kernel(p_ref, prev_ref, shift_ref, mu_ref, w0_ref, w2_ref, a0_ref, a2_ref, g2_ref, kk_ref, ka_ref,
                     ones_ref, r_out, w_out, k_out, v_out, kk_out, kka_out, g_out, *, n_seq):
    p = p_ref[...]
    tile = p.shape[0]
    head = jnp.where(pl.program_id(0) == 0, shift_ref[...], prev_ref[...])
    p_prev = head if tile == n_seq else jnp.concatenate([head, p[:tile - n_seq]], axis=0)
    ps = p + (p_prev - p) * mu_ref[...]
    w = RWKV_WIDTH
    r, k, v = ps[:, :w], ps[:, w:2 * w], ps[:, 2 * w:3 * w]
    lo = ps[:, 3 * w:3 * w + RWKV_LORA]
    g_lo = ps[:, 3 * w + RWKV_LORA:]
    w_raw = -jax.nn.softplus(-(w0_ref[...] + _dot_split(jnp.tanh(lo), w2_ref[0], w2_ref[1]))) - 0.5
    a = jax.nn.sigmoid(a0_ref[...] + _dot_split(lo, a2_ref[0], a2_ref[1]))
    kk = k * kk_ref[...]
    norm = jnp.sqrt(_head_sum(kk * kk, ones_ref))
    kk = kk / jnp.maximum(norm, 1e-12)
    r_out[...] = r
    w_out[...] = jnp.exp(-jnp.exp(w_raw))
    k_out[...] = k * (1.0 + (a - 1.0) * ka_ref[...])
    v_out[...] = v
    kk_out[...] = kk
    kka_out[...] = kk * a
    g_out[...] = _dot_split(jax.nn.sigmoid(g_lo), g2_ref[0], g2_ref[1])


def _rwkv_pre_call(p_rows, shift0, consts, n_seq, tile):
    rows = p_rows.shape[0]
    per = tile // n_seq
    vec = jax.ShapeDtypeStruct((rows, RWKV_WIDTH), F32)
    in_specs = [pl.BlockSpec((tile, RWKV_COLS), lambda i: (i, 0)),
                pl.BlockSpec((n_seq, RWKV_COLS), lambda i: (jnp.maximum(i * per - 1, 0), 0)),
                _full(shift0.shape)] + [_full(a.shape) for a in consts]
    return pl.pallas_call(
        functools.partial(_rwkv_pre_kernel, n_seq=n_seq),
        grid=(rows // tile,),
        in_specs=in_specs,
        out_specs=[pl.BlockSpec((tile, RWKV_WIDTH), lambda i: (i, 0))] * 7,
        out_shape=[vec] * 7,
        compiler_params=_params(("arbitrary",)),
        name="rwkv_prologue",
    )(p_rows, p_rows, shift0, *consts)


def _sublane_allsum(p):
    p = p + pltpu.roll(p, 4, 0)
    p = p + pltpu.roll(p, 2, 0)
    return p + pltpu.roll(p, 1, 0)


def _pairs_from_rows(x, nb):
    cols = [x[:, c * LANES:(c + 1) * LANES] for c in range(RWKV_WIDTH // LANES)]
    m = jnp.concatenate(cols * (LANES // (4 * nb)), axis=0)
    mt = m.T
    lane = lax.broadcasted_iota(jnp.int32, (RWKV_HEAD, LANES), 1)
    even_head = (lane // nb) % RWKV_HEADS < RWKV_HEADS // 2
    return jnp.where(even_head, mt[:RWKV_HEAD], pltpu.roll(mt[RWKV_HEAD:], 4 * nb, 1))


def _rows_from_pairs(y, nb):
    pairs = RWKV_HEADS * nb
    shifts = [(-(copy * pairs + parity * pairs // 2)) % LANES
              for parity in range(2) for copy in range(LANES // pairs)]
    mt = jnp.concatenate([y if s == 0 else pltpu.roll(y, s, 1) for s in shifts], axis=0)
    m = mt.T
    return jnp.concatenate([m[c * nb:(c + 1) * nb, :] for c in range(RWKV_WIDTH // LANES)], axis=1)


def _rwkv_scan_kernel(r_ref, w_ref, k_ref, kk_ref, kka_ref, v_ref, s0_ref, y_ref, sout_ref, s_s, sa_s, ka_s, kb_s,
                      va_s, vb_s, *, steps, n_v, nb):
    tc = pl.program_id(1)

    @pl.when(tc == 0)
    def _():
        s_s[...] = s0_ref[...]

    row_id = lax.broadcasted_iota(jnp.int32, (SUBLANES, LANES), 0)
    lane_id = lax.broadcasted_iota(jnp.int32, (n_v, LANES), 1)
    split = lambda x: x.reshape(RWKV_HEAD // SUBLANES, SUBLANES, LANES)
    k_refs = (r_ref, w_ref, k_ref, kk_ref, kka_ref)

    def stage(t, k_s, v_s):
        for i, ref in enumerate(k_refs):
            k_s[i] = _pairs_from_rows(ref[t], nb)
        full = _pairs_from_rows(v_ref[t], nb)
        v_s[...] = full if n_v == RWKV_HEAD else jnp.where(lane_id < LANES // 2, full[:n_v], full[n_v:])

    def advance(t, k_s, v_s):
        kk = split(k_s[3])
        for v in range(n_v):
            sa_s[v] = -_sublane_allsum(jnp.sum(split(s_s[v]) * kk, axis=0))
        r, w, k, kka = split(k_s[0]), split(k_s[1]), split(k_s[2]), split(k_s[4])
        y_tiles = []
        for vb in range(n_v // SUBLANES):
            v_tile = v_s[vb * SUBLANES:(vb + 1) * SUBLANES, :]
            y_tile = jnp.zeros((SUBLANES, LANES), F32)
            for j in range(SUBLANES):
                v = vb * SUBLANES + j
                v_row = jnp.broadcast_to(v_tile[j:j + 1, :], (SUBLANES, LANES))
                s = split(s_s[v]) * w + sa_s[v][None] * kka + v_row[None] * k
                s_s[v] = s.reshape(RWKV_HEAD, LANES)
                y_tile = jnp.where(row_id == j, _sublane_allsum(jnp.sum(s * r, axis=0)), y_tile)
            y_tiles.append(y_tile)
        y_ref[t] = _rows_from_pairs(jnp.concatenate(y_tiles, axis=0), nb)

    stage(0, ka_s, va_s)

    def two_steps(i, carry):
        t = 2 * i
        stage(t + 1, kb_s, vb_s)
        advance(t, ka_s, va_s)
        stage(jnp.minimum(t + 2, steps - 1), ka_s, va_s)
        advance(t + 1, kb_s, vb_s)
        return carry

    lax.fori_loop(0, steps // 2, two_steps, 0)

    @pl.when(tc == pl.num_programs(1) - 1)
    def _():
        sout_ref[...] = s_s[...]


def _rwkv_scan_call(r, w, k, kk, kka, v, s0, n_seq, steps):
    rows = r.shape[0]
    n_t = rows // n_seq
    n_v, _, lanes = s0.shape
    nb = n_seq // (lanes // LANES)
    assert steps % 2 == 0 and n_t % steps == 0
    as_steps = lambda a: a.reshape(n_t, n_seq, RWKV_WIDTH)
    xspec = pl.BlockSpec((steps, nb, RWKV_WIDTH), lambda g, i: (i, g, 0))
    sspec = pl.BlockSpec((n_v, RWKV_HEAD, LANES), lambda g, i: (0, 0, g))
    stage_k = pltpu.VMEM((5, RWKV_HEAD, LANES), F32)
    stage_v = pltpu.VMEM((n_v, LANES), F32)
    y, s = pl.pallas_call(
        functools.partial(_rwkv_scan_kernel, steps=steps, n_v=n_v, nb=nb),
        grid=(lanes // LANES, n_t // steps),
        in_specs=[xspec] * 6 + [sspec],
        out_specs=[xspec, sspec],
        out_shape=[jax.ShapeDtypeStruct((n_t, n_seq, RWKV_WIDTH), F32), jax.ShapeDtypeStruct(s0.shape, F32)],
        scratch_shapes=[pltpu.VMEM((n_v, RWKV_HEAD, LANES), F32), pltpu.VMEM((n_v, SUBLANES, LANES), F32),
                        stage_k, stage_k, stage_v, stage_v],
        compiler_params=_params(("arbitrary", "arbitrary")),
        name="rwkv_recurrence",
    )(*[as_steps(a) for a in (r, w, k, kk, kka, v)], s0)
    return y.reshape(rows, RWKV_WIDTH), s


def _state_to_pairs(s, nb):
    n_seq = s.shape[0]
    v_split = LANES // (RWKV_HEADS * nb)
    n_v = RWKV_HEAD // v_split
    s = s.reshape(n_seq // nb, nb, RWKV_HEADS // 2, 2, v_split, n_v, RWKV_HEAD)
    return jnp.transpose(s, (5, 6, 0, 4, 3, 2, 1)).reshape(n_v, RWKV_HEAD, n_seq // nb * LANES)


def _state_from_pairs(s, n_seq, nb):
    v_split = LANES // (RWKV_HEADS * nb)
    n_v = RWKV_HEAD // v_split
    s = s.reshape(n_v, RWKV_HEAD, n_seq // nb, v_split, 2, RWKV_HEADS // 2, nb)
    return jnp.transpose(s, (2, 6, 5, 4, 3, 0, 1)).reshape(n_seq, RWKV_HEADS, RWKV_HEAD, RWKV_HEAD)


def _mixout_kernel(x_ref, ys5_ref, yrw_ref, r_ref, k_ref, v_ref, g_ref, mod_ref, gnw_ref, gnb_ref, rk_ref,
                   ones_ref, wout_ref, n2g_ref, x1_ref, h2_ref, *, n_seq):
    y = yrw_ref[...]
    inv_n = 1.0 / RWKV_HEAD
    mean = _head_sum(y, ones_ref) * inv_n
    yc = y - mean
    var = _head_sum(yc * yc, ones_ref) * inv_n
    y = yc * lax.rsqrt(var + GN_EPS) * gnw_ref[...] + gnb_ref[...]
    v = v_ref[...]
    y = y + _head_sum(r_ref[...] * k_ref[...] * rk_ref[...], ones_ref) * v
    y = y * g_ref[...]
    mix = jnp.concatenate([ys5_ref[...], y], axis=-1).astype(BF16)
    x1 = _time_major_rows(x_ref) + _gated(_dot(mix, wout_ref[...]), mod_ref[2], n_seq)
    x1_ref[...] = x1
    h2_ref[...] = _modulate(_rms(x1, n2g_ref[...]), mod_ref[3], mod_ref[4], n_seq)


def _mixout_call(x, y_s5, y_rw, r, k, v, g, mod6, consts, n_seq, tile):
    d = x.shape[-1]
    rows = x.size // d
    wide = pl.BlockSpec((tile, d), lambda i: (i, 0))
    half = pl.BlockSpec((tile, RWKV_WIDTH), lambda i: (i, 0))
    return pl.pallas_call(
        functools.partial(_mixout_kernel, n_seq=n_seq),
        grid=(rows // tile,),
        in_specs=[_token_spec(x, n_seq, tile)] + [half] * 6 + [_full(mod6.shape)] + [_full(a.shape) for a in consts],
        out_specs=[wide, wide],
        out_shape=[jax.ShapeDtypeStruct((rows, d), F32)] * 2,
        compiler_params=_params(("arbitrary",)),
        name="mixer_out_norm2",
    )(x, y_s5, y_rw, r, k, v, g, mod6, *consts)


def _sort16_pairs():
    pairs = []
    k = 2
    while k <= PEER_TOPK:
        j = k // 2
        while j >= 1:
            pairs += [(i, i ^ j, (i & k) == 0) for i in range(PEER_TOPK) if i ^ j > i]
            j //= 2
        k *= 2
    return pairs


def _bitonic_merge_desc(x):
    x = list(x)
    j = PEER_TOPK // 2
    while j >= 1:
        for i in range(PEER_TOPK):
            if not i & j:
                x[i], x[i | j] = jnp.maximum(x[i], x[i | j]), jnp.minimum(x[i], x[i | j])
        j //= 2
    return x


def _top16_desc(slabs):
    x = list(slabs)
    for i, l, i_max in _sort16_pairs():
        hi, lo = jnp.maximum(x[i], x[l]), jnp.minimum(x[i], x[l])
        x[i], x[l] = (hi, lo) if i_max else (lo, hi)
    return _merge_sublane_lists(x)


def _merge_top16(x, y):
    return _bitonic_merge_desc([jnp.maximum(x[v], y[PEER_TOPK - 1 - v]) for v in range(PEER_TOPK)])


def _merge_sublane_lists(x):
    for shift in (1, 2, 4):
        x = _merge_top16(x, [pltpu.roll(v, shift, 0) for v in x])
    return x


def _top16_pair_sums(a, b):
    sub = lax.broadcasted_iota(jnp.int32, a[0].shape, 0)
    b_lo = b[0]
    for s in range(1, SUBLANES):
        b_lo = jnp.where(sub == s, b[s], b_lo)
    low = [a[i] + b_lo for i in range(PEER_TOPK)]
    high = [a[0] + b[j] for j in range(SUBLANES, PEER_TOPK)]
    padded = high + [jnp.full_like(a[0], -jnp.inf)] * (PEER_TOPK - len(high))
    return _merge_top16(_merge_sublane_lists(low), padded), low, high


def _count_at_least(slabs, bound):
    count = jnp.zeros_like(bound)
    for s in slabs:
        count = count + jnp.where(s >= bound, 1.0, 0.0)
    return _sublane_allsum(count)


def _peer_select(xt, wq_ref, k1_ref, k2_ref, q_s, a_s, b_s, e1_s, e2_s, thr_s, tie_s):
    tg = xt.shape[1]
    q_s[...] = _dot_split_lhs(wq_ref[0], wq_ref[1], xt).reshape(PEER_HEADS, 2 * PEER_HALF, tg)
    slabs = lambda s: [s[v * SUBLANES:(v + 1) * SUBLANES, :] for v in range(N_KEYS // SUBLANES)]
    tiled = lambda row: jnp.concatenate([row] * (N_KEYS // SUBLANES), axis=0)

    def head(h, carry):
        q = q_s[h]
        s1 = _dot(k1_ref[h], q[:PEER_HALF], HIGHEST)
        s2 = _dot(k2_ref[h], q[PEER_HALF:], HIGHEST)
        top1 = _top16_desc(slabs(s1))
        top2 = _top16_desc(slabs(s2))
        topc, sums_low, sums_high = _top16_pair_sums(top1, top2)
        z = jnp.ones_like(topc[0])
        for i in range(1, PEER_TOPK):
            z = z + jnp.exp(topc[i] - topc[0])
        a = jnp.where(s1 >= tiled(top1[-1]), s1, -jnp.inf)
        b = jnp.where(s2 >= tiled(top2[-1]), s2, -jnp.inf)
        a_s[h] = a
        b_s[h] = b
        e1_s[h] = jnp.exp(a - tiled(top1[0])) / tiled(z)
        e2_s[h] = jnp.exp(b - tiled(top2[0]))
        thr_s[h] = topc[-1]
        n_pairs = _count_at_least(sums_low, topc[-1])
        for e in sums_high:
            n_pairs = n_pairs + jnp.where(e >= topc[-1], 1.0, 0.0)
        tied = jnp.where(n_pairs == PEER_TOPK, 0.0, 1.0)
        for values, bound in ((slabs(s1), top1[-1]), (slabs(s2), top2[-1])):
            tied = jnp.maximum(tied, jnp.where(_count_at_least(values, bound) == PEER_TOPK, 0.0, 1.0))
        tie_s[...] = jnp.maximum(tie_s[...], tied)
        return carry

    lax.fori_loop(0, PEER_HEADS, head, 0)


def _stable_top16(x, row_id):
    vals, rows = [], []
    for _ in range(PEER_TOPK):
        m = jnp.max(x, axis=0, keepdims=True)
        r = jnp.min(jnp.where(x == m, row_id, float(x.shape[0])), axis=0, keepdims=True)
        vals.append(m)
        rows.append(r)
        x = jnp.where(row_id == r, -jnp.inf, x)
    return vals, rows


def _peer_select_exact(k1_ref, k2_ref, q_s, a_s, b_s, e1_s, e2_s, thr_s, pa_s, pb_s, pthr_s, lst_s, cand_s):
    tg = q_s.shape[2]
    key_id = lax.broadcasted_iota(jnp.int32, (N_KEYS, tg), 0).astype(F32)
    pair_id = lax.broadcasted_iota(jnp.int32, (PEER_TOPK * PEER_TOPK, tg), 0).astype(F32)

    def head(h, carry):
        q = q_s[h]
        s1 = _dot(k1_ref[h], q[:PEER_HALF], HIGHEST)
        s2 = _dot(k2_ref[h], q[PEER_HALF:], HIGHEST)
        v1, i1 = _stable_top16(s1, key_id)
        v2, i2 = _stable_top16(s2, key_id)
        rank1 = jnp.full((N_KEYS, tg), float(PEER_TOPK), F32)
        rank2 = rank1
        for r in range(PEER_TOPK):
            rank1 = jnp.where(key_id == i1[r], float(r), rank1)
            rank2 = jnp.where(key_id == i2[r], float(r), rank2)
            lst_s[r:r + 1, :] = v2[r]
        best2 = lst_s[...]
        for r in range(PEER_TOPK):
            cand_s[r * PEER_TOPK:(r + 1) * PEER_TOPK, :] = v1[r] + best2
        vc, pc = _stable_top16(cand_s[...], pair_id)
        z = jnp.ones_like(vc[0])
        for r in range(1, PEER_TOPK):
            z = z + jnp.exp(vc[r] - vc[0])
        a = jnp.where(rank1 < PEER_TOPK, s1, -jnp.inf)
        b = jnp.where(rank2 < PEER_TOPK, s2, -jnp.inf)
        a_s[h] = a
        b_s[h] = b
        e1_s[h] = jnp.exp(a - v1[0]) / z
        e2_s[h] = jnp.exp(b - v2[0])
        pa_s[h] = rank1 * float(PEER_TOPK)
        pb_s[h] = rank2
        thr_s[h] = jnp.broadcast_to(vc[-1], (SUBLANES, tg))
        pthr_s[h] = jnp.broadcast_to(pc[-1], (SUBLANES, tg))
        return carry

    lax.fori_loop(0, PEER_HEADS, head, 0)


def _peer_gates(g, key0, act_ref, pt_ref, a_s, b_s, e1_s, e2_s, thr_s):
    tg = act_ref.shape[2]
    for lg in range(tg // LANES):
        ls = slice(lg * LANES, (lg + 1) * LANES)
        a_rows = [a_s[g, h, pl.ds(key0, PEER_KEYS_PER_BLOCK), ls] for h in range(PEER_HEADS)]
        e_rows = [e1_s[g, h, pl.ds(key0, PEER_KEYS_PER_BLOCK), ls] for h in range(PEER_HEADS)]
        for i in range(PEER_KEYS_PER_BLOCK):
            gate = jnp.zeros((N_KEYS, LANES), F32)
            for h in range(PEER_HEADS):
                score = b_s[g, h, :, ls] + a_rows[h][i:i + 1, :]
                weight = e2_s[g, h, :, ls] * e_rows[h][i:i + 1, :]
                gate = gate + jnp.where(score >= thr_s[g, h, 0:1, ls], weight, 0.0)
            act = act_ref[g, i * N_KEYS:(i + 1) * N_KEYS, ls]
            pt_ref[g, i * N_KEYS:(i + 1) * N_KEYS, ls] = (gate * act).astype(BF16)


def _peer_gates_exact(g, key0, act_ref, pt_ref, a_s, b_s, e1_s, e2_s, thr_s, pa_s, pb_s, pthr_s):
    tg = act_ref.shape[2]
    rows = pl.ds(key0, PEER_KEYS_PER_BLOCK)
    for lg in range(tg // LANES):
        ls = slice(lg * LANES, (lg + 1) * LANES)
        for i in range(PEER_KEYS_PER_BLOCK):
            gate = jnp.zeros((N_KEYS, LANES), F32)
            for h in range(PEER_HEADS):
                score = b_s[g, h, :, ls] + a_s[g, h, rows, ls][i:i + 1, :]
                pos = pb_s[g, h, :, ls] + pa_s[g, h, rows, ls][i:i + 1, :]
                weight = e2_s[g, h, :, ls] * e1_s[g, h, rows, ls][i:i + 1, :]
                thr = thr_s[g, h, 0:1, ls]
                tied_in = jnp.where(score == thr, jnp.where(pos <= pthr_s[g, h, 0:1, ls], weight, 0.0), 0.0)
                gate = gate + jnp.where(score > thr, weight, tied_in)
            act = act_ref[g, i * N_KEYS:(i + 1) * N_KEYS, ls]
            pt_ref[g, i * N_KEYS:(i + 1) * N_KEYS, ls] = (gate * act).astype(BF16)


def _first_match(cases):
    (pred, fn), rest = cases[0], cases[1:]

    def hit():
        fn()

    def miss():
        if rest:
            _first_match(rest)

    lax.cond(pred, hit, miss)


def _peer_kernel(h2_ref, x1_ref, ga_ref, wq_ref, k1_ref, k2_ref, u_ref, vt_ref, fg_ref, o_ref,
                 xt_s, q_s, a_s, b_s, e1_s, e2_s, thr_s, pa_s, pb_s, pthr_s, tie_s, lst_s, cand_s, tied_s,
                 act0_s, act1_s, pt0_s, pt1_s, acc_s, *, n_seq, n_blocks):
    s = pl.program_id(1)
    n_groups = xt_s.shape[0]
    tg = xt_s.shape[2]
    acts, pts = (act0_s, act1_s), (pt0_s, pt1_s)
    sel = (a_s, b_s, e1_s, e2_s, thr_s)
    sel_exact = sel + (pa_s, pb_s, pthr_s)

    def stages(parity, first, gates, second, exact=False):
        key0 = pl.multiple_of((s - 1) * PEER_KEYS_PER_BLOCK, PEER_KEYS_PER_BLOCK)

        def group(g, carry):
            if first:
                acts[parity][g] = _gelu(_dot(u_ref[...], xt_s[g]))
            if gates and exact:
                _peer_gates_exact(g, key0, acts[1 - parity], pts[1 - parity], *sel_exact)
            elif gates:
                _peer_gates(g, key0, acts[1 - parity], pts[1 - parity], *sel)
            if second:
                acc_s[g] += _dot(vt_ref[...], pts[parity][g])
            return carry

        lax.fori_loop(0, n_groups, group, 0)

    def first_step():
        tie_s[...] = jnp.zeros_like(tie_s)

        def prep(g, carry):
            t0 = pl.multiple_of(g * tg, tg)
            xt = h2_ref[pl.ds(t0, tg), :].T
            xt_s[g] = xt.astype(BF16)
            _peer_select(xt, wq_ref, k1_ref, k2_ref, q_s, a_s.at[g], b_s.at[g], e1_s.at[g], e2_s.at[g],
                         thr_s.at[g], tie_s)
            return carry

        lax.fori_loop(0, n_groups, prep, 0)
        tied_s[0] = (jnp.max(tie_s[...]) > 0.0).astype(jnp.int32)

        @pl.when(tied_s[0] != 0)
        def _():
            def prep_exact(g, carry):
                t0 = pl.multiple_of(g * tg, tg)
                xt = h2_ref[pl.ds(t0, tg), :].T
                q_s[...] = _dot_split_lhs(wq_ref[0], wq_ref[1], xt).reshape(q_s.shape)
                _peer_select_exact(k1_ref, k2_ref, q_s, a_s.at[g], b_s.at[g], e1_s.at[g], e2_s.at[g], thr_s.at[g],
                                   pa_s.at[g], pb_s.at[g], pthr_s.at[g], lst_s, cand_s)
                return carry

            lax.fori_loop(0, n_groups, prep_exact, 0)

        acc_s[...] = jnp.zeros_like(acc_s)
        pt1_s[...] = jnp.zeros_like(pt1_s)
        stages(0, True, False, False)

    def by_tie(parity, first, gates, second):
        return lambda: lax.cond(tied_s[0] == 0, lambda: stages(parity, first, gates, second, False),
                                lambda: stages(parity, first, gates, second, True))

    def last_step():
        stages((n_blocks + 1) % 2, False, False, True)
        for g in range(n_groups):
            rows = slice(g * tg, (g + 1) * tg)
            x2 = x1_ref[rows, :] + _gated(acc_s[g].T, ga_ref[...], min(n_seq, tg))
            y = _rms(x2, fg_ref[...])
            if len(o_ref.shape) == 2:
                o_ref[rows, :] = y
            else:
                sg = tg // n_seq
                o_ref[:, g * sg:(g + 1) * sg, :] = jnp.swapaxes(y.reshape(sg, n_seq, y.shape[1]), 0, 1)

    middle = (s >= 1) & (s < n_blocks)
    _first_match([(middle & (s % 2 == 0), by_tie(0, True, True, True)),
                  (middle, by_tie(1, True, True, True)),
                  (s == 0, first_step),
                  (s == n_blocks, by_tie(n_blocks % 2, False, True, True)),
                  (s == n_blocks + 1, last_step)])


def _peer_call(h2, x1, gate2, wq_t, keys1, keys2, u_bf, vt_bf, final_g, n_seq, tm, batch_major):
    rows, d = h2.shape
    n_exp = u_bf.shape[0]
    eb = PEER_KEYS_PER_BLOCK * N_KEYS
    n_blocks = n_exp // eb
    tg = PEER_TOKEN_GROUP
    n_groups = tm // tg
    assert tm % tg == 0 and (tg % n_seq == 0 or n_seq % tg == 0) and n_exp % eb == 0 and n_blocks >= 2
    if n_seq > tg:
        raise NotImplementedError("more than PEER_TOKEN_GROUP sequences per step")
    tok = pl.BlockSpec((tm, d), lambda i, s: (i, 0))
    out = jax.ShapeDtypeStruct((n_seq, rows // n_seq, d) if batch_major else (rows, d), F32)
    sel = pltpu.VMEM((n_groups, PEER_HEADS, N_KEYS, tg), F32)
    row = pltpu.VMEM((n_groups, PEER_HEADS, SUBLANES, tg), F32)
    act = pltpu.VMEM((n_groups, eb, tg), F32)
    pt = pltpu.VMEM((n_groups, eb, tg), BF16)
    return pl.pallas_call(
        functools.partial(_peer_kernel, n_seq=n_seq, n_blocks=n_blocks),
        grid=(rows // tm, n_blocks + 2),
        in_specs=[tok, tok, _full(gate2.shape), _full(wq_t.shape), _full(keys1.shape), _full(keys2.shape),
                  pl.BlockSpec((eb, d), lambda i, s: (jnp.minimum(s, n_blocks - 1), 0)),
                  pl.BlockSpec((d, eb), lambda i, s: (0, jnp.clip(s - 2, 0, n_blocks - 1))),
                  _full((1, d))],
        out_specs=_token_spec(out, n_seq, tm),
        out_shape=out,
        scratch_shapes=[pltpu.VMEM((n_groups, d, tg), BF16), pltpu.VMEM((PEER_HEADS, 2 * PEER_HALF, tg), F32),
                        sel, sel, sel, sel, row, sel, sel, row, pltpu.VMEM((SUBLANES, tg), F32),
                        pltpu.VMEM((PEER_TOPK, tg), F32), pltpu.VMEM((PEER_TOPK * PEER_TOPK, tg), F32),
                        pltpu.SMEM((1,), jnp.int32),
                        act, act, pt, pt, pltpu.VMEM((n_groups, d, tg), F32)],
        compiler_params=_params(("arbitrary", "arbitrary")),
        name="peer_final_norm",
    )(h2, x1, gate2, wq_t, keys1, keys2, u_bf, vt_bf, final_g.reshape(1, d))


def _pick(n, target):
    t = min(n, target)
    while n % t:
        t -= 1
    return t


def _layer(x_btd, mod, s5_re0, s5_im0, wkv0, shift0, prm):
    n_seq, n_t, d = x_btd.shape
    rows = n_seq * n_t
    mod6 = jnp.transpose(mod.reshape(n_seq, 6, d), (1, 0, 2))
    tile = n_seq * _pick(n_t, max(1, 512 // n_seq))
    reorder_in_kernel = (PEER_TOKEN_GROUP // n_seq) % SUBLANES == 0 and (tile // n_seq) % SUBLANES == 0
    x_in = x_btd if reorder_in_kernel else jnp.transpose(x_btd, (1, 0, 2)).reshape(rows, d)

    u_rows, p_rows = _inproj_call(x_in, mod6, prm["norm1_g"], prm["w_in_bf"], n_seq, tile)

    s5_steps = _pick(n_t, max(1, 512 // n_seq))
    y_s5, s5_re, s5_im = _s5_call(u_rows, s5_re0.reshape(n_seq, S5_LANES), s5_im0.reshape(n_seq, S5_LANES),
                                  prm["s5_consts"], n_seq, s5_steps)

    r, w, k, v, kk, kka, g = _rwkv_pre_call(p_rows, shift0, prm["rwkv_pre_consts"], n_seq, tile)
    nb = min(n_seq, LANES // RWKV_HEADS)
    y_rw, s_pairs = _rwkv_scan_call(r, w, k, kk, kka, v, _state_to_pairs(wkv0, nb), n_seq, _pick(n_t, 16))
    wkv = _state_from_pairs(s_pairs, n_seq, nb)

    x1, h2 = _mixout_call(x_in, y_s5, y_rw, r, k, v, g, mod6, prm["mixout_consts"], n_seq, tile)

    y = _peer_call(h2, x1, mod6[5], prm["peer_wq_t"], prm["peer_keys1"], prm["peer_keys2"], prm["peer_u_bf"],
                   prm["peer_vt_bf"], prm["final_norm_g"], n_seq, tile, reorder_in_kernel)
    if not reorder_in_kernel:
        y = jnp.transpose(y.reshape(n_t, n_seq, d), (1, 0, 2))
    shift = p_rows[rows - n_seq:]
    return (y, s5_re.reshape(n_seq, S5_GROUPS, S5_STATE), s5_im.reshape(n_seq, S5_GROUPS, S5_STATE), wkv, shift)


def kernel(x_prompt, x_sample, state_s5_re, state_s5_im, state_wkv, state_shift, c_prompt, c_sample, w_ada, b_ada, norm1_g, norm2_g, w_in, w_out, s5_a_re, s5_a_im, s5_log_dt, s5_b_re, s5_b_im, s5_c_re, s5_c_im, s5_d, w_glu, b_glu, rwkv_mu, rwkv_w0, rwkv_w2, rwkv_a0, rwkv_a2, rwkv_g2, rwkv_k_k, rwkv_k_a, rwkv_r_k, rwkv_gn_w, rwkv_gn_b, peer_w_q, peer_keys1, peer_keys2, peer_u, peer_v, final_norm_g):
    assert w_ada.shape[0] == 1, "single-layer model"
    nbp = x_prompt.shape[0]
    row = lambda a: a.reshape(1, -1)

    mod = _ada_call(jnp.concatenate([c_prompt, c_sample], axis=0).astype(F32), w_ada[0], b_ada[0])

    lb_re, lb_im, bw_re, bw_im = _s5_prep_call(s5_a_re[0], s5_a_im[0], s5_log_dt[0], s5_b_re[0], s5_b_im[0])
    head_ones = jnp.kron(jnp.eye(RWKV_HEADS, dtype=BF16), jnp.ones((RWKV_HEAD, RWKV_HEAD), BF16))
    lora_pad = jnp.zeros((RWKV_LORA // 2, RWKV_WIDTH), F32)
    prm = {
        "norm1_g": norm1_g[0],
        "w_in_bf": w_in[0].astype(BF16),
        "s5_consts": (lb_re, lb_im, bw_re, bw_im, _s5_out_blockdiag(s5_c_re[0]).astype(BF16),
                      _s5_out_blockdiag(s5_c_im[0]).astype(BF16), row(s5_d[0]), w_glu[0].astype(BF16),
                      row(b_glu[0])),
        "rwkv_pre_consts": (row(rwkv_mu[0]), row(rwkv_w0[0]),
                            _split_call(jnp.concatenate([rwkv_w2[0], lora_pad], axis=0)), row(rwkv_a0[0]),
                            _split_call(jnp.concatenate([lora_pad, rwkv_a2[0]], axis=0)), _split_call(rwkv_g2[0]),
                            row(rwkv_k_k[0]), row(rwkv_k_a[0]), head_ones),
        "mixout_consts": (row(rwkv_gn_w[0]), row(rwkv_gn_b[0]), row(rwkv_r_k[0]), head_ones,
                          w_out[0].astype(BF16), row(norm2_g[0])),
        "peer_wq_t": _split_call(peer_w_q[0].T),
        "peer_keys1": peer_keys1[0],
        "peer_keys2": peer_keys2[0],
        "peer_u_bf": peer_u[0].astype(BF16),
        "peer_vt_bf": peer_v[0].T.astype(BF16),
        "final_norm_g": final_norm_g,
    }

    z_s5 = jnp.zeros((nbp, S5_GROUPS, S5_STATE), F32)
    z_wkv = jnp.zeros((nbp, RWKV_HEADS, RWKV_HEAD, RWKV_HEAD), F32)
    z_sh = jnp.zeros((nbp, RWKV_COLS), F32)
    yp, pr, pi, pw, psh = _layer(x_prompt.astype(F32), mod[:nbp], z_s5, z_s5, z_wkv, z_sh, prm)
    ys, sr, si, sw, ssh = _layer(x_sample.astype(F32), mod[nbp:], state_s5_re[0].astype(F32),
                                 state_s5_im[0].astype(F32), state_wkv[0].astype(F32),
                                 state_shift[0].astype(F32), prm)
    return (yp.astype(x_prompt.dtype), ys.astype(x_sample.dtype), pr[None], pi[None], pw[None], psh[None],
            sr[None], si[None], sw[None], ssh[None])
```

```python
import functools

import jax
import jax.numpy as jnp
from jax import lax
from jax.experimental import pallas as pl
from jax.experimental.pallas import tpu as pltpu

F32 = jnp.float32
BF16 = jnp.bfloat16
HIGHEST = lax.Precision.HIGHEST

LANES = 128
SUBLANES = 8
VMEM_LIMIT_BYTES = 56 * 1024 * 1024

D_MODEL = 1024
S5_WIDTH = 512
S5_GROUP = 16
S5_GROUPS = 32
S5_STATE = 64
S5_LANES = S5_GROUPS * S5_STATE
S5_CHUNKS = 4
S5_CHUNK_IN = S5_WIDTH // S5_CHUNKS
S5_CHUNK_ST = S5_LANES // S5_CHUNKS
RWKV_WIDTH = 512
RWKV_HEAD = 64
RWKV_HEADS = 8
RWKV_LORA = 128
RWKV_COLS = 3 * RWKV_WIDTH + 64 + 64 + 128
PEER_HEADS = 8
N_KEYS = 128
PEER_TOPK = 16
PEER_HALF = 64
PEER_TOKEN_GROUP = 256
PEER_KEYS_PER_BLOCK = 8
NORM_EPS = 1e-6
GN_EPS = 64e-5


def _params(sem):
    return pltpu.CompilerParams(dimension_semantics=sem, vmem_limit_bytes=VMEM_LIMIT_BYTES)


def _full(shape):
    return pl.BlockSpec(shape, lambda *_: (0,) * len(shape))


def _dot(a, b, precision=None):
    return jnp.dot(a, b, precision=precision, preferred_element_type=F32)


def _split_bf16(x):
    hi = x.astype(BF16)
    return hi, (x - hi.astype(F32)).astype(BF16)


def _dot_split(a, b_hi, b_lo):
    a_hi, a_lo = _split_bf16(a)
    return _dot(a_hi, b_hi) + (_dot(a_lo, b_hi) + _dot(a_hi, b_lo))


def _dot_split_lhs(a_hi, a_lo, b):
    b_hi, b_lo = _split_bf16(b)
    return _dot(a_hi, b_hi) + (_dot(a_lo, b_hi) + _dot(a_hi, b_lo))


def _gelu(x):
    return 0.5 * x * (1.0 + lax.erf(x * (2.0 ** -0.5)))


def _rms(x, g):
    return x * lax.rsqrt(jnp.mean(x * x, axis=-1, keepdims=True) + NORM_EPS) * g


def _modulate(h, shift, scale, n_seq):
    rows, d = h.shape
    h3 = h.reshape(rows // n_seq, n_seq, d)
    return (h3 * (1.0 + scale)[None] + shift[None]).reshape(rows, d)


def _gated(h, gate, n_seq):
    rows, d = h.shape
    return (h.reshape(rows // n_seq, n_seq, d) * gate[None]).reshape(rows, d)


def _split_kernel(w_ref, o_ref):
    o_ref[0], o_ref[1] = _split_bf16(w_ref[...])


def _split_call(w):
    return pl.pallas_call(
        _split_kernel,
        out_shape=jax.ShapeDtypeStruct((2,) + w.shape, BF16),
        compiler_params=pltpu.CompilerParams(vmem_limit_bytes=VMEM_LIMIT_BYTES),
        name="split_weight",
    )(w)


def _ada_kernel(c_ref, w_ref, b_ref, o_ref):
    s = jax.nn.silu(c_ref[...])
    o_ref[...] = _dot(s, w_ref[...], HIGHEST) + b_ref[...]


def _ada_call(c_all, w_ada, b_ada):
    n, d = c_all.shape
    cols = w_ada.shape[1]
    return pl.pallas_call(
        _ada_kernel,
        grid=(cols // d,),
        in_specs=[_full((n, d)), pl.BlockSpec((d, d), lambda j: (0, j)), pl.BlockSpec((1, d), lambda j: (0, j))],
        out_specs=pl.BlockSpec((n, d), lambda j: (0, j)),
        out_shape=jax.ShapeDtypeStruct((n, cols), F32),
        compiler_params=_params(("arbitrary",)),
        name="adaln_mod",
    )(c_all, w_ada, b_ada.reshape(1, cols))


def _time_major_rows(x_ref):
    if len(x_ref.shape) == 2:
        return x_ref[...]
    n_seq, steps, d = x_ref.shape
    return jnp.swapaxes(x_ref[...], 0, 1).reshape(n_seq * steps, d)


def _token_spec(x, n_seq, tile):
    if x.ndim == 2:
        return pl.BlockSpec((tile, x.shape[1]), lambda i, *_: (i, 0))
    return pl.BlockSpec((n_seq, tile // n_seq, x.shape[2]), lambda i, *_: (0, i, 0))


def _inproj_kernel(x_ref, mod_ref, g_ref, w_ref, u_ref, p_ref, *, n_seq):
    h = _modulate(_rms(_time_major_rows(x_ref), g_ref[...]), mod_ref[0], mod_ref[1], n_seq)
    proj = _dot(h.astype(BF16), w_ref[...])
    u_ref[...] = proj[:, :S5_WIDTH]
    p_ref[...] = proj[:, S5_WIDTH:]


def _inproj_call(x, mod6, norm_g, w_in_bf, n_seq, tile):
    d = x.shape[-1]
    rows = x.size // d
    return pl.pallas_call(
        functools.partial(_inproj_kernel, n_seq=n_seq),
        grid=(rows // tile,),
        in_specs=[_token_spec(x, n_seq, tile), _full(mod6.shape), _full((1, d)),
                  _full(w_in_bf.shape)],
        out_specs=[pl.BlockSpec((tile, S5_WIDTH), lambda i: (i, 0)),
                   pl.BlockSpec((tile, RWKV_COLS), lambda i: (i, 0))],
        out_shape=[jax.ShapeDtypeStruct((rows, S5_WIDTH), F32), jax.ShapeDtypeStruct((rows, RWKV_COLS), F32)],
        compiler_params=_params(("arbitrary",)),
        name="norm1_inproj",
    )(x, mod6, norm_g.reshape(1, d), w_in_bf)


def _s5_prep_kernel(are_ref, aim_ref, ldt_ref, bre_ref, bim_ref, lbr_ref, lbi_ref, wre_ref, wim_ref):
    lam_re, lam_im = are_ref[...], aim_ref[...]
    dt = jnp.exp(ldt_ref[...])
    mag = jnp.exp(lam_re * dt)
    ang = lam_im * dt
    lb_re, lb_im = mag * jnp.cos(ang), mag * jnp.sin(ang)
    den = lam_re * lam_re + lam_im * lam_im
    n_re, n_im = lb_re - 1.0, lb_im
    coef_re = (n_re * lam_re + n_im * lam_im) / den
    coef_im = (n_im * lam_re - n_re * lam_im) / den
    lbr_ref[...] = lb_re
    lbi_ref[...] = lb_im
    for c in range(S5_CHUNKS):
        cr = coef_re[:, c * S5_CHUNK_ST:(c + 1) * S5_CHUNK_ST]
        ci = coef_im[:, c * S5_CHUNK_ST:(c + 1) * S5_CHUNK_ST]
        wre_ref[c] = (cr * bre_ref[c] - ci * bim_ref[c]).astype(BF16)
        wim_ref[c] = (cr * bim_ref[c] + ci * bre_ref[c]).astype(BF16)


def _s5_prep_call(a_re, a_im, log_dt, b_re, b_im):
    row = lambda a: a.reshape(1, S5_LANES)
    ldt = jnp.repeat(log_dt, S5_STATE).reshape(1, S5_LANES)
    w_shape = (S5_CHUNKS, S5_CHUNK_IN, S5_CHUNK_ST)
    return pl.pallas_call(
        _s5_prep_kernel,
        out_shape=[jax.ShapeDtypeStruct((1, S5_LANES), F32)] * 2 + [jax.ShapeDtypeStruct(w_shape, BF16)] * 2,
        compiler_params=pltpu.CompilerParams(vmem_limit_bytes=VMEM_LIMIT_BYTES),
        name="s5_discretise",
    )(row(a_re), row(a_im), ldt, _s5_in_blockdiag(b_re), _s5_in_blockdiag(b_im))


def _s5_in_blockdiag(b):
    gpc = S5_GROUPS // S5_CHUNKS
    bt = jnp.transpose(b, (0, 2, 1)).reshape(S5_CHUNKS, gpc, S5_GROUP, S5_STATE)
    eye = jnp.eye(gpc, dtype=b.dtype)
    bd = bt[:, :, :, None, :] * eye[None, :, None, :, None]
    return bd.reshape(S5_CHUNKS, S5_CHUNK_IN, S5_CHUNK_ST)


def _s5_out_blockdiag(c):
    gpc = S5_GROUPS // S5_CHUNKS
    ct = jnp.transpose(c, (0, 2, 1)).reshape(S5_CHUNKS, gpc, S5_STATE, S5_GROUP)
    eye = jnp.eye(gpc, dtype=c.dtype)
    bd = ct[:, :, :, None, :] * eye[None, :, None, :, None]
    return bd.reshape(S5_CHUNKS, S5_CHUNK_ST, S5_CHUNK_IN)


def _s5_kernel(u_ref, h0r_ref, h0i_ref, lbr_ref, lbi_ref, wre_ref, wim_ref, cre_ref, cim_ref, d_ref, wglu_ref,
               bglu_ref, y_ref, hr_out, hi_out, re_s, im_s, str_s, sti_s, *, n_seq, steps):
    chunk = pl.program_id(0)

    @pl.when(chunk == 0)
    def _():
        str_s[...] = h0r_ref[...]
        sti_s[...] = h0i_ref[...]

    u = u_ref[...]
    for c in range(S5_CHUNKS):
        uc = u[:, c * S5_CHUNK_IN:(c + 1) * S5_CHUNK_IN].astype(BF16)
        re_s[:, c * S5_CHUNK_ST:(c + 1) * S5_CHUNK_ST] = _dot(uc, wre_ref[c])
        im_s[:, c * S5_CHUNK_ST:(c + 1) * S5_CHUNK_ST] = _dot(uc, wim_ref[c])

    def seq_block(rb, carry):
        r0 = pl.multiple_of(rb * SUBLANES, SUBLANES)
        for c in range(S5_CHUNKS):
            lanes = slice(c * S5_CHUNK_ST, (c + 1) * S5_CHUNK_ST)
            lr = jnp.broadcast_to(lbr_ref[:, lanes], (SUBLANES, S5_CHUNK_ST))
            li = jnp.broadcast_to(lbi_ref[:, lanes], (SUBLANES, S5_CHUNK_ST))

            def step(t, h):
                hr, hi = h
                row = pl.multiple_of(t * n_seq + r0, SUBLANES)
                nr = lr * hr - li * hi + re_s[pl.ds(row, SUBLANES), lanes]
                ni = lr * hi + li * hr + im_s[pl.ds(row, SUBLANES), lanes]
                re_s[pl.ds(row, SUBLANES), lanes] = nr
                im_s[pl.ds(row, SUBLANES), lanes] = ni
                return nr, ni

            h0 = (str_s[pl.ds(r0, SUBLANES), lanes], sti_s[pl.ds(r0, SUBLANES), lanes])
            hr, hi = lax.fori_loop(0, steps, step, h0, unroll=min(steps, 8))
            str_s[pl.ds(r0, SUBLANES), lanes] = hr
            sti_s[pl.ds(r0, SUBLANES), lanes] = hi
        return carry

    lax.fori_loop(0, n_seq // SUBLANES, seq_block, 0)

    ys = []
    for c in range(S5_CHUNKS):
        lanes = slice(c * S5_CHUNK_ST, (c + 1) * S5_CHUNK_ST)
        ys.append(_dot(re_s[:, lanes].astype(BF16), cre_ref[c]) - _dot(im_s[:, lanes].astype(BF16), cim_ref[c]))
    y = jnp.concatenate(ys, axis=-1) + d_ref[...] * u
    y = _gelu(y)
    y_ref[...] = y * jax.nn.sigmoid(_dot(y.astype(BF16), wglu_ref[...]) + bglu_ref[...])

    @pl.when(chunk == pl.num_programs(0) - 1)
    def _():
        hr_out[...] = str_s[...]
        hi_out[...] = sti_s[...]


def _s5_call(u_rows, h0_re, h0_im, consts, n_seq, steps):
    rows = u_rows.shape[0]
    tile = n_seq * steps
    lb_re, lb_im, w_re, w_im, c_re, c_im, d_skip, w_glu_bf, b_glu = consts
    state = jax.ShapeDtypeStruct((n_seq, S5_LANES), F32)
    args = (u_rows, h0_re, h0_im, lb_re, lb_im, w_re, w_im, c_re, c_im, d_skip, w_glu_bf, b_glu)
    in_specs = [pl.BlockSpec((tile, S5_WIDTH), lambda i: (i, 0))] + [_full(a.shape) for a in args[1:]]
    return pl.pallas_call(
        functools.partial(_s5_kernel, n_seq=n_seq, steps=steps),
        grid=(rows // tile,),
        in_specs=in_specs,
        out_specs=[pl.BlockSpec((tile, S5_WIDTH), lambda i: (i, 0)), _full(state.shape), _full(state.shape)],
        out_shape=[jax.ShapeDtypeStruct((rows, S5_WIDTH), F32), state, state],
        scratch_shapes=[pltpu.VMEM((tile, S5_LANES), F32), pltpu.VMEM((tile, S5_LANES), F32),
                        pltpu.VMEM((n_seq, S5_LANES), F32), pltpu.VMEM((n_seq, S5_LANES), F32)],
        compiler_params=_params(("arbitrary",)),
        name="s5_mixer",
    )(*args)


def _head_sum(x, ones_ref):
    hi, lo = _split_bf16(x)
    return _dot(hi, ones_ref[...]) + _dot(lo, ones_ref[...])


def _rwkv_pre_kernel(p_ref, prev_ref, shift_ref, mu_ref, w0_ref, w2_ref, a0_ref, a2_ref, g2_ref, kk_ref, ka_ref,
                     ones_ref, r_out, w_out, k_out, v_out, kk_out, kka_out, g_out, *, n_seq):
    p = p_ref[...]
    tile = p.shape[0]
    head = jnp.where(pl.program_id(0) == 0, shift_ref[...], prev_ref[...])
    p_prev = head if tile == n_seq else jnp.concatenate([head, p[:tile - n_seq]], axis=0)
    ps = p + (p_prev - p) * mu_ref[...]
    w = RWKV_WIDTH
    r, k, v = ps[:, :w], ps[:, w:2 * w], ps[:, 2 * w:3 * w]
    lo = ps[:, 3 * w:3 * w + RWKV_LORA]
    g_lo = ps[:, 3 * w + RWKV_LORA:]
    w_raw = -jax.nn.softplus(-(w0_ref[...] + _dot_split(jnp.tanh(lo), w2_ref[0], w2_ref[1]))) - 0.5
    a = jax.nn.sigmoid(a0_ref[...] + _dot_split(lo, a2_ref[0], a2_ref[1]))
    kk = k * kk_ref[...]
    norm = jnp.sqrt(_head_sum(kk * kk, ones_ref))
    kk = kk / jnp.maximum(norm, 1e-12)
    r_out[...] = r
    w_out[...] = jnp.exp(-jnp.exp(w_raw))
    k_out[...] = k * (1.0 + (a - 1.0) * ka_ref[...])
    v_out[...] = v
    kk_out[...] = kk
    kka_out[...] = kk * a
    g_out[...] = _dot_split(jax.nn.sigmoid(g_lo), g2_ref[0], g2_ref[1])


def _rwkv_pre_call(p_rows, shift0, consts, n_seq, tile):
    rows = p_rows.shape[0]
    per = tile // n_seq
    vec = jax.ShapeDtypeStruct((rows, RWKV_WIDTH), F32)
    in_specs = [pl.BlockSpec((tile, RWKV_COLS), lambda i: (i, 0)),
                pl.BlockSpec((n_seq, RWKV_COLS), lambda i: (jnp.maximum(i * per - 1, 0), 0)),
                _full(shift0.shape)] + [_full(a.shape) for a in consts]
    return pl.pallas_call(
        functools.partial(_rwkv_pre_kernel, n_seq=n_seq),
        grid=(rows // tile,),
        in_specs=in_specs,
        out_specs=[pl.BlockSpec((tile, RWKV_WIDTH), lambda i: (i, 0))] * 7,
        out_shape=[vec] * 7,
        compiler_params=_params(("arbitrary",)),
        name="rwkv_prologue",
    )(p_rows, p_rows, shift0, *consts)


def _sublane_allsum(p):
    p = p + pltpu.roll(p, 4, 0)
    p = p + pltpu.roll(p, 2, 0)
    return p + pltpu.roll(p, 1, 0)


def _pairs_from_rows(x, nb):
    cols = [x[:, c * LANES:(c + 1) * LANES] for c in range(RWKV_WIDTH // LANES)]
    m = jnp.concatenate(cols * (LANES // (4 * nb)), axis=0)
    mt = m.T
    lane = lax.broadcasted_iota(jnp.int32, (RWKV_HEAD, LANES), 1)
    even_head = (lane // nb) % RWKV_HEADS < RWKV_HEADS // 2
    return jnp.where(even_head, mt[:RWKV_HEAD], pltpu.roll(mt[RWKV_HEAD:], 4 * nb, 1))


def _rows_from_pairs(y, nb):
    pairs = RWKV_HEADS * nb
    shifts = [(-(copy * pairs + parity * pairs // 2)) % LANES
              for parity in range(2) for copy in range(LANES // pairs)]
    mt = jnp.concatenate([y if s == 0 else pltpu.roll(y, s, 1) for s in shifts], axis=0)
    m = mt.T
    return jnp.concatenate([m[c * nb:(c + 1) * nb, :] for c in range(RWKV_WIDTH // LANES)], axis=1)


def _rwkv_scan_kernel(r_ref, w_ref, k_ref, kk_ref, kka_ref, v_ref, s0_ref, y_ref, sout_ref, s_s, sa_s, ka_s, kb_s,
                      va_s, vb_s, *, steps, n_v, nb):
    tc = pl.program_id(1)

    @pl.when(tc == 0)
    def _():
        s_s[...] = s0_ref[...]

    row_id = lax.broadcasted_iota(jnp.int32, (SUBLANES, LANES), 0)
    lane_id = lax.broadcasted_iota(jnp.int32, (n_v, LANES), 1)
    split = lambda x: x.reshape(RWKV_HEAD // SUBLANES, SUBLANES, LANES)
    k_refs = (r_ref, w_ref, k_ref, kk_ref, kka_ref)

    def stage(t, k_s, v_s):
        for i, ref in enumerate(k_refs):
            k_s[i] = _pairs_from_rows(ref[t], nb)
        full = _pairs_from_rows(v_ref[t], nb)
        v_s[...] = full if n_v == RWKV_HEAD else jnp.where(lane_id < LANES // 2, full[:n_v], full[n_v:])

    def advance(t, k_s, v_s):
        kk = split(k_s[3])
        for v in range(n_v):
            sa_s[v] = -_sublane_allsum(jnp.sum(split(s_s[v]) * kk, axis=0))
        r, w, k, kka = split(k_s[0]), split(k_s[1]), split(k_s[2]), split(k_s[4])
        y_tiles = []
        for vb in range(n_v // SUBLANES):
            v_tile = v_s[vb * SUBLANES:(vb + 1) * SUBLANES, :]
            y_tile = jnp.zeros((SUBLANES, LANES), F32)
            for j in range(SUBLANES):
                v = vb * SUBLANES + j
                v_row = jnp.broadcast_to(v_tile[j:j + 1, :], (SUBLANES, LANES))
                s = split(s_s[v]) * w + sa_s[v][None] * kka + v_row[None] * k
                s_s[v] = s.reshape(RWKV_HEAD, LANES)
                y_tile = jnp.where(row_id == j, _sublane_allsum(jnp.sum(s * r, axis=0)), y_tile)
            y_tiles.append(y_tile)
        y_ref[t] = _rows_from_pairs(jnp.concatenate(y_tiles, axis=0), nb)

    stage(0, ka_s, va_s)

    def two_steps(i, carry):
        t = 2 * i
        stage(t + 1, kb_s, vb_s)
        advance(t, ka_s, va_s)
        stage(jnp.minimum(t + 2, steps - 1), ka_s, va_s)
        advance(t + 1, kb_s, vb_s)
        return carry

    lax.fori_loop(0, steps // 2, two_steps, 0)

    @pl.when(tc == pl.num_programs(1) - 1)
    def _():
        sout_ref[...] = s_s[...]


def _rwkv_scan_call(r, w, k, kk, kka, v, s0, n_seq, steps):
    rows = r.shape[0]
    n_t = rows // n_seq
    n_v, _, lanes = s0.shape
    nb = n_seq // (lanes // LANES)
    assert steps % 2 == 0 and n_t % steps == 0
    as_steps = lambda a: a.reshape(n_t, n_seq, RWKV_WIDTH)
    xspec = pl.BlockSpec((steps, nb, RWKV_WIDTH), lambda g, i: (i, g, 0))
    sspec = pl.BlockSpec((n_v, RWKV_HEAD, LANES), lambda g, i: (0, 0, g))
    stage_k = pltpu.VMEM((5, RWKV_HEAD, LANES), F32)
    stage_v = pltpu.VMEM((n_v, LANES), F32)
    y, s = pl.pallas_call(
        functools.partial(_rwkv_scan_kernel, steps=steps, n_v=n_v, nb=nb),
        grid=(lanes // LANES, n_t // steps),
        in_specs=[xspec] * 6 + [sspec],
        out_specs=[xspec, sspec],
        out_shape=[jax.ShapeDtypeStruct((n_t, n_seq, RWKV_WIDTH), F32), jax.ShapeDtypeStruct(s0.shape, F32)],
        scratch_shapes=[pltpu.VMEM((n_v, RWKV_HEAD, LANES), F32), pltpu.VMEM((n_v, SUBLANES, LANES), F32),
                        stage_k, stage_k, stage_v, stage_v],
        compiler_params=_params(("arbitrary", "arbitrary")),
        name="rwkv_recurrence",
    )(*[as_steps(a) for a in (r, w, k, kk, kka, v)], s0)
    return y.reshape(rows, RWKV_WIDTH), s


def _state_to_pairs(s, nb):
    n_seq = s.shape[0]
    v_split = LANES // (RWKV_HEADS * nb)
    n_v = RWKV_HEAD // v_split
    s = s.reshape(n_seq // nb, nb, RWKV_HEADS // 2, 2, v_split, n_v, RWKV_HEAD)
    return jnp.transpose(s, (5, 6, 0, 4, 3, 2, 1)).reshape(n_v, RWKV_HEAD, n_seq // nb * LANES)


def _state_from_pairs(s, n_seq, nb):
    v_split = LANES // (RWKV_HEADS * nb)
    n_v = RWKV_HEAD // v_split
    s = s.reshape(n_v, RWKV_HEAD, n_seq // nb, v_split, 2, RWKV_HEADS // 2, nb)
    return jnp.transpose(s, (2, 6, 5, 4, 3, 0, 1)).reshape(n_seq, RWKV_HEADS, RWKV_HEAD, RWKV_HEAD)


def _mixout_kernel(x_ref, ys5_ref, yrw_ref, r_ref, k_ref, v_ref, g_ref, mod_ref, gnw_ref, gnb_ref, rk_ref,
                   ones_ref, wout_ref, n2g_ref, x1_ref, h2_ref, *, n_seq):
    y = yrw_ref[...]
    inv_n = 1.0 / RWKV_HEAD
    mean = _head_sum(y, ones_ref) * inv_n
    yc = y - mean
    var = _head_sum(yc * yc, ones_ref) * inv_n
    y = yc * lax.rsqrt(var + GN_EPS) * gnw_ref[...] + gnb_ref[...]
    v = v_ref[...]
    y = y + _head_sum(r_ref[...] * k_ref[...] * rk_ref[...], ones_ref) * v
    y = y * g_ref[...]
    mix = jnp.concatenate([ys5_ref[...], y], axis=-1).astype(BF16)
    x1 = _time_major_rows(x_ref) + _gated(_dot(mix, wout_ref[...]), mod_ref[2], n_seq)
    x1_ref[...] = x1
    h2_ref[...] = _modulate(_rms(x1, n2g_ref[...]), mod_ref[3], mod_ref[4], n_seq)


def _mixout_call(x, y_s5, y_rw, r, k, v, g, mod6, consts, n_seq, tile):
    d = x.shape[-1]
    rows = x.size // d
    wide = pl.BlockSpec((tile, d), lambda i: (i, 0))
    half = pl.BlockSpec((tile, RWKV_WIDTH), lambda i: (i, 0))
    return pl.pallas_call(
        functools.partial(_mixout_kernel, n_seq=n_seq),
        grid=(rows // tile,),
        in_specs=[_token_spec(x, n_seq, tile)] + [half] * 6 + [_full(mod6.shape)] + [_full(a.shape) for a in consts],
        out_specs=[wide, wide],
        out_shape=[jax.ShapeDtypeStruct((rows, d), F32)] * 2,
        compiler_params=_params(("arbitrary",)),
        name="mixer_out_norm2",
    )(x, y_s5, y_rw, r, k, v, g, mod6, *consts)


def _sort16_pairs():
    pairs = []
    k = 2
    while k <= PEER_TOPK:
        j = k // 2
        while j >= 1:
            pairs += [(i, i ^ j, (i & k) == 0) for i in range(PEER_TOPK) if i ^ j > i]
            j //= 2
        k *= 2
    return pairs


def _bitonic_merge_desc(x):
    x = list(x)
    j = PEER_TOPK // 2
    while j >= 1:
        for i in range(PEER_TOPK):
            if not i & j:
                x[i], x[i | j] = jnp.maximum(x[i], x[i | j]), jnp.minimum(x[i], x[i | j])
        j //= 2
    return x


def _top16_desc(slabs):
    x = list(slabs)
    for i, l, i_max in _sort16_pairs():
        hi, lo = jnp.maximum(x[i], x[l]), jnp.minimum(x[i], x[l])
        x[i], x[l] = (hi, lo) if i_max else (lo, hi)
    return _merge_sublane_lists(x)


def _merge_top16(x, y):
    return _bitonic_merge_desc([jnp.maximum(x[v], y[PEER_TOPK - 1 - v]) for v in range(PEER_TOPK)])


def _merge_sublane_lists(x):
    for shift in (1, 2, 4):
        x = _merge_top16(x, [pltpu.roll(v, shift, 0) for v in x])
    return x


def _top16_pair_sums(a, b):
    sub = lax.broadcasted_iota(jnp.int32, a[0].shape, 0)
    b_lo = b[0]
    for s in range(1, SUBLANES):
        b_lo = jnp.where(sub == s, b[s], b_lo)
    low = [a[i] + b_lo for i in range(PEER_TOPK)]
    high = [a[0] + b[j] for j in range(SUBLANES, PEER_TOPK)]
    padded = high + [jnp.full_like(a[0], -jnp.inf)] * (PEER_TOPK - len(high))
    return _merge_top16(_merge_sublane_lists(low), padded), low, high


def _count_at_least(slabs, bound):
    count = jnp.zeros_like(bound)
    for s in slabs:
        count = count + jnp.where(s >= bound, 1.0, 0.0)
    return _sublane_allsum(count)


def _peer_select(xt, wq_ref, k1_ref, k2_ref, q_s, a_s, b_s, e1_s, e2_s, thr_s, tie_s):
    tg = xt.shape[1]
    q_s[...] = _dot_split_lhs(wq_ref[0], wq_ref[1], xt).reshape(PEER_HEADS, 2 * PEER_HALF, tg)
    slabs = lambda s: [s[v * SUBLANES:(v + 1) * SUBLANES, :] for v in range(N_KEYS // SUBLANES)]
    tiled = lambda row: jnp.concatenate([row] * (N_KEYS // SUBLANES), axis=0)

    def head(h, carry):
        q = q_s[h]
        s1 = _dot(k1_ref[h], q[:PEER_HALF], HIGHEST)
        s2 = _dot(k2_ref[h], q[PEER_HALF:], HIGHEST)
        top1 = _top16_desc(slabs(s1))
        top2 = _top16_desc(slabs(s2))
        topc, sums_low, sums_high = _top16_pair_sums(top1, top2)
        z = jnp.ones_like(topc[0])
        for i in range(1, PEER_TOPK):
            z = z + jnp.exp(topc[i] - topc[0])
        a = jnp.where(s1 >= tiled(top1[-1]), s1, -jnp.inf)
        b = jnp.where(s2 >= tiled(top2[-1]), s2, -jnp.inf)
        a_s[h] = a
        b_s[h] = b
        e1_s[h] = jnp.exp(a - tiled(top1[0])) / tiled(z)
        e2_s[h] = jnp.exp(b - tiled(top2[0]))
        thr_s[h] = topc[-1]
        n_pairs = _count_at_least(sums_low, topc[-1])
        for e in sums_high:
            n_pairs = n_pairs + jnp.where(e >= topc[-1], 1.0, 0.0)
        tied = jnp.where(n_pairs == PEER_TOPK, 0.0, 1.0)
        for values, bound in ((slabs(s1), top1[-1]), (slabs(s2), top2[-1])):
            tied = jnp.maximum(tied, jnp.where(_count_at_least(values, bound) == PEER_TOPK, 0.0, 1.0))
        tie_s[...] = jnp.maximum(tie_s[...], tied)
        return carry

    lax.fori_loop(0, PEER_HEADS, head, 0)


def _stable_top16(x, row_id):
    vals, rows = [], []
    for _ in range(PEER_TOPK):
        m = jnp.max(x, axis=0, keepdims=True)
        r = jnp.min(jnp.where(x == m, row_id, float(x.shape[0])), axis=0, keepdims=True)
        vals.append(m)
        rows.append(r)
        x = jnp.where(row_id == r, -jnp.inf, x)
    return vals, rows


def _peer_select_exact(k1_ref, k2_ref, q_s, a_s, b_s, e1_s, e2_s, thr_s, pa_s, pb_s, pthr_s, lst_s, cand_s):
    tg = q_s.shape[2]
    key_id = lax.broadcasted_iota(jnp.int32, (N_KEYS, tg), 0).astype(F32)
    pair_id = lax.broadcasted_iota(jnp.int32, (PEER_TOPK * PEER_TOPK, tg), 0).astype(F32)

    def head(h, carry):
        q = q_s[h]
        s1 = _dot(k1_ref[h], q[:PEER_HALF], HIGHEST)
        s2 = _dot(k2_ref[h], q[PEER_HALF:], HIGHEST)
        v1, i1 = _stable_top16(s1, key_id)
        v2, i2 = _stable_top16(s2, key_id)
        rank1 = jnp.full((N_KEYS, tg), float(PEER_TOPK), F32)
        rank2 = rank1
        for r in range(PEER_TOPK):
            rank1 = jnp.where(key_id == i1[r], float(r), rank1)
            rank2 = jnp.where(key_id == i2[r], float(r), rank2)
            lst_s[r:r + 1, :] = v2[r]
        best2 = lst_s[...]
        for r in range(PEER_TOPK):
            cand_s[r * PEER_TOPK:(r + 1) * PEER_TOPK, :] = v1[r] + best2
        vc, pc = _stable_top16(cand_s[...], pair_id)
        z = jnp.ones_like(vc[0])
        for r in range(1, PEER_TOPK):
            z = z + jnp.exp(vc[r] - vc[0])
        a = jnp.where(rank1 < PEER_TOPK, s1, -jnp.inf)
        b = jnp.where(rank2 < PEER_TOPK, s2, -jnp.inf)
        a_s[h] = a
        b_s[h] = b
        e1_s[h] = jnp.exp(a - v1[0]) / z
        e2_s[h] = jnp.exp(b - v2[0])
        pa_s[h] = rank1 * float(PEER_TOPK)
        pb_s[h] = rank2
        thr_s[h] = jnp.broadcast_to(vc[-1], (SUBLANES, tg))
        pthr_s[h] = jnp.broadcast_to(pc[-1], (SUBLANES, tg))
        return carry

    lax.fori_loop(0, PEER_HEADS, head, 0)


def _peer_gates(g, key0, act_ref, pt_ref, a_s, b_s, e1_s, e2_s, thr_s):
    tg = act_ref.shape[2]
    for lg in range(tg // LANES):
        ls = slice(lg * LANES, (lg + 1) * LANES)
        a_rows = [a_s[g, h, pl.ds(key0, PEER_KEYS_PER_BLOCK), ls] for h in range(PEER_HEADS)]
        e_rows = [e1_s[g, h, pl.ds(key0, PEER_KEYS_PER_BLOCK), ls] for h in range(PEER_HEADS)]
        for i in range(PEER_KEYS_PER_BLOCK):
            gate = jnp.zeros((N_KEYS, LANES), F32)
            for h in range(PEER_HEADS):
                score = b_s[g, h, :, ls] + a_rows[h][i:i + 1, :]
                weight = e2_s[g, h, :, ls] * e_rows[h][i:i + 1, :]
                gate = gate + jnp.where(score >= thr_s[g, h, 0:1, ls], weight, 0.0)
            act = act_ref[g, i * N_KEYS:(i + 1) * N_KEYS, ls]
            pt_ref[g, i * N_KEYS:(i + 1) * N_KEYS, ls] = (gate * act).astype(BF16)


def _peer_gates_exact(g, key0, act_ref, pt_ref, a_s, b_s, e1_s, e2_s, thr_s, pa_s, pb_s, pthr_s):
    tg = act_ref.shape[2]
    rows = pl.ds(key0, PEER_KEYS_PER_BLOCK)
    for lg in range(tg // LANES):
        ls = slice(lg * LANES, (lg + 1) * LANES)
        for i in range(PEER_KEYS_PER_BLOCK):
            gate = jnp.zeros((N_KEYS, LANES), F32)
            for h in range(PEER_HEADS):
                score = b_s[g, h, :, ls] + a_s[g, h, rows, ls][i:i + 1, :]
                pos = pb_s[g, h, :, ls] + pa_s[g, h, rows, ls][i:i + 1, :]
                weight = e2_s[g, h, :, ls] * e1_s[g, h, rows, ls][i:i + 1, :]
                thr = thr_s[g, h, 0:1, ls]
                tied_in = jnp.where(score == thr, jnp.where(pos <= pthr_s[g, h, 0:1, ls], weight, 0.0), 0.0)
                gate = gate + jnp.where(score > thr, weight, tied_in)
            act = act_ref[g, i * N_KEYS:(i + 1) * N_KEYS, ls]
            pt_ref[g, i * N_KEYS:(i + 1) * N_KEYS, ls] = (gate * act).astype(BF16)


def _first_match(cases):
    (pred, fn), rest = cases[0], cases[1:]

    def hit():
        fn()

    def miss():
        if rest:
            _first_match(rest)

    lax.cond(pred, hit, miss)


def _peer_kernel(h2_ref, x1_ref, ga_ref, wq_ref, k1_ref, k2_ref, u_ref, vt_ref, fg_ref, o_ref,
                 xt_s, q_s, a_s, b_s, e1_s, e2_s, thr_s, pa_s, pb_s, pthr_s, tie_s, lst_s, cand_s, tied_s,
                 act0_s, act1_s, pt0_s, pt1_s, acc_s, *, n_seq, n_blocks):
    s = pl.program_id(1)
    n_groups = xt_s.shape[0]
    tg = xt_s.shape[2]
    acts, pts = (act0_s, act1_s), (pt0_s, pt1_s)
    sel = (a_s, b_s, e1_s, e2_s, thr_s)
    sel_exact = sel + (pa_s, pb_s, pthr_s)

    def stages(parity, first, gates, second, exact=False):
        key0 = pl.multiple_of((s - 1) * PEER_KEYS_PER_BLOCK, PEER_KEYS_PER_BLOCK)

        def group(g, carry):
            if first:
                acts[parity][g] = _gelu(_dot(u_ref[...], xt_s[g]))
            if gates and exact:
                _peer_gates_exact(g, key0, acts[1 - parity], pts[1 - parity], *sel_exact)
            elif gates:
                _peer_gates(g, key0, acts[1 - parity], pts[1 - parity], *sel)
            if second:
                acc_s[g] += _dot(vt_ref[...], pts[parity][g])
            return carry

        lax.fori_loop(0, n_groups, group, 0)

    def first_step():
        tie_s[...] = jnp.zeros_like(tie_s)

        def prep(g, carry):
            t0 = pl.multiple_of(g * tg, tg)
            xt = h2_ref[pl.ds(t0, tg), :].T
            xt_s[g] = xt.astype(BF16)
            _peer_select(xt, wq_ref, k1_ref, k2_ref, q_s, a_s.at[g], b_s.at[g], e1_s.at[g], e2_s.at[g],
                         thr_s.at[g], tie_s)
            return carry

        lax.fori_loop(0, n_groups, prep, 0)
        tied_s[0] = (jnp.max(tie_s[...]) > 0.0).astype(jnp.int32)

        @pl.when(tied_s[0] != 0)
        def _():
            def prep_exact(g, carry):
                t0 = pl.multiple_of(g * tg, tg)
                xt = h2_ref[pl.ds(t0, tg), :].T
                q_s[...] = _dot_split_lhs(wq_ref[0], wq_ref[1], xt).reshape(q_s.shape)
                _peer_select_exact(k1_ref, k2_ref, q_s, a_s.at[g], b_s.at[g], e1_s.at[g], e2_s.at[g], thr_s.at[g],
                                   pa_s.at[g], pb_s.at[g], pthr_s.at[g], lst_s, cand_s)
                return carry

            lax.fori_loop(0, n_groups, prep_exact, 0)

        acc_s[...] = jnp.zeros_like(acc_s)
        pt1_s[...] = jnp.zeros_like(pt1_s)
        stages(0, True, False, False)

    def by_tie(parity, first, gates, second):
        return lambda: lax.cond(tied_s[0] == 0, lambda: stages(parity, first, gates, second, False),
                                lambda: stages(parity, first, gates, second, True))

    def last_step():
        stages((n_blocks + 1) % 2, False, False, True)
        for g in range(n_groups):
            rows = slice(g * tg, (g + 1) * tg)
            x2 = x1_ref[rows, :] + _gated(acc_s[g].T, ga_ref[...], min(n_seq, tg))
            y = _rms(x2, fg_ref[...])
            if len(o_ref.shape) == 2:
                o_ref[rows, :] = y
            else:
                sg = tg // n_seq
                o_ref[:, g * sg:(g + 1) * sg, :] = jnp.swapaxes(y.reshape(sg, n_seq, y.shape[1]), 0, 1)

    middle = (s >= 1) & (s < n_blocks)
    _first_match([(middle & (s % 2 == 0), by_tie(0, True, True, True)),
                  (middle, by_tie(1, True, True, True)),
                  (s == 0, first_step),
                  (s == n_blocks, by_tie(n_blocks % 2, False, True, True)),
                  (s == n_blocks + 1, last_step)])


def _peer_call(h2, x1, gate2, wq_t, keys1, keys2, u_bf, vt_bf, final_g, n_seq, tm, batch_major):
    rows, d = h2.shape
    n_exp = u_bf.shape[0]
    eb = PEER_KEYS_PER_BLOCK * N_KEYS
    n_blocks = n_exp // eb
    tg = PEER_TOKEN_GROUP
    n_groups = tm // tg
    assert tm % tg == 0 and (tg % n_seq == 0 or n_seq % tg == 0) and n_exp % eb == 0 and n_blocks >= 2
    if n_seq > tg:
        raise NotImplementedError("more than PEER_TOKEN_GROUP sequences per step")
    tok = pl.BlockSpec((tm, d), lambda i, s: (i, 0))
    out = jax.ShapeDtypeStruct((n_seq, rows // n_seq, d) if batch_major else (rows, d), F32)
    sel = pltpu.VMEM((n_groups, PEER_HEADS, N_KEYS, tg), F32)
    row = pltpu.VMEM((n_groups, PEER_HEADS, SUBLANES, tg), F32)
    act = pltpu.VMEM((n_groups, eb, tg), F32)
    pt = pltpu.VMEM((n_groups, eb, tg), BF16)
    return pl.pallas_call(
        functools.partial(_peer_kernel, n_seq=n_seq, n_blocks=n_blocks),
        grid=(rows // tm, n_blocks + 2),
        in_specs=[tok, tok, _full(gate2.shape), _full(wq_t.shape), _full(keys1.shape), _full(keys2.shape),
                  pl.BlockSpec((eb, d), lambda i, s: (jnp.minimum(s, n_blocks - 1), 0)),
                  pl.BlockSpec((d, eb), lambda i, s: (0, jnp.clip(s - 2, 0, n_blocks - 1))),
                  _full((1, d))],
        out_specs=_token_spec(out, n_seq, tm),
        out_shape=out,
        scratch_shapes=[pltpu.VMEM((n_groups, d, tg), BF16), pltpu.VMEM((PEER_HEADS, 2 * PEER_HALF, tg), F32),
                        sel, sel, sel, sel, row, sel, sel, row, pltpu.VMEM((SUBLANES, tg), F32),
                        pltpu.VMEM((PEER_TOPK, tg), F32), pltpu.VMEM((PEER_TOPK * PEER_TOPK, tg), F32),
                        pltpu.SMEM((1,), jnp.int32),
                        act, act, pt, pt, pltpu.VMEM((n_groups, d, tg), F32)],
        compiler_params=_params(("arbitrary", "arbitrary")),
        name="peer_final_norm",
    )(h2, x1, gate2, wq_t, keys1, keys2, u_bf, vt_bf, final_g.reshape(1, d))


def _pick(n, target):
    t = min(n, target)
    while n % t:
        t -= 1
    return t


def _layer(x_btd, mod, s5_re0, s5_im0, wkv0, shift0, prm):
    n_seq, n_t, d = x_btd.shape
    rows = n_seq * n_t
    mod6 = jnp.transpose(mod.reshape(n_seq, 6, d), (1, 0, 2))
    tile = n_seq * _pick(n_t, max(1, 512 // n_seq))
    reorder_in_kernel = (PEER_TOKEN_GROUP // n_seq) % SUBLANES == 0 and (tile // n_seq) % SUBLANES == 0
    x_in = x_btd if reorder_in_kernel else jnp.transpose(x_btd, (1, 0, 2)).reshape(rows, d)

    u_rows, p_rows = _inproj_call(x_in, mod6, prm["norm1_g"], prm["w_in_bf"], n_seq, tile)

    s5_steps = _pick(n_t, max(1, 1024 // n_seq))
    y_s5, s5_re, s5_im = _s5_call(u_rows, s5_re0.reshape(n_seq, S5_LANES), s5_im0.reshape(n_seq, S5_LANES),
                                  prm["s5_consts"], n_seq, s5_steps)

    r, w, k, v, kk, kka, g = _rwkv_pre_call(p_rows, shift0, prm["rwkv_pre_consts"], n_seq, tile)
    nb = min(n_seq, LANES // RWKV_HEADS)
    y_rw, s_pairs = _rwkv_scan_call(r, w, k, kk, kka, v, _state_to_pairs(wkv0, nb), n_seq, _pick(n_t, 32))
    wkv = _state_from_pairs(s_pairs, n_seq, nb)

    x1, h2 = _mixout_call(x_in, y_s5, y_rw, r, k, v, g, mod6, prm["mixout_consts"], n_seq, tile)

    y = _peer_call(h2, x1, mod6[5], prm["peer_wq_t"], prm["peer_keys1"], prm["peer_keys2"], prm["peer_u_bf"],
                   prm["peer_vt_bf"], prm["final_norm_g"], n_seq, tile, reorder_in_kernel)
    if not reorder_in_kernel:
        y = jnp.transpose(y.reshape(n_t, n_seq, d), (1, 0, 2))
    shift = p_rows[rows - n_seq:]
    return (y, s5_re.reshape(n_seq, S5_GROUPS, S5_STATE), s5_im.reshape(n_seq, S5_GROUPS, S5_STATE), wkv, shift)


def kernel(x_prompt, x_sample, state_s5_re, state_s5_im, state_wkv, state_shift, c_prompt, c_sample, w_ada, b_ada, norm1_g, norm2_g, w_in, w_out, s5_a_re, s5_a_im, s5_log_dt, s5_b_re, s5_b_im, s5_c_re, s5_c_im, s5_d, w_glu, b_glu, rwkv_mu, rwkv_w0, rwkv_w2, rwkv_a0, rwkv_a2, rwkv_g2, rwkv_k_k, rwkv_k_a, rwkv_r_k, rwkv_gn_w, rwkv_gn_b, peer_w_q, peer_keys1, peer_keys2, peer_u, peer_v, final_norm_g):
    assert w_ada.shape[0] == 1, "single-layer model"
    nbp = x_prompt.shape[0]
    row = lambda a: a.reshape(1, -1)

    mod = _ada_call(jnp.concatenate([c_prompt, c_sample], axis=0).astype(F32), w_ada[0], b_ada[0])

    lb_re, lb_im, bw_re, bw_im = _s5_prep_call(s5_a_re[0], s5_a_im[0], s5_log_dt[0], s5_b_re[0], s5_b_im[0])
    head_ones = jnp.kron(jnp.eye(RWKV_HEADS, dtype=BF16), jnp.ones((RWKV_HEAD, RWKV_HEAD), BF16))
    lora_pad = jnp.zeros((RWKV_LORA // 2, RWKV_WIDTH), F32)
    prm = {
        "norm1_g": norm1_g[0],
        "w_in_bf": w_in[0].astype(BF16),
        "s5_consts": (lb_re, lb_im, bw_re, bw_im, _s5_out_blockdiag(s5_c_re[0]).astype(BF16),
                      _s5_out_blockdiag(s5_c_im[0]).astype(BF16), row(s5_d[0]), w_glu[0].astype(BF16),
                      row(b_glu[0])),
        "rwkv_pre_consts": (row(rwkv_mu[0]), row(rwkv_w0[0]),
                            _split_call(jnp.concatenate([rwkv_w2[0], lora_pad], axis=0)), row(rwkv_a0[0]),
                            _split_call(jnp.concatenate([lora_pad, rwkv_a2[0]], axis=0)), _split_call(rwkv_g2[0]),
                            row(rwkv_k_k[0]), row(rwkv_k_a[0]), head_ones),
        "mixout_consts": (row(rwkv_gn_w[0]), row(rwkv_gn_b[0]), row(rwkv_r_k[0]), head_ones,
                          w_out[0].astype(BF16), row(norm2_g[0])),
        "peer_wq_t": _split_call(peer_w_q[0].T),
        "peer_keys1": peer_keys1[0],
        "peer_keys2": peer_keys2[0],
        "peer_u_bf": peer_u[0].astype(BF16),
        "peer_vt_bf": peer_v[0].T.astype(BF16),
        "final_norm_g": final_norm_g,
    }

    z_s5 = jnp.zeros((nbp, S5_GROUPS, S5_STATE), F32)
    z_wkv = jnp.zeros((nbp, RWKV_HEADS, RWKV_HEAD, RWKV_HEAD), F32)
    z_sh = jnp.zeros((nbp, RWKV_COLS), F32)
    yp, pr, pi, pw, psh = _layer(x_prompt.astype(F32), mod[:nbp], z_s5, z_s5, z_wkv, z_sh, prm)
    ys, sr, si, sw, ssh = _layer(x_sample.astype(F32), mod[nbp:], state_s5_re[0].astype(F32),
                                 state_s5_im[0].astype(F32), state_wkv[0].astype(F32),
                                 state_shift[0].astype(F32), prm)
    return (yp.astype(x_prompt.dtype), ys.astype(x_sample.dtype), pr[None], pi[None], pw[None], psh[None],
            sr[None], si[None], sw[None], ssh[None])
```
